```python
import jax, jax.numpy as jnp
from jax import lax
import numpy as np

D_MODEL = 1024
BATCH = 16
SEQ = 256
DEPTH = 4
DEC_BATCH = 2
DEC_SEQ = 4096
PAST_LEN = 256

GRID_W = 64
N_MIXERS = 2
N_FOURIER_LAYERS = (DEPTH + 1) // 2
N_NA_LAYERS = DEPTH // 2
N_HEADS = 16
HEAD_DIM = D_MODEL // N_HEADS
N_FOURIER_GROUPS = 8
FOURIER_GROUP_DIM = D_MODEL // N_FOURIER_GROUPS
WIN_ROWS_MAX = 8
WIN_COLS = 16
D_FF = 2816
N_MOD = 9
RMS_EPS = 1e-6

kernel_name = "hybrid_fnet_natten_macaron_step"


def _rmsnorm(x, g):
    x32 = x.astype(jnp.float32)
    y = x32 * lax.rsqrt(jnp.mean(x32 * x32, axis=-1, keepdims=True) + RMS_EPS)
    return (y * g.astype(jnp.float32)).astype(x.dtype)


def _modulation(cond, w_mod, b_mod):
    m = jax.nn.silu(cond) @ w_mod + b_mod
    return m.reshape(cond.shape[0], N_MOD, D_MODEL)


def _modulated_prenorm(x, g, m, k):
    return _rmsnorm(x, g) * (1 + m[:, 3 * k + 1, None]) + m[:, 3 * k, None]


def _swiglu(h, w1, w2):
    gate, up = jnp.split(h @ w1, 2, axis=-1)
    return (jax.nn.silu(gate) * up) @ w2


def _half_ffn(x, m, k, g_pre, g_post, w1, w2):
    h = _modulated_prenorm(x, g_pre, m, k)
    return x + 0.5 * m[:, 3 * k + 2, None] * _rmsnorm(_swiglu(h, w1, w2), g_post)


def _fourier_mix(h, w_in, w_out):
    b, s, _ = h.shape
    u = (h @ w_in).reshape(b, s, N_FOURIER_GROUPS, FOURIER_GROUP_DIM)
    f = jnp.fft.fft2(u.astype(jnp.float32), axes=(1, 3), norm="ortho").real
    return f.astype(h.dtype).reshape(b, s, D_MODEL) @ w_out


def _split_heads(t):
    b, s, _ = t.shape
    return t.reshape(b, s, N_HEADS, HEAD_DIM).transpose(0, 2, 1, 3)


def _merge_heads(t):
    b, h, s, dh = t.shape
    return t.transpose(0, 2, 1, 3).reshape(b, s, h * dh)


def _qkv(h, w_qkv):
    q, k, v = jnp.split(h @ w_qkv, 3, axis=-1)
    return _split_heads(q) * (HEAD_DIM ** -0.5), _split_heads(k), _split_heads(v)


def _context_attention(q, k, v):
    s = jnp.einsum('bhqd,bhkd->bhqk', q, k).astype(jnp.float32)
    p = jax.nn.softmax(s, axis=-1).astype(v.dtype)
    return jnp.einsum('bhqk,bhkd->bhqd', p, v)


def _neighbourhood_attention(q, k, v, k_ctx, v_ctx, rpb, rows):
    b, h, s, dh = q.shape
    kh = min(WIN_ROWS_MAX, rows)
    qg = q.reshape(b, h, rows, GRID_W, dh)
    kg = k.reshape(b, h, rows, GRID_W, dh)
    vg = v.reshape(b, h, rows, GRID_W, dh)
    cols = jnp.arange(GRID_W)
    col_start = jnp.clip(cols - WIN_COLS // 2, 0, GRID_W - WIN_COLS)
    col_idx = col_start[:, None] + jnp.arange(WIN_COLS)[None, :]
    col_off = col_idx - cols[:, None] + (WIN_COLS - 1)
    n_loc = kh * WIN_COLS

    def row_block(r):
        rs = jnp.clip(r - kh // 2, 0, rows - kh)
        q_r = lax.dynamic_index_in_dim(qg, r, axis=2, keepdims=False)
        k_rows = lax.dynamic_slice_in_dim(kg, rs, kh, axis=2)
        v_rows = lax.dynamic_slice_in_dim(vg, rs, kh, axis=2)
        k_win = jnp.take(k_rows, col_idx, axis=3)
        v_win = jnp.take(v_rows, col_idx, axis=3)
        bias_rows = lax.dynamic_slice_in_dim(rpb, rs - r + (WIN_ROWS_MAX - 1), kh, axis=1)
        bias = jnp.take(bias_rows, col_off, axis=2).transpose(0, 2, 1, 3)
        s_loc = jnp.einsum('bhwd,bhiwjd->bhwij', q_r, k_win).astype(jnp.float32) + bias[None].astype(jnp.float32)
        s_ctx = jnp.einsum('bhwd,bhpd->bhwp', q_r, k_ctx).astype(jnp.float32)
        logits = jnp.concatenate([s_loc.reshape(b, h, GRID_W, n_loc), s_ctx], axis=-1)
        p = jax.nn.softmax(logits, axis=-1).astype(v.dtype)
        p_loc = p[..., :n_loc].reshape(b, h, GRID_W, kh, WIN_COLS)
        p_ctx = p[..., n_loc:]
        return (jnp.einsum('bhwij,bhiwjd->bhwd', p_loc, v_win)
                + jnp.einsum('bhwp,bhpd->bhwd', p_ctx, v_ctx))

    o = lax.map(row_block, jnp.arange(rows))
    return o.transpose(1, 0, 3, 2, 4).reshape(b, s, h * dh)


def setup_inputs(seed: int = 0) -> dict:
    key = jax.random.key(seed)
    ks = jax.random.split(key, 18)
    f32 = jnp.float32

    def nrm(k, shape, scale):
        return jax.random.normal(k, shape, f32) * scale

    return {
        "x_prompt": nrm(ks[0], (BATCH, SEQ, D_MODEL), 1.0),
        "x_sample": nrm(ks[1], (DEC_BATCH, DEC_SEQ, D_MODEL), 1.0),
        "cache_k": nrm(ks[2], (DEC_BATCH, N_NA_LAYERS, N_HEADS, PAST_LEN, HEAD_DIM), 1.0),
        "cache_v": nrm(ks[3], (DEC_BATCH, N_NA_LAYERS, N_HEADS, PAST_LEN, HEAD_DIM), 1.0),
        "c": nrm(ks[4], (DEC_BATCH, D_MODEL), 1.0),
        "c_ctx": nrm(ks[5], (D_MODEL,), 1.0),
        "w_mod": nrm(ks[6], (DEPTH, D_MODEL, N_MOD * D_MODEL), 0.5 * D_MODEL ** -0.5),
        "b_mod": nrm(ks[7], (DEPTH, N_MOD * D_MODEL), 0.01),
        "norm_pre": 1.0 + nrm(ks[8], (DEPTH, 3, D_MODEL), 0.02),
        "norm_post": 1.0 + nrm(ks[9], (DEPTH, 3, D_MODEL), 0.02),
        "ffn_w1": nrm(ks[10], (DEPTH, 2, D_MODEL, 2 * D_FF), D_MODEL ** -0.5),
        "ffn_w2": nrm(ks[11], (DEPTH, 2, D_FF, D_MODEL), D_FF ** -0.5),
        "four_w_in": nrm(ks[12], (N_FOURIER_LAYERS, D_MODEL, D_MODEL), D_MODEL ** -0.5),
        "four_w_out": nrm(ks[13], (N_FOURIER_LAYERS, D_MODEL, D_MODEL), D_MODEL ** -0.5),
        "na_w_qkv": nrm(ks[14], (N_NA_LAYERS, D_MODEL, 3 * D_MODEL), D_MODEL ** -0.5),
        "na_w_out": nrm(ks[15], (N_NA_LAYERS, D_MODEL, D_MODEL), D_MODEL ** -0.5),
        "na_rpb": nrm(ks[16], (N_NA_LAYERS, N_HEADS, 2 * WIN_ROWS_MAX - 1, 2 * WIN_COLS - 1), 0.02),
    }


def reference(x_prompt, x_sample, cache_k, cache_v, c, c_ctx, w_mod, b_mod, norm_pre, norm_post,
              ffn_w1, ffn_w2, four_w_in, four_w_out, na_w_qkv, na_w_out, na_rpb):
    rows = x_sample.shape[1] // GRID_W
    xp = x_prompt
    xs = x_sample
    new_k = []
    new_v = []
    for i in range(DEPTH):
        m_ctx = _modulation(c_ctx[None, :], w_mod[i], b_mod[i])
        m_lat = _modulation(c, w_mod[i], b_mod[i])
        xp = _half_ffn(xp, m_ctx, 0, norm_pre[i, 0], norm_post[i, 0], ffn_w1[i, 0], ffn_w2[i, 0])
        xs = _half_ffn(xs, m_lat, 0, norm_pre[i, 0], norm_post[i, 0], ffn_w1[i, 0], ffn_w2[i, 0])
        hp = _modulated_prenorm(xp, norm_pre[i, 1], m_ctx, 1)
        hs = _modulated_prenorm(xs, norm_pre[i, 1], m_lat, 1)
        j = i // N_MIXERS
        if i % N_MIXERS == 0:
            yp = _fourier_mix(hp, four_w_in[j], four_w_out[j])
            ys = _fourier_mix(hs, four_w_in[j], four_w_out[j])
        else:
            qp, kp, vp = _qkv(hp, na_w_qkv[j])
            new_k.append(kp)
            new_v.append(vp)
            yp = _merge_heads(_context_attention(qp, kp, vp)) @ na_w_out[j]
            qs, ks_, vs = _qkv(hs, na_w_qkv[j])
            ys = _neighbourhood_attention(qs, ks_, vs, cache_k[:, j], cache_v[:, j], na_rpb[j], rows) @ na_w_out[j]
        xp = xp + m_ctx[:, 5, None] * _rmsnorm(yp, norm_post[i, 1])
        xs = xs + m_lat[:, 5, None] * _rmsnorm(ys, norm_post[i, 1])
        xp = _half_ffn(xp, m_ctx, 2, norm_pre[i, 2], norm_post[i, 2], ffn_w1[i, 1], ffn_w2[i, 1])
        xs = _half_ffn(xs, m_lat, 2, norm_pre[i, 2], norm_post[i, 2], ffn_w1[i, 1], ffn_w2[i, 1])
    new_cache_k = jnp.stack(new_k, axis=1)
    new_cache_v = jnp.stack(new_v, axis=1)
    return (xp, xs, new_cache_k, new_cache_v)
```

```python
import functools

import numpy as np
import jax
import jax.numpy as jnp
from jax import lax
from jax.experimental import pallas as pl
from jax.experimental.pallas import tpu as pltpu

F32 = jnp.float32
BF16 = jnp.bfloat16

D = 1024
D_FF = 2816
DEPTH = 4
N_MOD = 9
N_HEADS = 16
HEAD_DIM = 64
N_PAIRS = N_HEADS // 2
SEQ = 256
BATCH = 16
GRID = 64
LAT = GRID * GRID
N_LAT = 2
GROUP_ROWS = 4096
N_GROUPS = 3
T_ALL = N_GROUPS * GROUP_ROWS
PAST = 256
WIN_ROWS = 8
WIN_COLS = 16
N_FG = 8
FG_DIM = D // N_FG
EPS = 1e-6
NEG = -1e30

VMEM_LIMIT = 56 * 1024 * 1024

FFN_TM = 512
FFN_NC = 2
MOD_TN = 2304
QKV_TM = 512
OUT_TM = 512
FS_TN = 16
NA_ROWS = 4
NA_Q = NA_ROWS * GRID
NA_KROWS = 12
NA_K = NA_KROWS * GRID
NA_BLOCKS = GRID // NA_ROWS
NA_VARIANTS = 5
NA_DT = 22


def _cparams(n_axes):
    return pltpu.CompilerParams(
        dimension_semantics=("arbitrary",) * n_axes, vmem_limit_bytes=VMEM_LIMIT)


def _const_spec(shape):
    nd = len(shape)
    return pl.BlockSpec(shape, lambda *_: (0,) * nd, pipeline_mode=pl.Buffered(1))


def _silu(x):
    return x * jax.nn.sigmoid(x)


def _prenorm(x, g, mod_ref, k):
    shift = mod_ref[3 * k:3 * k + 1, :]
    scale = mod_ref[3 * k + 1:3 * k + 2, :]
    y = x * lax.rsqrt(jnp.mean(x * x, axis=-1, keepdims=True) + EPS)
    return ((y * g) * (1.0 + scale) + shift).astype(BF16)


def _postnorm(y, g):
    return (y * lax.rsqrt(jnp.mean(y * y, axis=-1, keepdims=True) + EPS)) * g


def _dot(a, b):
    return jnp.dot(a, b, preferred_element_type=F32)


def _dot_nt(a, b):
    return lax.dot_general(a, b, (((1,), (1,)), ((), ())), preferred_element_type=F32)


def _mod_body(cond_ref, w_ref, b_ref, o_ref):
    s = _silu(cond_ref[...]).astype(BF16)
    o_ref[...] = _dot(s, w_ref[...].astype(BF16)) + b_ref[...]


def _modulation(conds, w_mod, b_mod):
    n = N_MOD * D
    return pl.pallas_call(
        _mod_body,
        grid=(DEPTH, n // MOD_TN),
        in_specs=[
            pl.BlockSpec((8, D), lambda l, j: (0, 0)),
            pl.BlockSpec((None, D, MOD_TN), lambda l, j: (l, 0, j)),
            pl.BlockSpec((None, 1, MOD_TN), lambda l, j: (l, 0, j)),
        ],
        out_specs=pl.BlockSpec((None, 8, MOD_TN), lambda l, j: (l, 0, j)),
        out_shape=jax.ShapeDtypeStruct((DEPTH, 8, n), F32),
        compiler_params=_cparams(2),
        name="modulation",
    )(conds, w_mod, b_mod.reshape(DEPTH, 1, n))


def _ffn_body(x_ref, mod_ref, gpre_ref, gpost_ref, w1_ref, w2_ref, o_ref, *, k):
    x = x_ref[...]
    h = _prenorm(x, gpre_ref[...], mod_ref, k)
    ck = D_FF // FFN_NC
    y = None
    for c in range(FFN_NC):
        g = _dot(h, w1_ref[:, c * ck:(c + 1) * ck])
        u = _dot(h, w1_ref[:, D_FF + c * ck:D_FF + (c + 1) * ck])
        a = (_silu(g) * u).astype(BF16)
        yc = _dot(a, w2_ref[c * ck:(c + 1) * ck, :])
        y = yc if y is None else y + yc
    gate = mod_ref[3 * k + 2:3 * k + 3, :]
    o_ref[...] = x + (0.5 * gate) * _postnorm(y, gpost_ref[...])


def _half_ffn(x, mods, gpre, gpost, w1, w2, k):
    tiles_per_group = GROUP_ROWS // FFN_TM
    return pl.pallas_call(
        functools.partial(_ffn_body, k=k),
        grid=(T_ALL // FFN_TM,),
        in_specs=[
            pl.BlockSpec((FFN_TM, D), lambda i: (i, 0)),
            pl.BlockSpec((None, N_MOD, D), lambda i: (i // tiles_per_group, 0, 0)),
            _const_spec((1, D)),
            _const_spec((1, D)),
            _const_spec((D, 2 * D_FF)),
            _const_spec((D_FF, D)),
        ],
        out_specs=pl.BlockSpec((FFN_TM, D), lambda i: (i, 0)),
        out_shape=jax.ShapeDtypeStruct((T_ALL, D), F32),
        compiler_params=_cparams(1),
        name="half_ffn",
    )(x, mods, gpre, gpost, w1, w2)


def _dft_cos_sin(n):
    idx = np.arange(n)
    ang = 2.0 * np.pi * ((idx[:, None] * idx[None, :]) % n) / n
    return np.cos(ang), np.sin(ang)


def _fourier_constants():
    c128, s128 = _dft_cos_sin(FG_DIM)
    chan = np.stack([c128, s128]) / np.sqrt(FG_DIM)

    c256, s256 = _dft_cos_sin(SEQ)
    dft_prompt = np.concatenate([c256, -s256], axis=0) / np.sqrt(SEQ)

    c64, s64 = _dft_cos_sin(GRID)
    c64 = jnp.asarray(c64 / np.sqrt(GRID), F32)
    s64 = jnp.asarray(s64 / np.sqrt(GRID), F32)
    eye = jnp.eye(FS_TN, dtype=F32)
    rows = GRID * FS_TN
    def kron_eye(f):
        return (f[:, None, :, None] * eye[None, :, None, :]).reshape(rows, rows)
    m1 = jnp.concatenate([kron_eye(c64), kron_eye(-s64)], axis=0)
    def spread(f):
        return (f[:, None, None, :] * eye[None, :, :, None]).reshape(rows, rows)
    m2 = jnp.block([[spread(c64), spread(s64)], [spread(-s64), spread(c64)]])
    k1 = np.arange(GRID)[:, None]
    n2 = np.arange(GRID)[None, :]
    ang = 2.0 * np.pi * (k1 * n2) / LAT
    tw = np.stack([np.cos(ang), np.sin(ang)])[..., None] * np.ones((1, 1, 1, 128))
    return (jnp.asarray(chan, F32), jnp.asarray(dft_prompt, F32).astype(BF16),
            m1.astype(BF16), m2.astype(BF16), jnp.asarray(tw, F32))


def _fold_body(cs_ref, w_ref, o_ref):
    w = w_ref[...]
    for t in range(2):
        o_ref[t] = jnp.dot(cs_ref[t], w, preferred_element_type=F32,
                           precision=lax.Precision.HIGHEST).astype(BF16)


def _fold_channel_dft(chan, w_out):
    return pl.pallas_call(
        _fold_body,
        grid=(N_FG,),
        in_specs=[
            pl.BlockSpec((2, FG_DIM, FG_DIM), lambda g: (0, 0, 0)),
            pl.BlockSpec((FG_DIM, D), lambda g: (g, 0)),
        ],
        out_specs=pl.BlockSpec((2, FG_DIM, D), lambda g: (0, g, 0)),
        out_shape=jax.ShapeDtypeStruct((2, D, D), BF16),
        compiler_params=_cparams(1),
        name="fold_channel_dft",
    )(chan, w_out)


def _four_prompt_body(x_ref, mod_ref, gpre_ref, gpost_ref, win_ref, dft_ref, w2_ref, o_ref):
    x = x_ref[...]
    h = _prenorm(x, gpre_ref[...], mod_ref, 1)
    u = _dot(h, win_ref[...]).astype(BF16)
    p = _dot(dft_ref[...], u)
    y = _dot(p[:SEQ].astype(BF16), w2_ref[0]) + _dot(p[SEQ:].astype(BF16), w2_ref[1])
    o_ref[...] = x + mod_ref[5:6, :] * _postnorm(y, gpost_ref[...])


def _fourier_prompt(x, mods, gpre, gpost, w_in, dft_prompt, w2):
    return pl.pallas_call(
        _four_prompt_body,
        grid=(BATCH,),
        in_specs=[
            pl.BlockSpec((SEQ, D), lambda b: (b, 0)),
            pl.BlockSpec((None, N_MOD, D), lambda b: (0, 0, 0)),
            _const_spec((1, D)),
            _const_spec((1, D)),
            _const_spec((D, D)),
            _const_spec((2 * SEQ, SEQ)),
            _const_spec((2, D, D)),
        ],
        out_specs=pl.BlockSpec((SEQ, D), lambda b: (b, 0)),
        out_shape=jax.ShapeDtypeStruct((T_ALL, D), F32),
        input_output_aliases={0: 0},
        compiler_params=_cparams(1),
        name="fourier_prompt",
    )(x, mods, gpre, gpost, w_in, dft_prompt, w2)


def _four_s1_body(x_ref, mod_ref, gpre_ref, win_ref, m1_ref, tw_ref, y_ref):
    rows = GRID * FS_TN
    x = x_ref[...].reshape(rows, D)
    h = _prenorm(x, gpre_ref[...], mod_ref, 1)
    u = _dot(h, win_ref[...]).astype(BF16)
    y = _dot(m1_ref[...], u)
    tc = tw_ref[0].reshape(rows, 128)
    ts = tw_ref[1].reshape(rows, 128)
    for l in range(D // 128):
        sl = slice(l * 128, (l + 1) * 128)
        yr = y[:rows, sl]
        yi = y[rows:, sl]
        y_ref[0, :, :, sl] = (yr * tc + yi * ts).reshape(GRID, FS_TN, 128).astype(BF16)
        y_ref[1, :, :, sl] = (yi * tc - yr * ts).reshape(GRID, FS_TN, 128).astype(BF16)


def _fourier_latent_stage1(x4, mods, gpre, w_in, m1, tw):
    rows = GRID * FS_TN
    return pl.pallas_call(
        _four_s1_body,
        grid=(N_LAT, GRID // FS_TN),
        in_specs=[
            pl.BlockSpec((None, GRID, FS_TN, D), lambda b, j: (b + 1, 0, j, 0)),
            pl.BlockSpec((None, N_MOD, D), lambda b, j: (b + 1, 0, 0)),
            _const_spec((1, D)),
            _const_spec((D, D)),
            _const_spec((2 * rows, rows)),
            pl.BlockSpec((2, GRID, FS_TN, 128), lambda b, j: (0, 0, j, 0)),
        ],
        out_specs=pl.BlockSpec((None, 2, GRID, FS_TN, D), lambda b, j: (b, 0, 0, j, 0)),
        out_shape=jax.ShapeDtypeStruct((N_LAT, 2, GRID, GRID, D), BF16),
        compiler_params=_cparams(2),
        name="fourier_latent_stage1",
    )(x4, mods, gpre, w_in, m1, tw)


def _four_s2_body(y_ref, x_ref, mod_ref, gpost_ref, m2_ref, w2_ref, o_ref):
    rows = GRID * FS_TN
    yin = y_ref[...].reshape(2 * rows, D)
    z = _dot(m2_ref[...], yin)
    y = _dot(z[:rows].astype(BF16), w2_ref[0]) + _dot(z[rows:].astype(BF16), w2_ref[1])
    x = x_ref[...].reshape(rows, D)
    out = x + mod_ref[5:6, :] * _postnorm(y, gpost_ref[...])
    o_ref[...] = out.reshape(GRID, FS_TN, D)


def _fourier_latent_stage2(y5, x4, mods, gpost, m2, w2):
    rows = GRID * FS_TN
    return pl.pallas_call(
        _four_s2_body,
        grid=(N_LAT, GRID // FS_TN),
        in_specs=[
            pl.BlockSpec((None, 2, FS_TN, GRID, D), lambda b, j: (b, 0, j, 0, 0)),
            pl.BlockSpec((None, GRID, FS_TN, D), lambda b, j: (b + 1, 0, j, 0)),
            pl.BlockSpec((None, N_MOD, D), lambda b, j: (b + 1, 0, 0)),
            _const_spec((1, D)),
            _const_spec((2 * rows, 2 * rows)),
            _const_spec((2, D, D)),
        ],
        out_specs=pl.BlockSpec((None, GRID, FS_TN, D), lambda b, j: (b + 1, 0, j, 0)),
        out_shape=jax.ShapeDtypeStruct((N_GROUPS, GRID, GRID, D), F32),
        input_output_aliases={1: 0},
        compiler_params=_cparams(2),
        name="fourier_latent_stage2",
    )(y5, x4, mods, gpost, m2, w2)


def _ctx_attn_body(x_ref, mod_ref, gpre_ref, gpost_ref, wqkv_ref, wout_ref,
                   o_ref, kc_ref, vc_ref):
    x = x_ref[...]
    h = _prenorm(x, gpre_ref[...], mod_ref, 1)
    qkv = _dot(h, wqkv_ref[...])
    k = qkv[:, D:2 * D]
    v = qkv[:, 2 * D:]
    for hd in range(N_HEADS):
        sl = slice(hd * HEAD_DIM, (hd + 1) * HEAD_DIM)
        kc_ref[hd] = k[:, sl]
        vc_ref[hd] = v[:, sl]
    qb = (qkv[:, :D] * (HEAD_DIM ** -0.5)).astype(BF16)
    kb = k.astype(BF16)
    vb = v.astype(BF16)
    first = lax.broadcasted_iota(jnp.int32, (1, 2 * HEAD_DIM), 1) < HEAD_DIM
    zero = jnp.zeros((), BF16)
    outs = []
    for p in range(N_PAIRS):
        sl = slice(p * 2 * HEAD_DIM, (p + 1) * 2 * HEAD_DIM)
        kp = kb[:, sl]
        vp = vb[:, sl]
        k2 = jnp.concatenate([jnp.where(first, kp, zero), jnp.where(first, zero, kp)], axis=0)
        v2 = jnp.concatenate([jnp.where(first, vp, zero), jnp.where(first, zero, vp)], axis=0)
        s = _dot_nt(qb[:, sl], k2)
        probs = []
        for hh in range(2):
            sh = s[:, hh * SEQ:(hh + 1) * SEQ]
            e = jnp.exp(sh - jnp.max(sh, axis=-1, keepdims=True))
            probs.append(e / jnp.sum(e, axis=-1, keepdims=True))
        outs.append(_dot(jnp.concatenate(probs, axis=1).astype(BF16), v2))
    o = jnp.concatenate(outs, axis=1).astype(BF16)
    y = _dot(o, wout_ref[...])
    o_ref[...] = x + mod_ref[5:6, :] * _postnorm(y, gpost_ref[...])


def _context_attention(x, mods, gpre, gpost, w_qkv, w_out):
    cache = jax.ShapeDtypeStruct((BATCH, N_HEADS, SEQ, HEAD_DIM), F32)
    cache_spec = pl.BlockSpec((None, N_HEADS, SEQ, HEAD_DIM), lambda b: (b, 0, 0, 0))
    return pl.pallas_call(
        _ctx_attn_body,
        grid=(BATCH,),
        in_specs=[
            pl.BlockSpec((SEQ, D), lambda b: (b, 0)),
            pl.BlockSpec((None, N_MOD, D), lambda b: (0, 0, 0)),
            _const_spec((1, D)),
            _const_spec((1, D)),
            _const_spec((D, 3 * D)),
            _const_spec((D, D)),
        ],
        out_specs=[pl.BlockSpec((SEQ, D), lambda b: (b, 0)), cache_spec, cache_spec],
        out_shape=[jax.ShapeDtypeStruct((T_ALL, D), F32), cache, cache],
        input_output_aliases={0: 0},
        compiler_params=_cparams(1),
        name="context_attention",
    )(x, mods, gpre, gpost, w_qkv, w_out)


def _qkv_lat_body(x_ref, mod_ref, gpre_ref, wq_ref, wkt_ref, wv_ref, q_ref, kt_ref, v_ref):
    h = _prenorm(x_ref[...], gpre_ref[...], mod_ref, 1)
    q_ref[...] = (_dot(h, wq_ref[...]) * (HEAD_DIM ** -0.5)).astype(BF16)
    v_ref[...] = _dot(h, wv_ref[...]).astype(BF16)
    kt_ref[...] = _dot_nt(wkt_ref[...], h).astype(BF16)


def _qkv_latent(x, mods, gpre, wq, wkt, wv):
    per_lat = LAT // QKV_TM
    first = GROUP_ROWS // QKV_TM
    tok = jax.ShapeDtypeStruct((N_LAT * LAT, D), BF16)
    return pl.pallas_call(
        _qkv_lat_body,
        grid=(N_LAT * per_lat,),
        in_specs=[
            pl.BlockSpec((QKV_TM, D), lambda i: (first + i, 0)),
            pl.BlockSpec((None, N_MOD, D), lambda i: (1 + i // per_lat, 0, 0)),
            _const_spec((1, D)),
            _const_spec((D, D)),
            _const_spec((D, D)),
            _const_spec((D, D)),
        ],
        out_specs=[
            pl.BlockSpec((QKV_TM, D), lambda i: (i, 0)),
            pl.BlockSpec((None, D, QKV_TM), lambda i: (i // per_lat, 0, i % per_lat)),
            pl.BlockSpec((QKV_TM, D), lambda i: (i, 0)),
        ],
        out_shape=[tok, jax.ShapeDtypeStruct((N_LAT, D, LAT), BF16), tok],
        compiler_params=_cparams(1),
        name="qkv_latent",
    )(x, mods, gpre, wq, wkt, wv)


def _na_block_geometry(blk):
    return min(max(NA_ROWS * blk - WIN_ROWS // 2, 0), GRID - NA_KROWS)


_NA_VARIANT_BLOCKS = (0, 1, 2, NA_BLOCKS - 2, NA_BLOCKS - 1)


def _na_tables():
    t = np.arange(NA_DT)[:, None]
    e = np.arange(2)[None, :]
    drow = t - 11 + e
    row_ok = (drow >= -(WIN_ROWS - 1)) & (drow <= WIN_ROWS - 1)
    ridx = np.clip(drow + WIN_ROWS - 1, 0, 2 * WIN_ROWS - 2)
    c = np.arange(GRID)[:, None]
    cp = np.arange(GRID)[None, :]
    cs = np.clip(c - WIN_COLS // 2, 0, GRID - WIN_COLS)
    col_ok = (cp >= cs) & (cp < cs + WIN_COLS)
    cidx = np.clip(cp - c + WIN_COLS - 1, 0, 2 * WIN_COLS - 2)
    ok = row_ok[:, None, :, None] & col_ok[None, :, None, :]
    rowmask = np.zeros((24, NA_K), np.float32)
    delta0 = np.zeros((NA_VARIANTS, NA_ROWS, NA_KROWS // 2), np.int64)
    for v, blk in enumerate(_NA_VARIANT_BLOCKS):
        u0 = _na_block_geometry(blk)
        for rl in range(NA_ROWS):
            r = NA_ROWS * blk + rl
            rs = min(max(r - WIN_ROWS // 2, 0), GRID - WIN_ROWS)
            kr = u0 + np.arange(NA_KROWS)
            valid = (kr >= rs) & (kr < rs + WIN_ROWS)
            rowmask[v * NA_ROWS + rl] = np.repeat(np.where(valid, 0.0, NEG), GRID)
            delta0[v, rl] = u0 - r + 2 * np.arange(NA_KROWS // 2) + 11
    assert delta0.min() >= 0 and delta0.max() < NA_DT
    return ridx, cidx, ok, rowmask, delta0


def _na_bias_pairs(rpb, ridx, cidx, ok):
    tab = rpb[:, ridx[:, None, :, None], cidx[None, :, None, :]]
    tab = jnp.where(ok[None], tab, NEG)
    return tab.reshape(N_HEADS, NA_DT, GRID, 2 * GRID)


def _natten_body(q_ref, kt_ref, v_ref, kct_ref, vc_ref, bp_ref, rm_ref, o_ref, bias_scr,
                 *, delta0):
    for hh in range(2):
        for var in range(NA_VARIANTS):
            for rl in range(NA_ROWS):
                strip = jnp.concatenate(
                    [bp_ref[hh, int(delta0[var, rl, i])] for i in range(NA_KROWS // 2)], axis=1)
                row = var * NA_ROWS + rl
                bias_scr[hh, var, rl * GRID:(rl + 1) * GRID, :] = strip + rm_ref[row:row + 1, :]

    first = lax.broadcasted_iota(jnp.int32, (1, 2 * HEAD_DIM), 1) < HEAD_DIM
    zero = jnp.zeros((), BF16)
    kct = kct_ref[...]
    vc = vc_ref[...]

    def block(blk, carry):
        koff = pl.multiple_of(jnp.clip(blk - 1, 0, NA_BLOCKS - 3) * NA_Q, NA_Q)
        qoff = pl.multiple_of(blk * NA_Q, NA_Q)
        var = jnp.where(blk < 2, blk, jnp.where(blk < NA_BLOCKS - 2, 2, blk - (NA_BLOCKS - 5)))
        qb = q_ref[pl.ds(qoff, NA_Q), :]
        ktb = kt_ref[:, pl.ds(koff, NA_K)]
        vb = v_ref[pl.ds(koff, NA_K), :]
        o = jnp.zeros((NA_Q, 2 * HEAD_DIM), F32)
        for hh in range(2):
            keep = first if hh == 0 else jnp.logical_not(first)
            qh = jnp.where(keep, qb, zero)
            s_loc = _dot(qh, ktb) + bias_scr[hh, var]
            s_ctx = _dot(qh, kct)
            mx = jnp.maximum(jnp.max(s_loc, axis=-1, keepdims=True),
                             jnp.max(s_ctx, axis=-1, keepdims=True))
            e_loc = jnp.exp(s_loc - mx)
            e_ctx = jnp.exp(s_ctx - mx)
            den = jnp.sum(e_loc, axis=-1, keepdims=True) + jnp.sum(e_ctx, axis=-1, keepdims=True)
            oh = (_dot(e_loc.astype(BF16), jnp.where(keep, vb, zero))
                  + _dot(e_ctx.astype(BF16), jnp.where(keep, vc, zero)))
            o = o + oh / den
        o_ref[pl.ds(qoff, NA_Q), :] = o.astype(BF16)
        return carry

    lax.fori_loop(0, NA_BLOCKS, block, 0)


def _neighbourhood_attention(q, kt, v, kct, vc, bias_pairs, rowmask, delta0):
    lane = 2 * HEAD_DIM
    return pl.pallas_call(
        functools.partial(_natten_body, delta0=delta0),
        grid=(N_LAT, N_PAIRS),
        in_specs=[
            pl.BlockSpec((LAT, lane), lambda b, p: (b, p)),
            pl.BlockSpec((None, lane, LAT), lambda b, p: (b, p, 0)),
            pl.BlockSpec((LAT, lane), lambda b, p: (b, p)),
            pl.BlockSpec((None, lane, PAST), lambda b, p: (b, p, 0)),
            pl.BlockSpec((None, PAST, lane), lambda b, p: (b, 0, p)),
            pl.BlockSpec((2, NA_DT, GRID, lane), lambda b, p: (p, 0, 0, 0)),
            pl.BlockSpec((24, NA_K), lambda b, p: (0, 0)),
        ],
        out_specs=pl.BlockSpec((LAT, lane), lambda b, p: (b, p)),
        out_shape=jax.ShapeDtypeStruct((N_LAT * LAT, D), BF16),
        scratch_shapes=[pltpu.VMEM((2, NA_VARIANTS, NA_Q, NA_K), F32)],
        compiler_params=_cparams(2),
        name="neighbourhood_attention",
    )(q, kt, v, kct, vc, bias_pairs, rowmask)


def _attn_out_body(o_ref, x_ref, mod_ref, gpost_ref, wout_ref, out_ref):
    y = _dot(o_ref[...], wout_ref[...])
    out_ref[...] = x_ref[...] + mod_ref[5:6, :] * _postnorm(y, gpost_ref[...])


def _attention_out_latent(o, x, mods, gpost, w_out):
    per_lat = LAT // OUT_TM
    first = GROUP_ROWS // OUT_TM
    return pl.pallas_call(
        _attn_out_body,
        grid=(N_LAT * per_lat,),
        in_specs=[
            pl.BlockSpec((OUT_TM, D), lambda i: (i, 0)),
            pl.BlockSpec((OUT_TM, D), lambda i: (first + i, 0)),
            pl.BlockSpec((None, N_MOD, D), lambda i: (1 + i // per_lat, 0, 0)),
            _const_spec((1, D)),
            _const_spec((D, D)),
        ],
        out_specs=pl.BlockSpec((OUT_TM, D), lambda i: (first + i, 0)),
        out_shape=jax.ShapeDtypeStruct((T_ALL, D), F32),
        input_output_aliases={1: 0},
        compiler_params=_cparams(1),
        name="attention_out_latent",
    )(o, x, mods, gpost, w_out)


def kernel(x_prompt, x_sample, cache_k, cache_v, c, c_ctx, w_mod, b_mod, norm_pre, norm_post,
           ffn_w1, ffn_w2, four_w_in, four_w_out, na_w_qkv, na_w_out, na_rpb):
    x = jnp.concatenate([x_prompt.reshape(BATCH * SEQ, D), x_sample.reshape(N_LAT * LAT, D)])

    conds = jnp.concatenate([c_ctx[None, :], c, jnp.zeros((8 - 1 - N_LAT, D), F32)], axis=0)
    mods = _modulation(conds, w_mod, b_mod)[:, :N_GROUPS].reshape(DEPTH, N_GROUPS, N_MOD, D)

    chan, dft_prompt, m1, m2, tw = _fourier_constants()
    ridx, cidx, ok, rowmask, delta0 = _na_tables()
    rowmask = jnp.asarray(rowmask)

    w1 = ffn_w1.astype(BF16)
    w2 = ffn_w2.astype(BF16)
    pre = norm_pre[:, :, None, :]
    post = norm_post[:, :, None, :]

    new_k, new_v = [], []
    for i in range(DEPTH):
        j = i // 2
        x = _half_ffn(x, mods[i], pre[i, 0], post[i, 0], w1[i, 0], w2[i, 0], 0)
        if i % 2 == 0:
            w_in = four_w_in[j].astype(BF16)
            wf = _fold_channel_dft(chan, four_w_out[j])
            x = _fourier_prompt(x, mods[i], pre[i, 1], post[i, 1], w_in, dft_prompt, wf)
            x4 = x.reshape(N_GROUPS, GRID, GRID, D)
            y5 = _fourier_latent_stage1(x4, mods[i], pre[i, 1], w_in, m1, tw)
            x4 = _fourier_latent_stage2(y5, x4, mods[i], post[i, 1], m2, wf)
            x = x4.reshape(T_ALL, D)
        else:
            w_qkv = na_w_qkv[j].astype(BF16)
            w_out = na_w_out[j].astype(BF16)
            x, kc, vc = _context_attention(x, mods[i], pre[i, 1], post[i, 1], w_qkv, w_out)
            new_k.append(kc)
            new_v.append(vc)
            q, kt, v = _qkv_latent(x, mods[i], pre[i, 1], w_qkv[:, :D],
                                   w_qkv[:, D:2 * D].T, w_qkv[:, 2 * D:])
            kct = cache_k[:, j].transpose(0, 1, 3, 2).reshape(N_LAT, D, PAST).astype(BF16)
            vct = cache_v[:, j].transpose(0, 2, 1, 3).reshape(N_LAT, PAST, D).astype(BF16)
            bias_pairs = _na_bias_pairs(na_rpb[j], ridx, cidx, ok)
            o = _neighbourhood_attention(q, kt, v, kct, vct, bias_pairs, rowmask, delta0)
            x = _attention_out_latent(o, x, mods[i], post[i, 1], w_out)
        x = _half_ffn(x, mods[i], pre[i, 2], post[i, 2], w1[i, 1], w2[i, 1], 2)

    y_prompt = x[:BATCH * SEQ].reshape(BATCH, SEQ, D)
    y_sample = x[BATCH * SEQ:].reshape(N_LAT, LAT, D)
    return (y_prompt, y_sample, jnp.stack(new_k, axis=1), jnp.stack(new_v, axis=1))
```

```python
import functools

import numpy as np
import jax
import jax.numpy as jnp
from jax import lax
from jax.experimental import pallas as pl
from jax.experimental.pallas import tpu as pltpu

F32 = jnp.float32
BF16 = jnp.bfloat16

D = 1024
D_FF = 2816
DEPTH = 4
N_MOD = 9
N_HEADS = 16
HEAD_DIM = 64
N_PAIRS = N_HEADS // 2
SEQ = 256
BATCH = 16
GRID = 64
LAT = GRID * GRID
N_LAT = 2
GROUP_ROWS = 4096
N_GROUPS = 3
T_ALL = N_GROUPS * GROUP_ROWS
PAST = 256
WIN_ROWS = 8
WIN_COLS = 16
N_FG = 8
FG_DIM = D // N_FG
EPS = 1e-6
NEG = -1e30
LOG2E = 1.4426950408889634
Q_SCALE = HEAD_DIM ** -0.5 * LOG2E

VMEM_LIMIT = 56 * 1024 * 1024

FFN_TM = 1024
FFN_SUB = 256
FFN_NC = 1
MOD_TN = 2304
QKV_TM = 512
OUT_TM = 512
FS_TN = 16
NA_ROWS = 4
NA_Q = NA_ROWS * GRID
NA_KROWS = 12
NA_K = NA_KROWS * GRID
NA_BLOCKS = GRID // NA_ROWS
NA_VARIANTS = 5
NA_DT = 22


def _cparams(n_axes):
    return pltpu.CompilerParams(
        dimension_semantics=("arbitrary",) * n_axes, vmem_limit_bytes=VMEM_LIMIT)


def _const_spec(shape, *lead):
    nd = len(shape)
    return pl.BlockSpec((None,) * len(lead) + tuple(shape), lambda *_: tuple(lead) + (0,) * nd,
                        pipeline_mode=pl.Buffered(1))


def _mod_spec(layer, group_of):
    return pl.BlockSpec((None, None, N_MOD, D), lambda *ids: (layer, group_of(*ids), 0, 0))


def _silu(x):
    return x * jax.nn.sigmoid(x)


def _prenorm(x, g, mod_ref, k):
    shift = mod_ref[3 * k:3 * k + 1, :]
    scale = mod_ref[3 * k + 1:3 * k + 2, :]
    y = x * lax.rsqrt(jnp.mean(x * x, axis=-1, keepdims=True) + EPS)
    return ((y * g) * (1.0 + scale) + shift).astype(BF16)


def _postnorm(y, g):
    return (y * lax.rsqrt(jnp.mean(y * y, axis=-1, keepdims=True) + EPS)) * g


def _dot(a, b):
    return jnp.dot(a, b, preferred_element_type=F32)


def _dot_nt(a, b):
    return lax.dot_general(a, b, (((1,), (1,)), ((), ())), preferred_element_type=F32)


def _mod_body(cond_ref, w_ref, b_ref, o_ref):
    s = _silu(cond_ref[...]).astype(BF16)
    o_ref[...] = _dot(s, w_ref[...].astype(BF16)) + b_ref[...]


def _modulation(conds, w_mod, b_mod):
    n = N_MOD * D
    return pl.pallas_call(
        _mod_body,
        grid=(DEPTH, n // MOD_TN),
        in_specs=[
            pl.BlockSpec((8, D), lambda l, j: (0, 0)),
            pl.BlockSpec((None, D, MOD_TN), lambda l, j: (l, 0, j)),
            pl.BlockSpec((None, 1, MOD_TN), lambda l, j: (l, 0, j)),
        ],
        out_specs=pl.BlockSpec((None, 8, MOD_TN), lambda l, j: (l, 0, j)),
        out_shape=jax.ShapeDtypeStruct((DEPTH, 8, n), F32),
        compiler_params=_cparams(2),
        name="modulation",
    )(conds, w_mod, b_mod.reshape(DEPTH, 1, n))


def _ffn_body(x_ref, mod_ref, gpre_ref, gpost_ref, w1_ref, w2_ref, o_ref, *, k):
    ck = D_FF // FFN_NC
    gate = mod_ref[3 * k + 2:3 * k + 3, :]
    for r in range(FFN_TM // FFN_SUB):
        rows = slice(r * FFN_SUB, (r + 1) * FFN_SUB)
        x = x_ref[rows, :]
        h = _prenorm(x, gpre_ref[...], mod_ref, k)
        y = None
        for c in range(FFN_NC):
            g = _dot(h, w1_ref[:, c * ck:(c + 1) * ck])
            u = _dot(h, w1_ref[:, D_FF + c * ck:D_FF + (c + 1) * ck])
            a = (_silu(g) * u).astype(BF16)
            yc = _dot(a, w2_ref[c * ck:(c + 1) * ck, :])
            y = yc if y is None else y + yc
        o_ref[rows, :] = x + (0.5 * gate) * _postnorm(y, gpost_ref[...])


def _half_ffn(x, mods, pre, post, w1, w2, layer, slot):
    k = 2 * slot
    tiles_per_group = GROUP_ROWS // FFN_TM
    return pl.pallas_call(
        functools.partial(_ffn_body, k=k),
        grid=(T_ALL // FFN_TM,),
        in_specs=[
            pl.BlockSpec((FFN_TM, D), lambda i: (i, 0)),
            _mod_spec(layer, lambda i: i // tiles_per_group),
            _const_spec((1, D), layer, k),
            _const_spec((1, D), layer, k),
            _const_spec((D, 2 * D_FF), layer, slot),
            _const_spec((D_FF, D), layer, slot),
        ],
        out_specs=pl.BlockSpec((FFN_TM, D), lambda i: (i, 0)),
        out_shape=jax.ShapeDtypeStruct((T_ALL, D), F32),
        compiler_params=_cparams(1),
        name="half_ffn",
    )(x, mods, pre, post, w1, w2)


def _dft_cos_sin(n):
    idx = np.arange(n)
    ang = 2.0 * np.pi * ((idx[:, None] * idx[None, :]) % n) / n
    return np.cos(ang), np.sin(ang)


def _fourier_constants():
    c128, s128 = _dft_cos_sin(FG_DIM)
    chan = np.stack([c128, s128]) / np.sqrt(FG_DIM)

    c256, s256 = _dft_cos_sin(SEQ)
    dft_prompt = np.concatenate([c256, -s256], axis=0) / np.sqrt(SEQ)

    c64, s64 = _dft_cos_sin(GRID)
    c64 = jnp.asarray(c64 / np.sqrt(GRID), F32)
    s64 = jnp.asarray(s64 / np.sqrt(GRID), F32)
    eye = jnp.eye(FS_TN, dtype=F32)
    rows = GRID * FS_TN
    def kron_eye(f):
        return (f[:, None, :, None] * eye[None, :, None, :]).reshape(rows, rows)
    m1 = jnp.concatenate([kron_eye(c64), kron_eye(-s64)], axis=0)
    def spread(f):
        return (f[:, None, None, :] * eye[None, :, :, None]).reshape(rows, rows)
    m2 = jnp.block([[spread(c64), spread(s64)], [spread(-s64), spread(c64)]])
    k1 = np.arange(GRID)[:, None]
    n2 = np.arange(GRID)[None, :]
    ang = 2.0 * np.pi * (k1 * n2) / LAT
    tw = np.stack([np.cos(ang), np.sin(ang)])[..., None] * np.ones((1, 1, 1, 128))
    return (jnp.asarray(chan, F32), jnp.asarray(dft_prompt, F32).astype(BF16),
            m1.astype(BF16), m2.astype(BF16), jnp.asarray(tw, F32))


def _fold_body(cs_ref, w_ref, o_ref):
    w = w_ref[...]
    for t in range(2):
        o_ref[t] = jnp.dot(cs_ref[t], w, preferred_element_type=F32,
                           precision=lax.Precision.HIGHEST).astype(BF16)


def _fold_channel_dft(chan, w_out):
    return pl.pallas_call(
        _fold_body,
        grid=(N_FG,),
        in_specs=[
            pl.BlockSpec((2, FG_DIM, FG_DIM), lambda g: (0, 0, 0)),
            pl.BlockSpec((FG_DIM, D), lambda g: (g, 0)),
        ],
        out_specs=pl.BlockSpec((2, FG_DIM, D), lambda g: (0, g, 0)),
        out_shape=jax.ShapeDtypeStruct((2, D, D), BF16),
        compiler_params=_cparams(1),
        name="fold_channel_dft",
    )(chan, w_out)


def _four_prompt_body(x_ref, mod_ref, gpre_ref, gpost_ref, win_ref, dft_ref, w2_ref, o_ref):
    x = x_ref[...]
    h = _prenorm(x, gpre_ref[...], mod_ref, 1)
    u = _dot(h, win_ref[...]).astype(BF16)
    p = _dot(dft_ref[...], u)
    y = _dot(p[:SEQ].astype(BF16), w2_ref[0]) + _dot(p[SEQ:].astype(BF16), w2_ref[1])
    o_ref[...] = x + mod_ref[5:6, :] * _postnorm(y, gpost_ref[...])


def _fourier_prompt(x, mods, pre, post, w_in, dft_prompt, w2, layer):
    return pl.pallas_call(
        _four_prompt_body,
        grid=(BATCH,),
        in_specs=[
            pl.BlockSpec((SEQ, D), lambda b: (b, 0)),
            _mod_spec(layer, lambda b: 0),
            _const_spec((1, D), layer, 1),
            _const_spec((1, D), layer, 1),
            _const_spec((D, D), layer // 2),
            _const_spec((2 * SEQ, SEQ)),
            _const_spec((2, D, D)),
        ],
        out_specs=pl.BlockSpec((SEQ, D), lambda b: (b, 0)),
        out_shape=jax.ShapeDtypeStruct((T_ALL, D), F32),
        input_output_aliases={0: 0},
        compiler_params=_cparams(1),
        name="fourier_prompt",
    )(x, mods, pre, post, w_in, dft_prompt, w2)


def _four_s1_body(x_ref, mod_ref, gpre_ref, win_ref, m1_ref, tw_ref, y_ref):
    rows = GRID * FS_TN
    x = x_ref[...].reshape(rows, D)
    h = _prenorm(x, gpre_ref[...], mod_ref, 1)
    u = _dot(h, win_ref[...]).astype(BF16)
    y = _dot(m1_ref[...], u)
    tc = tw_ref[0].reshape(rows, 128)
    ts = tw_ref[1].reshape(rows, 128)
    for l in range(D // 128):
        sl = slice(l * 128, (l + 1) * 128)
        yr = y[:rows, sl]
        yi = y[rows:, sl]
        y_ref[0, :, :, sl] = (yr * tc + yi * ts).reshape(GRID, FS_TN, 128).astype(BF16)
        y_ref[1, :, :, sl] = (yi * tc - yr * ts).reshape(GRID, FS_TN, 128).astype(BF16)


def _fourier_latent_stage1(x4, mods, pre, w_in, m1, tw, layer):
    rows = GRID * FS_TN
    return pl.pallas_call(
        _four_s1_body,
        grid=(N_LAT, GRID // FS_TN),
        in_specs=[
            pl.BlockSpec((None, GRID, FS_TN, D), lambda b, j: (b + 1, 0, j, 0)),
            _mod_spec(layer, lambda b, j: b + 1),
            _const_spec((1, D), layer, 1),
            _const_spec((D, D), layer // 2),
            _const_spec((2 * rows, rows)),
            pl.BlockSpec((2, GRID, FS_TN, 128), lambda b, j: (0, 0, j, 0)),
        ],
        out_specs=pl.BlockSpec((None, 2, GRID, FS_TN, D), lambda b, j: (b, 0, 0, j, 0)),
        out_shape=jax.ShapeDtypeStruct((N_LAT, 2, GRID, GRID, D), BF16),
        compiler_params=_cparams(2),
        name="fourier_latent_stage1",
    )(x4, mods, pre, w_in, m1, tw)


def _four_s2_body(y_ref, x_ref, mod_ref, gpost_ref, m2_ref, w2_ref, o_ref):
    rows = GRID * FS_TN
    yin = y_ref[...].reshape(2 * rows, D)
    z = _dot(m2_ref[...], yin)
    y = _dot(z[:rows].astype(BF16), w2_ref[0]) + _dot(z[rows:].astype(BF16), w2_ref[1])
    x = x_ref[...].reshape(rows, D)
    out = x + mod_ref[5:6, :] * _postnorm(y, gpost_ref[...])
    o_ref[...] = out.reshape(GRID, FS_TN, D)


def _fourier_latent_stage2(y5, x4, mods, post, m2, w2, layer):
    rows = GRID * FS_TN
    return pl.pallas_call(
        _four_s2_body,
        grid=(N_LAT, GRID // FS_TN),
        in_specs=[
            pl.BlockSpec((None, 2, FS_TN, GRID, D), lambda b, j: (b, 0, j, 0, 0)),
            pl.BlockSpec((None, GRID, FS_TN, D), lambda b, j: (b + 1, 0, j, 0)),
            _mod_spec(layer, lambda b, j: b + 1),
            _const_spec((1, D), layer, 1),
            _const_spec((2 * rows, 2 * rows)),
            _const_spec((2, D, D)),
        ],
        out_specs=pl.BlockSpec((None, GRID, FS_TN, D), lambda b, j: (b + 1, 0, j, 0)),
        out_shape=jax.ShapeDtypeStruct((N_GROUPS, GRID, GRID, D), F32),
        input_output_aliases={1: 0},
        compiler_params=_cparams(2),
        name="fourier_latent_stage2",
    )(y5, x4, mods, post, m2, w2)


def _ctx_attn_body(x_ref, mod_ref, gpre_ref, gpost_ref, wqkv_ref, wout_ref, *rest):
    o_ref, kc_ref, vc_ref = rest[-3:]
    x = x_ref[...]
    h = _prenorm(x, gpre_ref[...], mod_ref, 1)
    qkv = _dot(h, wqkv_ref[...])
    k = qkv[:, D:2 * D]
    v = qkv[:, 2 * D:]
    for hd in range(N_HEADS):
        sl = slice(hd * HEAD_DIM, (hd + 1) * HEAD_DIM)
        kc_ref[hd] = k[:, sl]
        vc_ref[hd] = v[:, sl]
    qb = (qkv[:, :D] * Q_SCALE).astype(BF16)
    kb = k.astype(BF16)
    vb = v.astype(BF16)
    first = lax.broadcasted_iota(jnp.int32, (1, 2 * HEAD_DIM), 1) < HEAD_DIM
    zero = jnp.zeros((), BF16)
    outs = []
    for p in range(N_PAIRS):
        sl = slice(p * 2 * HEAD_DIM, (p + 1) * 2 * HEAD_DIM)
        kp = kb[:, sl]
        vp = vb[:, sl]
        k2 = jnp.concatenate([jnp.where(first, kp, zero), jnp.where(first, zero, kp)], axis=0)
        v2 = jnp.concatenate([jnp.where(first, vp, zero), jnp.where(first, zero, vp)], axis=0)
        s = _dot_nt(qb[:, sl], k2)
        probs = []
        for hh in range(2):
            sh = s[:, hh * SEQ:(hh + 1) * SEQ]
            e = jnp.exp2(sh - jnp.max(sh, axis=-1, keepdims=True))
            probs.append(e / jnp.sum(e, axis=-1, keepdims=True))
        outs.append(_dot(jnp.concatenate(probs, axis=1).astype(BF16), v2))
    o = jnp.concatenate(outs, axis=1).astype(BF16)
    y = _dot(o, wout_ref[...])
    o_ref[...] = x + mod_ref[5:6, :] * _postnorm(y, gpost_ref[...])


def _context_attention(x, mods, pre, post, w_qkv, w_out, layer, caches):
    j = layer // 2
    cache = jax.ShapeDtypeStruct((BATCH, DEPTH // 2, N_HEADS, SEQ, HEAD_DIM), F32)
    cache_spec = pl.BlockSpec((None, None, N_HEADS, SEQ, HEAD_DIM), lambda b: (b, j, 0, 0, 0))
    in_specs = [
        pl.BlockSpec((SEQ, D), lambda b: (b, 0)),
        _mod_spec(layer, lambda b: 0),
        _const_spec((1, D), layer, 1),
        _const_spec((1, D), layer, 1),
        _const_spec((D, 3 * D), j),
        _const_spec((D, D), j),
    ]
    aliases = {0: 0}
    if caches:
        in_specs += [pl.BlockSpec(memory_space=pl.ANY)] * 2
        aliases.update({6: 1, 7: 2})
    return pl.pallas_call(
        _ctx_attn_body,
        grid=(BATCH,),
        in_specs=in_specs,
        out_specs=[pl.BlockSpec((SEQ, D), lambda b: (b, 0)), cache_spec, cache_spec],
        out_shape=[jax.ShapeDtypeStruct((T_ALL, D), F32), cache, cache],
        input_output_aliases=aliases,
        compiler_params=_cparams(1),
        name="context_attention",
    )(x, mods, pre, post, w_qkv, w_out, *caches)


def _qkv_lat_body(x_ref, mod_ref, gpre_ref, wq_ref, wkt_ref, wv_ref, q_ref, kt_ref, v_ref):
    h = _prenorm(x_ref[...], gpre_ref[...], mod_ref, 1)
    q_ref[...] = (_dot(h, wq_ref[...]) * Q_SCALE).astype(BF16)
    v_ref[...] = _dot(h, wv_ref[...]).astype(BF16)
    kt_ref[...] = _dot_nt(wkt_ref[...], h).astype(BF16)


def _qkv_latent(x, mods, pre, w_qkv, wkt, layer):
    j = layer // 2
    per_lat = LAT // QKV_TM
    first = GROUP_ROWS // QKV_TM
    tok = jax.ShapeDtypeStruct((N_LAT * LAT, D), BF16)

    def w_cols(col_block):
        return pl.BlockSpec((None, D, D), lambda i: (j, 0, col_block), pipeline_mode=pl.Buffered(1))

    return pl.pallas_call(
        _qkv_lat_body,
        grid=(N_LAT * per_lat,),
        in_specs=[
            pl.BlockSpec((QKV_TM, D), lambda i: (first + i, 0)),
            _mod_spec(layer, lambda i: 1 + i // per_lat),
            _const_spec((1, D), layer, 1),
            w_cols(0),
            _const_spec((D, D), j),
            w_cols(2),
        ],
        out_specs=[
            pl.BlockSpec((QKV_TM, D), lambda i: (i, 0)),
            pl.BlockSpec((None, D, QKV_TM), lambda i: (i // per_lat, 0, i % per_lat)),
            pl.BlockSpec((QKV_TM, D), lambda i: (i, 0)),
        ],
        out_shape=[tok, jax.ShapeDtypeStruct((N_LAT, D, LAT), BF16), tok],
        compiler_params=_cparams(1),
        name="qkv_latent",
    )(x, mods, pre, w_qkv, wkt, w_qkv)


def _na_block_geometry(blk):
    return min(max(NA_ROWS * blk - WIN_ROWS // 2, 0), GRID - NA_KROWS)


_NA_VARIANT_BLOCKS = (0, 1, 2, NA_BLOCKS - 2, NA_BLOCKS - 1)


def _na_tables():
    t = np.arange(NA_DT)[:, None]
    e = np.arange(2)[None, :]
    drow = t - 11 + e
    row_ok = (drow >= -(WIN_ROWS - 1)) & (drow <= WIN_ROWS - 1)
    c = np.arange(GRID)[:, None]
    cp = np.arange(GRID)[None, :]
    cs = np.clip(c - WIN_COLS // 2, 0, GRID - WIN_COLS)
    col_ok = (cp >= cs) & (cp < cs + WIN_COLS)
    ok = row_ok[:, None, :, None] & col_ok[None, :, None, :]
    rowmask = np.zeros((24, NA_K), np.float32)
    delta0 = np.zeros((NA_VARIANTS, NA_ROWS, NA_KROWS // 2), np.int64)
    for v, blk in enumerate(_NA_VARIANT_BLOCKS):
        u0 = _na_block_geometry(blk)
        for rl in range(NA_ROWS):
            r = NA_ROWS * blk + rl
            rs = min(max(r - WIN_ROWS // 2, 0), GRID - WIN_ROWS)
            kr = u0 + np.arange(NA_KROWS)
            valid = (kr >= rs) & (kr < rs + WIN_ROWS)
            rowmask[v * NA_ROWS + rl] = np.repeat(np.where(valid, 0.0, NEG), GRID)
            delta0[v, rl] = u0 - r + 2 * np.arange(NA_KROWS // 2) + 11
    assert delta0.min() >= 0 and delta0.max() < NA_DT
    return ok, rowmask, delta0


def _na_bias_pairs(rpb, ok):
    nrow = 2 * WIN_ROWS - 1
    period = 2 * GRID
    fill = jnp.full((N_HEADS, nrow, period - (2 * WIN_COLS - 1)), NEG, F32)
    p = jnp.concatenate([rpb[:, :, WIN_COLS - 1:], fill, rpb[:, :, :WIN_COLS - 1]], axis=-1)
    toep = jnp.tile(p, (1, 1, GRID))[..., :GRID * (period - 1)]
    toep = toep.reshape(N_HEADS, nrow, GRID, period - 1)[..., :GRID]
    masked = jnp.full((N_HEADS, GRID, GRID), NEG, F32)

    def rows(drow):
        idx = drow + WIN_ROWS - 1
        return toep[:, idx] if 0 <= idx < nrow else masked

    tab = jnp.stack([jnp.concatenate([rows(t - 11), rows(t - 10)], axis=-1)
                     for t in range(NA_DT)], axis=1)
    return jnp.where(ok.reshape(1, NA_DT, GRID, 2 * GRID), tab * LOG2E, NEG)


def _natten_body(q_ref, kt_ref, v_ref, kct_ref, vc_ref, bp_ref, rm_ref, o_ref, bias_scr,
                 *, delta0):
    for hh in range(2):
        for var in range(NA_VARIANTS):
            for rl in range(NA_ROWS):
                strip = jnp.concatenate(
                    [bp_ref[hh, int(delta0[var, rl, i])] for i in range(NA_KROWS // 2)], axis=1)
                row = var * NA_ROWS + rl
                bias_scr[hh, var, rl * GRID:(rl + 1) * GRID, :] = strip + rm_ref[row:row + 1, :]

    first = lax.broadcasted_iota(jnp.int32, (1, 2 * HEAD_DIM), 1) < HEAD_DIM
    zero = jnp.zeros((), BF16)
    kct = kct_ref[...]
    vc = vc_ref[...]

    def block(blk, carry):
        koff = pl.multiple_of(jnp.clip(blk - 1, 0, NA_BLOCKS - 3) * NA_Q, NA_Q)
        qoff = pl.multiple_of(blk * NA_Q, NA_Q)
        var = jnp.where(blk < 2, blk, jnp.where(blk < NA_BLOCKS - 2, 2, blk - (NA_BLOCKS - 5)))
        qb = q_ref[pl.ds(qoff, NA_Q), :]
        ktb = kt_ref[:, pl.ds(koff, NA_K)]
        vb = v_ref[pl.ds(koff, NA_K), :]
        o = jnp.zeros((NA_Q, 2 * HEAD_DIM), F32)
        for hh in range(2):
            keep = first if hh == 0 else jnp.logical_not(first)
            qh = jnp.where(keep, qb, zero)
            s_loc = _dot(qh, ktb) + bias_scr[hh, var]
            s_ctx = _dot(qh, kct)
            mx = jnp.maximum(jnp.max(s_loc, axis=-1, keepdims=True),
                             jnp.max(s_ctx, axis=-1, keepdims=True))
            e_loc = jnp.exp2(s_loc - mx)
            e_ctx = jnp.exp2(s_ctx - mx)
            den = jnp.sum(e_loc, axis=-1, keepdims=True) + jnp.sum(e_ctx, axis=-1, keepdims=True)
            oh = (_dot(e_loc.astype(BF16), jnp.where(keep, vb, zero))
                  + _dot(e_ctx.astype(BF16), jnp.where(keep, vc, zero)))
            o = o + oh / den
        o_ref[pl.ds(qoff, NA_Q), :] = o.astype(BF16)
        return carry

    lax.fori_loop(0, NA_BLOCKS, block, 0, unroll=2)


def _neighbourhood_attention(q, kt, v, kct, vc, bias_pairs, rowmask, delta0):
    lane = 2 * HEAD_DIM
    return pl.pallas_call(
        functools.partial(_natten_body, delta0=delta0),
        grid=(N_LAT, N_PAIRS),
        in_specs=[
            pl.BlockSpec((LAT, lane), lambda b, p: (b, p)),
            pl.BlockSpec((None, lane, LAT), lambda b, p: (b, p, 0)),
            pl.BlockSpec((LAT, lane), lambda b, p: (b, p)),
            pl.BlockSpec((None, lane, PAST), lambda b, p: (b, p, 0)),
            pl.BlockSpec((None, PAST, lane), lambda b, p: (b, 0, p)),
            pl.BlockSpec((2, NA_DT, GRID, lane), lambda b, p: (p, 0, 0, 0)),
            pl.BlockSpec((24, NA_K), lambda b, p: (0, 0)),
        ],
        out_specs=pl.BlockSpec((LAT, lane), lambda b, p: (b, p)),
        out_shape=jax.ShapeDtypeStruct((N_LAT * LAT, D), BF16),
        scratch_shapes=[pltpu.VMEM((2, NA_VARIANTS, NA_Q, NA_K), F32)],
        compiler_params=_cparams(2),
        name="neighbourhood_attention",
    )(q, kt, v, kct, vc, bias_pairs, rowmask)


def _attn_out_body(o_ref, x_ref, mod_ref, gpost_ref, wout_ref, out_ref):
    y = _dot(o_ref[...], wout_ref[...])
    out_ref[...] = x_ref[...] + mod_ref[5:6, :] * _postnorm(y, gpost_ref[...])


def _attention_out_latent(o, x, mods, post, w_out, layer):
    per_lat = LAT // OUT_TM
    first = GROUP_ROWS // OUT_TM
    return pl.pallas_call(
        _attn_out_body,
        grid=(N_LAT * per_lat,),
        in_specs=[
            pl.BlockSpec((OUT_TM, D), lambda i: (i, 0)),
            pl.BlockSpec((OUT_TM, D), lambda i: (first + i, 0)),
            _mod_spec(layer, lambda i: 1 + i // per_lat),
            _const_spec((1, D), layer, 1),
            _const_spec((D, D), layer // 2),
        ],
        out_specs=pl.BlockSpec((OUT_TM, D), lambda i: (first + i, 0)),
        out_shape=jax.ShapeDtypeStruct((T_ALL, D), F32),
        input_output_aliases={1: 0},
        compiler_params=_cparams(1),
        name="attention_out_latent",
    )(o, x, mods, post, w_out)


def kernel(x_prompt, x_sample, cache_k, cache_v, c, c_ctx, w_mod, b_mod, norm_pre, norm_post,
           ffn_w1, ffn_w2, four_w_in, four_w_out, na_w_qkv, na_w_out, na_rpb):
    x = jnp.concatenate([x_prompt.reshape(BATCH * SEQ, D), x_sample.reshape(N_LAT * LAT, D)])

    conds = jnp.concatenate([c_ctx[None, :], c, jnp.zeros((8 - 1 - N_LAT, D), F32)], axis=0)
    mods = _modulation(conds, w_mod, b_mod)[:, :N_GROUPS].reshape(DEPTH, N_GROUPS, N_MOD, D)

    chan, dft_prompt, m1, m2, tw = _fourier_constants()
    ok, rowmask, delta0 = _na_tables()
    rowmask = jnp.asarray(rowmask)

    w1 = ffn_w1.astype(BF16)
    w2 = ffn_w2.astype(BF16)
    w_in = four_w_in.astype(BF16)
    w_qkv = na_w_qkv.astype(BF16)
    w_out = na_w_out.astype(BF16)
    wkt = w_qkv[:, :, D:2 * D].transpose(0, 2, 1)
    pre = norm_pre[:, :, None, :]
    post = norm_post[:, :, None, :]

    caches = ()
    for i in range(DEPTH):
        j = i // 2
        x = _half_ffn(x, mods, pre, post, w1, w2, i, 0)
        if i % 2 == 0:
            wf = _fold_channel_dft(chan, four_w_out[j])
            x = _fourier_prompt(x, mods, pre, post, w_in, dft_prompt, wf, i)
            x4 = x.reshape(N_GROUPS, GRID, GRID, D)
            y5 = _fourier_latent_stage1(x4, mods, pre, w_in, m1, tw, i)
            x4 = _fourier_latent_stage2(y5, x4, mods, post, m2, wf, i)
            x = x4.reshape(T_ALL, D)
        else:
            x, *caches = _context_attention(x, mods, pre, post, w_qkv, w_out, i, caches)
            q, kt, v = _qkv_latent(x, mods, pre, w_qkv, wkt, i)
            kct = cache_k[:, j].transpose(0, 1, 3, 2).reshape(N_LAT, D, PAST).astype(BF16)
            vct = cache_v[:, j].transpose(0, 2, 1, 3).reshape(N_LAT, PAST, D).astype(BF16)
            bias_pairs = _na_bias_pairs(na_rpb[j], ok)
            o = _neighbourhood_attention(q, kt, v, kct, vct, bias_pairs, rowmask, delta0)
            x = _attention_out_latent(o, x, mods, post, w_out, i)
        x = _half_ffn(x, mods, pre, post, w1, w2, i, 1)

    y_prompt = x[:BATCH * SEQ].reshape(BATCH, SEQ, D)
    y_sample = x[BATCH * SEQ:].reshape(N_LAT, LAT, D)
    return (y_prompt, y_sample, caches[0], caches[1])
```

```python
import functools

import numpy as np
import jax
import jax.numpy as jnp
from jax import lax
from jax.experimental import pallas as pl
from jax.experimental.pallas import tpu as pltpu

F32 = jnp.float32
BF16 = jnp.bfloat16

D = 1024
D_FF = 2816
DEPTH = 4
N_MOD = 9
N_HEADS = 16
HEAD_DIM = 64
N_PAIRS = N_HEADS // 2
SEQ = 256
BATCH = 16
GRID = 64
LAT = GRID * GRID
N_LAT = 2
GROUP_ROWS = 4096
N_GROUPS = 3
T_ALL = N_GROUPS * GROUP_ROWS
PAST = 256
WIN_ROWS = 8
WIN_COLS = 16
N_FG = 8
FG_DIM = D // N_FG
EPS = 1e-6
NEG = -1e30
LOG2E = 1.4426950408889634
Q_SCALE = HEAD_DIM ** -0.5 * LOG2E

VMEM_LIMIT = 56 * 1024 * 1024

FFN_TM = 1024
FFN_SUB = 256
FFN_NC = 1
MOD_TN = 2304
QKV_TM = 512
OUT_TM = 512
FS_TN = 16
FS_SUB = 8
NA_ROWS = 4
NA_Q = NA_ROWS * GRID
NA_KROWS = 12
NA_K = NA_KROWS * GRID
NA_BLOCKS = GRID // NA_ROWS
NA_VARIANTS = 3
NA_MASK_ROWS = 16
NA_DT = 22


def _cparams(n_axes):
    return pltpu.CompilerParams(
        dimension_semantics=("arbitrary",) * n_axes, vmem_limit_bytes=VMEM_LIMIT)


def _const_spec(shape, *lead):
    nd = len(shape)
    return pl.BlockSpec((None,) * len(lead) + tuple(shape), lambda *_: tuple(lead) + (0,) * nd,
                        pipeline_mode=pl.Buffered(1))


def _mod_spec(layer, group_of):
    return pl.BlockSpec((None, None, N_MOD, D), lambda *ids: (layer, group_of(*ids), 0, 0))


def _silu(x):
    return x * jax.nn.sigmoid(x)


def _prenorm(x, g, mod_ref, k):
    shift = mod_ref[3 * k:3 * k + 1, :]
    scale = mod_ref[3 * k + 1:3 * k + 2, :]
    y = x * lax.rsqrt(jnp.mean(x * x, axis=-1, keepdims=True) + EPS)
    return ((y * g) * (1.0 + scale) + shift).astype(BF16)


def _postnorm(y, g):
    return (y * lax.rsqrt(jnp.mean(y * y, axis=-1, keepdims=True) + EPS)) * g


def _dot(a, b):
    return jnp.dot(a, b, preferred_element_type=F32)


def _dot_nt(a, b):
    return lax.dot_general(a, b, (((1,), (1,)), ((), ())), preferred_element_type=F32)


def _mod_body(cond_ref, w_ref, b_ref, o_ref):
    s = _silu(cond_ref[...]).astype(BF16)
    o_ref[...] = _dot(s, w_ref[...].astype(BF16)) + b_ref[...]


def _modulation(conds, w_mod, b_mod):
    n = N_MOD * D
    return pl.pallas_call(
        _mod_body,
        grid=(DEPTH, n // MOD_TN),
        in_specs=[
            pl.BlockSpec((8, D), lambda l, j: (0, 0)),
            pl.BlockSpec((None, D, MOD_TN), lambda l, j: (l, 0, j)),
            pl.BlockSpec((None, 1, MOD_TN), lambda l, j: (l, 0, j)),
        ],
        out_specs=pl.BlockSpec((None, 8, MOD_TN), lambda l, j: (l, 0, j)),
        out_shape=jax.ShapeDtypeStruct((DEPTH, 8, n), F32),
        compiler_params=_cparams(2),
        name="modulation",
    )(conds, w_mod, b_mod.reshape(DEPTH, 1, n))


def _ffn_body(x_ref, mod_ref, gpre_ref, gpost_ref, w1_ref, w2_ref, o_ref, *, k):
    ck = D_FF // FFN_NC
    gate = mod_ref[3 * k + 2:3 * k + 3, :]
    for r in range(FFN_TM // FFN_SUB):
        rows = slice(r * FFN_SUB, (r + 1) * FFN_SUB)
        x = x_ref[rows, :]
        h = _prenorm(x, gpre_ref[...], mod_ref, k)
        y = None
        for c in range(FFN_NC):
            g = _dot(h, w1_ref[:, c * ck:(c + 1) * ck])
            u = _dot(h, w1_ref[:, D_FF + c * ck:D_FF + (c + 1) * ck])
            a = (_silu(g) * u).astype(BF16)
            yc = _dot(a, w2_ref[c * ck:(c + 1) * ck, :])
            y = yc if y is None else y + yc
        o_ref[rows, :] = x + (0.5 * gate) * _postnorm(y, gpost_ref[...])


def _ffn_body_carry(x_ref, mod_ref, gpre_ref, gpost_ref, w1_ref, w2_ref, carry_ref, o_ref, *, k):
    del carry_ref
    _ffn_body(x_ref, mod_ref, gpre_ref, gpost_ref, w1_ref, w2_ref, o_ref, k=k)


def _half_ffn(src, mods, pre, post, w1, w2, layer, slot, *, first_row=0, n_rows=T_ALL,
              src_row=None, dst_rows=T_ALL, dst_row=None, carry=None):
    k = 2 * slot
    tiles_per_group = GROUP_ROWS // FFN_TM
    t0 = first_row // FFN_TM
    s0 = t0 if src_row is None else src_row // FFN_TM
    d0 = t0 if dst_row is None else dst_row // FFN_TM
    in_specs = [
        pl.BlockSpec((FFN_TM, D), lambda i: (s0 + i, 0)),
        _mod_spec(layer, lambda i: (t0 + i) // tiles_per_group),
        _const_spec((1, D), layer, k),
        _const_spec((1, D), layer, k),
        _const_spec((D, 2 * D_FF), layer, slot),
        _const_spec((D_FF, D), layer, slot),
    ]
    args = [src, mods, pre, post, w1, w2]
    body, aliases = _ffn_body, {}
    if carry is not None:
        in_specs.append(pl.BlockSpec(memory_space=pl.ANY))
        args.append(carry)
        body, aliases = _ffn_body_carry, {6: 0}
    return pl.pallas_call(
        functools.partial(body, k=k),
        grid=(n_rows // FFN_TM,),
        in_specs=in_specs,
        out_specs=pl.BlockSpec((FFN_TM, D), lambda i: (d0 + i, 0)),
        out_shape=jax.ShapeDtypeStruct((dst_rows, D), F32),
        input_output_aliases=aliases,
        compiler_params=_cparams(1),
        name="half_ffn",
    )(*args)


def _dft_cos_sin(n):
    idx = np.arange(n)
    ang = 2.0 * np.pi * ((idx[:, None] * idx[None, :]) % n) / n
    return np.cos(ang), np.sin(ang)


def _fourier_constants():
    c128, s128 = _dft_cos_sin(FG_DIM)
    chan = np.stack([c128, s128]) / np.sqrt(FG_DIM)

    c256, s256 = _dft_cos_sin(SEQ)
    dft_prompt = np.concatenate([c256, -s256], axis=0) / np.sqrt(SEQ)

    c64, s64 = _dft_cos_sin(GRID)
    c64 = c64 / np.sqrt(GRID)
    s64 = s64 / np.sqrt(GRID)
    eye = np.eye(FS_SUB)
    rows = GRID * FS_SUB
    m1 = np.concatenate([np.kron(c64, eye), np.kron(-s64, eye)], axis=0)
    def spread(f):
        return np.einsum("kn,jl->kjln", f, eye).reshape(rows, rows)
    m2 = np.block([[spread(c64), spread(s64)], [spread(-s64), spread(c64)]])
    k1 = np.arange(GRID)[:, None]
    n2 = np.arange(GRID)[None, :]
    ang = 2.0 * np.pi * (k1 * n2) / LAT
    tw = np.stack([np.cos(ang), np.sin(ang)])[..., None] * np.ones((1, 1, 1, 128))

    def mxu_const(a):
        return jnp.asarray(a, F32).astype(BF16)

    return (jnp.asarray(chan, F32), mxu_const(dft_prompt), mxu_const(m1), mxu_const(m2),
            jnp.asarray(tw, F32))


def _fold_body(cs_ref, w_ref, o_ref):
    w = w_ref[...]
    for t in range(2):
        o_ref[t] = jnp.dot(cs_ref[t], w, preferred_element_type=F32,
                           precision=lax.Precision.HIGHEST).astype(BF16)


def _fold_channel_dft(chan, w_out):
    return pl.pallas_call(
        _fold_body,
        grid=(N_FG,),
        in_specs=[
            pl.BlockSpec((2, FG_DIM, FG_DIM), lambda g: (0, 0, 0)),
            pl.BlockSpec((FG_DIM, D), lambda g: (g, 0)),
        ],
        out_specs=pl.BlockSpec((2, FG_DIM, D), lambda g: (0, g, 0)),
        out_shape=jax.ShapeDtypeStruct((2, D, D), BF16),
        compiler_params=_cparams(1),
        name="fold_channel_dft",
    )(chan, w_out)


def _four_prompt_body(x_ref, mod_ref, gpre_ref, gpost_ref, win_ref, dft_ref, w2_ref, o_ref):
    x = x_ref[...]
    h = _prenorm(x, gpre_ref[...], mod_ref, 1)
    u = _dot(h, win_ref[...]).astype(BF16)
    p = _dot(dft_ref[...], u)
    y = _dot(p[:SEQ].astype(BF16), w2_ref[0]) + _dot(p[SEQ:].astype(BF16), w2_ref[1])
    o_ref[...] = x + mod_ref[5:6, :] * _postnorm(y, gpost_ref[...])


def _fourier_prompt(x, mods, pre, post, w_in, dft_prompt, w2, layer):
    return pl.pallas_call(
        _four_prompt_body,
        grid=(BATCH,),
        in_specs=[
            pl.BlockSpec((SEQ, D), lambda b: (b, 0)),
            _mod_spec(layer, lambda b: 0),
            _const_spec((1, D), layer, 1),
            _const_spec((1, D), layer, 1),
            _const_spec((D, D), layer // 2),
            _const_spec((2 * SEQ, SEQ)),
            _const_spec((2, D, D)),
        ],
        out_specs=pl.BlockSpec((SEQ, D), lambda b: (b, 0)),
        out_shape=jax.ShapeDtypeStruct((T_ALL, D), F32),
        input_output_aliases={0: 0},
        compiler_params=_cparams(1),
        name="fourier_prompt",
    )(x, mods, pre, post, w_in, dft_prompt, w2)


def _four_s1_body(x_ref, mod_ref, gpre_ref, win_ref, m1_ref, tw_ref, y_ref, y_scr):
    rows = GRID * FS_SUB
    for s in range(FS_TN // FS_SUB):
        cols = slice(s * FS_SUB, (s + 1) * FS_SUB)
        x = x_ref[:, cols, :].reshape(rows, D)
        h = _prenorm(x, gpre_ref[...], mod_ref, 1)
        u = _dot(h, win_ref[...]).astype(BF16)
        y = _dot(m1_ref[...], u)
        tc = tw_ref[0, :, cols, :].reshape(rows, 128)
        ts = tw_ref[1, :, cols, :].reshape(rows, 128)
        for l in range(D // 128):
            sl = slice(l * 128, (l + 1) * 128)
            yr = y[:rows, sl]
            yi = y[rows:, sl]
            y_scr[0, :, cols, sl] = (yr * tc + yi * ts).reshape(GRID, FS_SUB, 128)
            y_scr[1, :, cols, sl] = (yi * tc - yr * ts).reshape(GRID, FS_SUB, 128)
    y_ref[...] = y_scr[...].astype(BF16)


def _fourier_latent_stage1(x4, mods, pre, w_in, m1, tw, layer):
    rows = GRID * FS_SUB
    return pl.pallas_call(
        _four_s1_body,
        grid=(N_LAT, GRID // FS_TN),
        in_specs=[
            pl.BlockSpec((None, GRID, FS_TN, D), lambda b, j: (b + 1, 0, j, 0)),
            _mod_spec(layer, lambda b, j: b + 1),
            _const_spec((1, D), layer, 1),
            _const_spec((D, D), layer // 2),
            _const_spec((2 * rows, rows)),
            pl.BlockSpec((2, GRID, FS_TN, 128), lambda b, j: (0, 0, j, 0)),
        ],
        out_specs=pl.BlockSpec((None, 2, GRID, FS_TN, D), lambda b, j: (b, 0, 0, j, 0)),
        out_shape=jax.ShapeDtypeStruct((N_LAT, 2, GRID, GRID, D), BF16),
        scratch_shapes=[pltpu.VMEM((2, GRID, FS_TN, D), F32)],
        compiler_params=_cparams(2),
        name="fourier_latent_stage1",
    )(x4, mods, pre, w_in, m1, tw)


def _four_s2_body(y_ref, x_ref, mod_ref, gpost_ref, m2_ref, w2_ref, o_ref):
    rows = GRID * FS_SUB
    yin = y_ref[...].reshape(2 * rows, D)
    z = _dot(m2_ref[...], yin)
    y = _dot(z[:rows].astype(BF16), w2_ref[0]) + _dot(z[rows:].astype(BF16), w2_ref[1])
    x = x_ref[...].reshape(rows, D)
    out = x + mod_ref[5:6, :] * _postnorm(y, gpost_ref[...])
    o_ref[...] = out.reshape(GRID, FS_SUB, D)


def _fourier_latent_stage2(y5, x4, mods, post, m2, w2, layer):
    rows = GRID * FS_SUB
    return pl.pallas_call(
        _four_s2_body,
        grid=(N_LAT, GRID // FS_SUB),
        in_specs=[
            pl.BlockSpec((None, 2, FS_SUB, GRID, D), lambda b, j: (b, 0, j, 0, 0)),
            pl.BlockSpec((None, GRID, FS_SUB, D), lambda b, j: (b + 1, 0, j, 0)),
            _mod_spec(layer, lambda b, j: b + 1),
            _const_spec((1, D), layer, 1),
            _const_spec((2 * rows, 2 * rows)),
            _const_spec((2, D, D)),
        ],
        out_specs=pl.BlockSpec((None, GRID, FS_SUB, D), lambda b, j: (b + 1, 0, j, 0)),
        out_shape=jax.ShapeDtypeStruct((N_GROUPS, GRID, GRID, D), F32),
        input_output_aliases={1: 0},
        compiler_params=_cparams(2),
        name="fourier_latent_stage2",
    )(y5, x4, mods, post, m2, w2)


def _ctx_attn_body(x_ref, mod_ref, gpre_ref, gpost_ref, wqkv_ref, wout_ref, *rest):
    o_ref, kc_ref, vc_ref = rest[-3:]
    x = x_ref[...]
    h = _prenorm(x, gpre_ref[...], mod_ref, 1)
    qkv = _dot(h, wqkv_ref[...])
    k = qkv[:, D:2 * D]
    v = qkv[:, 2 * D:]
    for hd in range(N_HEADS):
        sl = slice(hd * HEAD_DIM, (hd + 1) * HEAD_DIM)
        kc_ref[hd] = k[:, sl]
        vc_ref[hd] = v[:, sl]
    qb = (qkv[:, :D] * Q_SCALE).astype(BF16)
    kb = k.astype(BF16)
    vb = v.astype(BF16)
    first = lax.broadcasted_iota(jnp.int32, (1, 2 * HEAD_DIM), 1) < HEAD_DIM
    zero = jnp.zeros((), BF16)
    outs = []
    for p in range(N_PAIRS):
        sl = slice(p * 2 * HEAD_DIM, (p + 1) * 2 * HEAD_DIM)
        kp = kb[:, sl]
        vp = vb[:, sl]
        k2 = jnp.concatenate([jnp.where(first, kp, zero), jnp.where(first, zero, kp)], axis=0)
        v2 = jnp.concatenate([jnp.where(first, vp, zero), jnp.where(first, zero, vp)], axis=0)
        s = _dot_nt(qb[:, sl], k2)
        probs = []
        for hh in range(2):
            sh = s[:, hh * SEQ:(hh + 1) * SEQ]
            e = jnp.exp2(sh - jnp.max(sh, axis=-1, keepdims=True))
            probs.append(e / jnp.sum(e, axis=-1, keepdims=True))
        outs.append(_dot(jnp.concatenate(probs, axis=1).astype(BF16), v2))
    o = jnp.concatenate(outs, axis=1).astype(BF16)
    y = _dot(o, wout_ref[...])
    o_ref[...] = x + mod_ref[5:6, :] * _postnorm(y, gpost_ref[...])


def _context_attention(x, mods, pre, post, w_qkv, w_out, layer, caches):
    j = layer // 2
    cache = jax.ShapeDtypeStruct((BATCH, DEPTH // 2, N_HEADS, SEQ, HEAD_DIM), F32)
    cache_spec = pl.BlockSpec((None, None, N_HEADS, SEQ, HEAD_DIM), lambda b: (b, j, 0, 0, 0))
    in_specs = [
        pl.BlockSpec((SEQ, D), lambda b: (b, 0)),
        _mod_spec(layer, lambda b: 0),
        _const_spec((1, D), layer, 1),
        _const_spec((1, D), layer, 1),
        _const_spec((D, 3 * D), j),
        _const_spec((D, D), j),
    ]
    aliases = {0: 0}
    if caches:
        in_specs += [pl.BlockSpec(memory_space=pl.ANY)] * 2
        aliases.update({6: 1, 7: 2})
    return pl.pallas_call(
        _ctx_attn_body,
        grid=(BATCH,),
        in_specs=in_specs,
        out_specs=[pl.BlockSpec((SEQ, D), lambda b: (b, 0)), cache_spec, cache_spec],
        out_shape=[jax.ShapeDtypeStruct((T_ALL, D), F32), cache, cache],
        input_output_aliases=aliases,
        compiler_params=_cparams(1),
        name="context_attention",
    )(x, mods, pre, post, w_qkv, w_out, *caches)


def _qkv_lat_body(x_ref, mod_ref, gpre_ref, wq_ref, wkt_ref, wv_ref, q_ref, kt_ref, v_ref):
    h = _prenorm(x_ref[...], gpre_ref[...], mod_ref, 1)
    q_ref[...] = (_dot(h, wq_ref[...]) * Q_SCALE).astype(BF16)
    v_ref[...] = _dot(h, wv_ref[...]).astype(BF16)
    kt_ref[...] = _dot_nt(wkt_ref[...], h).astype(BF16)


def _qkv_latent(x, mods, pre, w_qkv, wkt, layer):
    j = layer // 2
    per_lat = LAT // QKV_TM
    first = GROUP_ROWS // QKV_TM
    tok = jax.ShapeDtypeStruct((N_LAT * LAT, D), BF16)

    def w_cols(col_block):
        return pl.BlockSpec((None, D, D), lambda i: (j, 0, col_block), pipeline_mode=pl.Buffered(1))

    return pl.pallas_call(
        _qkv_lat_body,
        grid=(N_LAT * per_lat,),
        in_specs=[
            pl.BlockSpec((QKV_TM, D), lambda i: (first + i, 0)),
            _mod_spec(layer, lambda i: 1 + i // per_lat),
            _const_spec((1, D), layer, 1),
            w_cols(0),
            _const_spec((D, D), j),
            w_cols(2),
        ],
        out_specs=[
            pl.BlockSpec((QKV_TM, D), lambda i: (i, 0)),
            pl.BlockSpec((None, D, QKV_TM), lambda i: (i // per_lat, 0, i % per_lat)),
            pl.BlockSpec((QKV_TM, D), lambda i: (i, 0)),
        ],
        out_shape=[tok, jax.ShapeDtypeStruct((N_LAT, D, LAT), BF16), tok],
        compiler_params=_cparams(1),
        name="qkv_latent",
    )(x, mods, pre, w_qkv, wkt, w_qkv)


def _na_block_geometry(blk):
    return min(max(NA_ROWS * blk - WIN_ROWS // 2, 0), GRID - NA_KROWS)


def _na_variant(blk):
    return 0 if blk == 0 else (2 if blk == NA_BLOCKS - 1 else 1)


def _na_tables():
    t = np.arange(NA_DT)[:, None]
    e = np.arange(2)[None, :]
    drow = t - 11 + e
    row_ok = (drow >= -(WIN_ROWS - 1)) & (drow <= WIN_ROWS - 1)
    c = np.arange(GRID)[:, None]
    cp = np.arange(GRID)[None, :]
    cs = np.clip(c - WIN_COLS // 2, 0, GRID - WIN_COLS)
    col_ok = (cp >= cs) & (cp < cs + WIN_COLS)
    ok = row_ok[:, None, :, None] & col_ok[None, :, None, :]
    rowmask = np.zeros((NA_MASK_ROWS, NA_K), np.float32)
    delta0 = np.zeros((NA_VARIANTS, NA_ROWS, NA_KROWS // 2), np.int64)
    seen = set()
    for blk in range(NA_BLOCKS):
        v = _na_variant(blk)
        u0 = _na_block_geometry(blk)
        for rl in range(NA_ROWS):
            r = NA_ROWS * blk + rl
            rs = min(max(r - WIN_ROWS // 2, 0), GRID - WIN_ROWS)
            kr = u0 + np.arange(NA_KROWS)
            valid = (kr >= rs) & (kr < rs + WIN_ROWS)
            mask = np.repeat(np.where(valid, 0.0, NEG), GRID).astype(np.float32)
            d0 = u0 - r + 2 * np.arange(NA_KROWS // 2) + 11
            if (v, rl) in seen:
                assert (rowmask[v * NA_ROWS + rl] == mask).all() and (delta0[v, rl] == d0).all()
            seen.add((v, rl))
            rowmask[v * NA_ROWS + rl] = mask
            delta0[v, rl] = d0
    assert delta0.min() >= 0 and delta0.max() < NA_DT
    return ok, rowmask, delta0


def _na_bias_pairs(rpb, ok):
    nrow = 2 * WIN_ROWS - 1
    period = 2 * GRID
    fill = jnp.full((N_HEADS, nrow, period - (2 * WIN_COLS - 1)), NEG, F32)
    p = jnp.concatenate([rpb[:, :, WIN_COLS - 1:], fill, rpb[:, :, :WIN_COLS - 1]], axis=-1)
    toep = jnp.tile(p, (1, 1, GRID))[..., :GRID * (period - 1)]
    toep = toep.reshape(N_HEADS, nrow, GRID, period - 1)[..., :GRID]
    masked = jnp.full((N_HEADS, GRID, GRID), NEG, F32)

    def rows(drow):
        idx = drow + WIN_ROWS - 1
        return toep[:, idx] if 0 <= idx < nrow else masked

    tab = jnp.stack([jnp.concatenate([rows(t - 11), rows(t - 10)], axis=-1)
                     for t in range(NA_DT)], axis=1)
    return jnp.where(ok.reshape(1, NA_DT, GRID, 2 * GRID), tab * LOG2E, NEG)


def _natten_body(q_ref, kt_ref, v_ref, kct_ref, vc_ref, bp_ref, rm_ref, o_ref, bias_scr,
                 *, delta0):
    for hh in range(2):
        for var in range(NA_VARIANTS):
            for rl in range(NA_ROWS):
                strip = jnp.concatenate(
                    [bp_ref[hh, int(delta0[var, rl, i])] for i in range(NA_KROWS // 2)], axis=1)
                row = var * NA_ROWS + rl
                bias_scr[hh, var, rl * GRID:(rl + 1) * GRID, :] = strip + rm_ref[row:row + 1, :]

    first = lax.broadcasted_iota(jnp.int32, (1, 2 * HEAD_DIM), 1) < HEAD_DIM
    zero = jnp.zeros((), BF16)
    kct = kct_ref[...]
    vc = vc_ref[...]

    def block(blk, carry):
        koff = pl.multiple_of(jnp.clip(blk - 1, 0, NA_BLOCKS - 3) * NA_Q, NA_Q)
        qoff = pl.multiple_of(blk * NA_Q, NA_Q)
        var = jnp.where(blk == 0, 0, jnp.where(blk == NA_BLOCKS - 1, 2, 1))
        qb = q_ref[pl.ds(qoff, NA_Q), :]
        ktb = kt_ref[:, pl.ds(koff, NA_K)]
        vb = v_ref[pl.ds(koff, NA_K), :]
        o = jnp.zeros((NA_Q, 2 * HEAD_DIM), F32)
        for hh in range(2):
            keep = first if hh == 0 else jnp.logical_not(first)
            qh = jnp.where(keep, qb, zero)
            s_loc = _dot(qh, ktb) + bias_scr[hh, var]
            s_ctx = _dot(qh, kct)
            mx = jnp.maximum(jnp.max(s_loc, axis=-1, keepdims=True),
                             jnp.max(s_ctx, axis=-1, keepdims=True))
            e_loc = jnp.exp2(s_loc - mx)
            e_ctx = jnp.exp2(s_ctx - mx)
            den = jnp.sum(e_loc, axis=-1, keepdims=True) + jnp.sum(e_ctx, axis=-1, keepdims=True)
            oh = (_dot(e_loc.astype(BF16), jnp.where(keep, vb, zero))
                  + _dot(e_ctx.astype(BF16), jnp.where(keep, vc, zero)))
            o = o + oh / den
        o_ref[pl.ds(qoff, NA_Q), :] = o.astype(BF16)
        return carry

    lax.fori_loop(0, NA_BLOCKS, block, 0, unroll=2)


def _neighbourhood_attention(q, kt, v, kct, vc, bias_pairs, rowmask, delta0):
    lane = 2 * HEAD_DIM
    return pl.pallas_call(
        functools.partial(_natten_body, delta0=delta0),
        grid=(N_LAT, N_PAIRS),
        in_specs=[
            pl.BlockSpec((LAT, lane), lambda b, p: (b, p)),
            pl.BlockSpec((None, lane, LAT), lambda b, p: (b, p, 0)),
            pl.BlockSpec((LAT, lane), lambda b, p: (b, p)),
            pl.BlockSpec((None, lane, PAST), lambda b, p: (b, p, 0)),
            pl.BlockSpec((None, PAST, lane), lambda b, p: (b, 0, p)),
            pl.BlockSpec((2, NA_DT, GRID, lane), lambda b, p: (p, 0, 0, 0)),
            pl.BlockSpec((NA_MASK_ROWS, NA_K), lambda b, p: (0, 0)),
        ],
        out_specs=pl.BlockSpec((LAT, lane), lambda b, p: (b, p)),
        out_shape=jax.ShapeDtypeStruct((N_LAT * LAT, D), BF16),
        scratch_shapes=[pltpu.VMEM((2, NA_VARIANTS, NA_Q, NA_K), F32)],
        compiler_params=_cparams(2),
        name="neighbourhood_attention",
    )(q, kt, v, kct, vc, bias_pairs, rowmask)


def _attn_out_body(o_ref, x_ref, mod_ref, gpost_ref, wout_ref, out_ref):
    y = _dot(o_ref[...], wout_ref[...])
    out_ref[...] = x_ref[...] + mod_ref[5:6, :] * _postnorm(y, gpost_ref[...])


def _attention_out_latent(o, x, mods, post, w_out, layer):
    per_lat = LAT // OUT_TM
    first = GROUP_ROWS // OUT_TM
    return pl.pallas_call(
        _attn_out_body,
        grid=(N_LAT * per_lat,),
        in_specs=[
            pl.BlockSpec((OUT_TM, D), lambda i: (i, 0)),
            pl.BlockSpec((OUT_TM, D), lambda i: (first + i, 0)),
            _mod_spec(layer, lambda i: 1 + i // per_lat),
            _const_spec((1, D), layer, 1),
            _const_spec((D, D), layer // 2),
        ],
        out_specs=pl.BlockSpec((OUT_TM, D), lambda i: (first + i, 0)),
        out_shape=jax.ShapeDtypeStruct((T_ALL, D), F32),
        input_output_aliases={1: 0},
        compiler_params=_cparams(1),
        name="attention_out_latent",
    )(o, x, mods, post, w_out)


def kernel(x_prompt, x_sample, cache_k, cache_v, c, c_ctx, w_mod, b_mod, norm_pre, norm_post,
           ffn_w1, ffn_w2, four_w_in, four_w_out, na_w_qkv, na_w_out, na_rpb):
    conds = jnp.concatenate([c_ctx[None, :], c, jnp.zeros((8 - 1 - N_LAT, D), F32)], axis=0)
    mods = _modulation(conds, w_mod, b_mod)[:, :N_GROUPS].reshape(DEPTH, N_GROUPS, N_MOD, D)

    chan, dft_prompt, m1, m2, tw = _fourier_constants()
    ok, rowmask, delta0 = _na_tables()
    rowmask = jnp.asarray(rowmask)

    w1 = ffn_w1.astype(BF16)
    w2 = ffn_w2.astype(BF16)
    w_in = four_w_in.astype(BF16)
    w_qkv = na_w_qkv.astype(BF16)
    w_out = na_w_out.astype(BF16)
    wkt = w_qkv[:, :, D:2 * D].transpose(0, 2, 1)
    pre = norm_pre[:, :, None, :]
    post = norm_post[:, :, None, :]

    n_ctx = BATCH * SEQ
    n_lat = N_LAT * LAT
    ffn = functools.partial(_half_ffn, mods=mods, pre=pre, post=post, w1=w1, w2=w2)
    caches = ()
    for i in range(DEPTH):
        j = i // 2
        if i == 0:
            x = ffn(x_prompt.reshape(n_ctx, D), layer=0, slot=0, n_rows=n_ctx)
            x = ffn(x_sample.reshape(n_lat, D), layer=0, slot=0, first_row=n_ctx, n_rows=n_lat,
                    src_row=0, carry=x)
        else:
            x = ffn(x, layer=i, slot=0)
        if i % 2 == 0:
            wf = _fold_channel_dft(chan, four_w_out[j])
            x = _fourier_prompt(x, mods, pre, post, w_in, dft_prompt, wf, i)
            x4 = x.reshape(N_GROUPS, GRID, GRID, D)
            y5 = _fourier_latent_stage1(x4, mods, pre, w_in, m1, tw, i)
            x4 = _fourier_latent_stage2(y5, x4, mods, post, m2, wf, i)
            x = x4.reshape(T_ALL, D)
        else:
            x, *caches = _context_attention(x, mods, pre, post, w_qkv, w_out, i, caches)
            q, kt, v = _qkv_latent(x, mods, pre, w_qkv, wkt, i)
            kct = cache_k[:, j].transpose(0, 1, 3, 2).reshape(N_LAT, D, PAST).astype(BF16)
            vct = cache_v[:, j].transpose(0, 2, 1, 3).reshape(N_LAT, PAST, D).astype(BF16)
            bias_pairs = _na_bias_pairs(na_rpb[j], ok)
            o = _neighbourhood_attention(q, kt, v, kct, vct, bias_pairs, rowmask, delta0)
            x = _attention_out_latent(o, x, mods, post, w_out, i)
        if i < DEPTH - 1:
            x = ffn(x, layer=i, slot=1)

    last = DEPTH - 1
    y_prompt = ffn(x, layer=last, slot=1, n_rows=n_ctx, dst_rows=n_ctx)
    y_sample = ffn(x, layer=last, slot=1, first_row=n_ctx, n_rows=n_lat, dst_rows=n_lat, dst_row=0)
    return (y_prompt.reshape(BATCH, SEQ, D), y_sample.reshape(N_LAT, LAT, D), caches[0], caches[1])
```

```python
import functools

import numpy as np
import jax
import jax.numpy as jnp
from jax import lax
from jax.experimental import pallas as pl
from jax.experimental.pallas import tpu as pltpu

F32 = jnp.float32
BF16 = jnp.bfloat16

D = 1024
D_FF = 2816
DEPTH = 4
N_MOD = 9
N_HEADS = 16
HEAD_DIM = 64
N_PAIRS = N_HEADS // 2
SEQ = 256
BATCH = 16
GRID = 64
LAT = GRID * GRID
N_LAT = 2
GROUP_ROWS = 4096
N_GROUPS = 3
T_ALL = N_GROUPS * GROUP_ROWS
PAST = 256
WIN_ROWS = 8
WIN_COLS = 16
N_FG = 8
FG_DIM = D // N_FG
EPS = 1e-6
NEG = -1e30
LOG2E = 1.4426950408889634
Q_SCALE = HEAD_DIM ** -0.5 * LOG2E

VMEM_LIMIT = 56 * 1024 * 1024

FFN_TM = 1024
FFN_SUB = 256
FFN_NC = 1
MOD_TN = 2304
QKV_TM = 512
CTX_NB = 2
OUT_TM = 512
FS_TN = 16
FS_SUB = 8
NA_ROWS = 4
NA_Q = NA_ROWS * GRID
NA_KROWS = 12
NA_K = NA_KROWS * GRID
NA_BLOCKS = GRID // NA_ROWS
NA_VARIANTS = 3
NA_MASK_ROWS = 16
NA_DT = 22


def _cparams(n_axes):
    return pltpu.CompilerParams(
        dimension_semantics=("arbitrary",) * n_axes, vmem_limit_bytes=VMEM_LIMIT)


def _const_spec(shape, *lead):
    nd = len(shape)
    return pl.BlockSpec((None,) * len(lead) + tuple(shape), lambda *_: tuple(lead) + (0,) * nd,
                        pipeline_mode=pl.Buffered(1))


def _mod_spec(layer, group_of):
    return pl.BlockSpec((None, None, N_MOD, D), lambda *ids: (layer, group_of(*ids), 0, 0))


def _silu(x):
    return x * jax.nn.sigmoid(x)


def _prenorm(x, g, mod_ref, k):
    shift = mod_ref[3 * k:3 * k + 1, :]
    scale = mod_ref[3 * k + 1:3 * k + 2, :]
    y = x * lax.rsqrt(jnp.mean(x * x, axis=-1, keepdims=True) + EPS)
    return ((y * g) * (1.0 + scale) + shift).astype(BF16)


def _postnorm(y, g):
    return (y * lax.rsqrt(jnp.mean(y * y, axis=-1, keepdims=True) + EPS)) * g


def _dot(a, b):
    return jnp.dot(a, b, preferred_element_type=F32)


def _dot_nt(a, b):
    return lax.dot_general(a, b, (((1,), (1,)), ((), ())), preferred_element_type=F32)


def _mod_body(cond_ref, w_ref, b_ref, o_ref):
    s = _silu(cond_ref[...]).astype(BF16)
    o_ref[...] = _dot(s, w_ref[...].astype(BF16)) + b_ref[...]


def _modulation(conds, w_mod, b_mod):
    n = N_MOD * D
    return pl.pallas_call(
        _mod_body,
        grid=(DEPTH, n // MOD_TN),
        in_specs=[
            pl.BlockSpec((8, D), lambda l, j: (0, 0)),
            pl.BlockSpec((None, D, MOD_TN), lambda l, j: (l, 0, j)),
            pl.BlockSpec((None, 1, MOD_TN), lambda l, j: (l, 0, j)),
        ],
        out_specs=pl.BlockSpec((None, 8, MOD_TN), lambda l, j: (l, 0, j)),
        out_shape=jax.ShapeDtypeStruct((DEPTH, 8, n), F32),
        compiler_params=_cparams(2),
        name="modulation",
    )(conds, w_mod, b_mod.reshape(DEPTH, 1, n))


def _ffn_body(x_ref, mod_ref, gpre_ref, gpost_ref, w1_ref, w2_ref, o_ref, *, k):
    ck = D_FF // FFN_NC
    gate = mod_ref[3 * k + 2:3 * k + 3, :]
    for r in range(FFN_TM // FFN_SUB):
        rows = slice(r * FFN_SUB, (r + 1) * FFN_SUB)
        x = x_ref[rows, :]
        h = _prenorm(x, gpre_ref[...], mod_ref, k)
        y = None
        for c in range(FFN_NC):
            g = _dot(h, w1_ref[:, c * ck:(c + 1) * ck])
            u = _dot(h, w1_ref[:, D_FF + c * ck:D_FF + (c + 1) * ck])
            a = (_silu(g) * u).astype(BF16)
            yc = _dot(a, w2_ref[c * ck:(c + 1) * ck, :])
            y = yc if y is None else y + yc
        o_ref[rows, :] = x + (0.5 * gate) * _postnorm(y, gpost_ref[...])


def _ffn_body_carry(x_ref, mod_ref, gpre_ref, gpost_ref, w1_ref, w2_ref, carry_ref, o_ref, *, k):
    del carry_ref
    _ffn_body(x_ref, mod_ref, gpre_ref, gpost_ref, w1_ref, w2_ref, o_ref, k=k)


def _half_ffn(src, mods, pre, post, w1, w2, layer, slot, *, first_row=0, n_rows=T_ALL,
              src_row=None, dst_rows=T_ALL, dst_row=None, carry=None):
    k = 2 * slot
    tiles_per_group = GROUP_ROWS // FFN_TM
    t0 = first_row // FFN_TM
    s0 = t0 if src_row is None else src_row // FFN_TM
    d0 = t0 if dst_row is None else dst_row // FFN_TM
    in_specs = [
        pl.BlockSpec((FFN_TM, D), lambda i: (s0 + i, 0)),
        _mod_spec(layer, lambda i: (t0 + i) // tiles_per_group),
        _const_spec((1, D), layer, k),
        _const_spec((1, D), layer, k),
        _const_spec((D, 2 * D_FF), layer, slot),
        _const_spec((D_FF, D), layer, slot),
    ]
    args = [src, mods, pre, post, w1, w2]
    body, aliases = _ffn_body, {}
    if carry is not None:
        in_specs.append(pl.BlockSpec(memory_space=pl.ANY))
        args.append(carry)
        body, aliases = _ffn_body_carry, {6: 0}
    return pl.pallas_call(
        functools.partial(body, k=k),
        grid=(n_rows // FFN_TM,),
        in_specs=in_specs,
        out_specs=pl.BlockSpec((FFN_TM, D), lambda i: (d0 + i, 0)),
        out_shape=jax.ShapeDtypeStruct((dst_rows, D), F32),
        input_output_aliases=aliases,
        compiler_params=_cparams(1),
        name="half_ffn",
    )(*args)


def _dft_cos_sin(n):
    idx = np.arange(n)
    ang = 2.0 * np.pi * ((idx[:, None] * idx[None, :]) % n) / n
    return np.cos(ang), np.sin(ang)


def _fourier_constants():
    c128, s128 = _dft_cos_sin(FG_DIM)
    chan = np.stack([c128, s128]) / np.sqrt(FG_DIM)

    c256, s256 = _dft_cos_sin(SEQ)
    dft_prompt = np.concatenate([c256, -s256], axis=0) / np.sqrt(SEQ)

    c64, s64 = _dft_cos_sin(GRID)
    c64 = c64 / np.sqrt(GRID)
    s64 = s64 / np.sqrt(GRID)
    eye = np.eye(FS_SUB)
    rows = GRID * FS_SUB
    m1 = np.concatenate([np.kron(c64, eye), np.kron(-s64, eye)], axis=0)
    def spread(f):
        return np.einsum("kn,jl->kjln", f, eye).reshape(rows, rows)
    m2 = np.block([[spread(c64), spread(s64)], [spread(-s64), spread(c64)]])
    k1 = np.arange(GRID)[:, None]
    n2 = np.arange(GRID)[None, :]
    ang = 2.0 * np.pi * (k1 * n2) / LAT
    tw = np.stack([np.cos(ang), np.sin(ang)])[..., None] * np.ones((1, 1, 1, 128))

    def mxu_const(a):
        return jnp.asarray(a, F32).astype(BF16)

    return (jnp.asarray(chan, F32), mxu_const(dft_prompt), mxu_const(m1), mxu_const(m2),
            jnp.asarray(tw, F32))


def _fold_body(cs_ref, w_ref, o_ref):
    w = w_ref[...]
    for t in range(2):
        o_ref[t] = jnp.dot(cs_ref[t], w, preferred_element_type=F32,
                           precision=lax.Precision.HIGHEST).astype(BF16)


def _fold_channel_dft(chan, w_out):
    return pl.pallas_call(
        _fold_body,
        grid=(N_FG,),
        in_specs=[
            pl.BlockSpec((2, FG_DIM, FG_DIM), lambda g: (0, 0, 0)),
            pl.BlockSpec((FG_DIM, D), lambda g: (g, 0)),
        ],
        out_specs=pl.BlockSpec((2, FG_DIM, D), lambda g: (0, g, 0)),
        out_shape=jax.ShapeDtypeStruct((2, D, D), BF16),
        compiler_params=_cparams(1),
        name="fold_channel_dft",
    )(chan, w_out)


def _four_prompt_body(x_ref, mod_ref, gpre_ref, gpost_ref, win_ref, dft_ref, w2_ref, o_ref):
    x = x_ref[...]
    h = _prenorm(x, gpre_ref[...], mod_ref, 1)
    u = _dot(h, win_ref[...]).astype(BF16)
    p = _dot(dft_ref[...], u)
    y = _dot(p[:SEQ].astype(BF16), w2_ref[0]) + _dot(p[SEQ:].astype(BF16), w2_ref[1])
    o_ref[...] = x + mod_ref[5:6, :] * _postnorm(y, gpost_ref[...])


def _fourier_prompt(x, mods, pre, post, w_in, dft_prompt, w2, layer):
    return pl.pallas_call(
        _four_prompt_body,
        grid=(BATCH,),
        in_specs=[
            pl.BlockSpec((SEQ, D), lambda b: (b, 0)),
            _mod_spec(layer, lambda b: 0),
            _const_spec((1, D), layer, 1),
            _const_spec((1, D), layer, 1),
            _const_spec((D, D), layer // 2),
            _const_spec((2 * SEQ, SEQ)),
            _const_spec((2, D, D)),
        ],
        out_specs=pl.BlockSpec((SEQ, D), lambda b: (b, 0)),
        out_shape=jax.ShapeDtypeStruct((T_ALL, D), F32),
        input_output_aliases={0: 0},
        compiler_params=_cparams(1),
        name="fourier_prompt",
    )(x, mods, pre, post, w_in, dft_prompt, w2)


def _four_s1_body(x_ref, mod_ref, gpre_ref, win_ref, m1_ref, tw_ref, y_ref, y_scr):
    rows = GRID * FS_SUB
    for s in range(FS_TN // FS_SUB):
        cols = slice(s * FS_SUB, (s + 1) * FS_SUB)
        x = x_ref[:, cols, :].reshape(rows, D)
        h = _prenorm(x, gpre_ref[...], mod_ref, 1)
        u = _dot(h, win_ref[...]).astype(BF16)
        y = _dot(m1_ref[...], u)
        tc = tw_ref[0, :, cols, :].reshape(rows, 128)
        ts = tw_ref[1, :, cols, :].reshape(rows, 128)
        for l in range(D // 128):
            sl = slice(l * 128, (l + 1) * 128)
            yr = y[:rows, sl]
            yi = y[rows:, sl]
            y_scr[0, :, cols, sl] = (yr * tc + yi * ts).reshape(GRID, FS_SUB, 128)
            y_scr[1, :, cols, sl] = (yi * tc - yr * ts).reshape(GRID, FS_SUB, 128)
    y_ref[...] = y_scr[...].astype(BF16)


def _fourier_latent_stage1(x4, mods, pre, w_in, m1, tw, layer):
    rows = GRID * FS_SUB
    return pl.pallas_call(
        _four_s1_body,
        grid=(N_LAT, GRID // FS_TN),
        in_specs=[
            pl.BlockSpec((None, GRID, FS_TN, D), lambda b, j: (b + 1, 0, j, 0)),
            _mod_spec(layer, lambda b, j: b + 1),
            _const_spec((1, D), layer, 1),
            _const_spec((D, D), layer // 2),
            _const_spec((2 * rows, rows)),
            pl.BlockSpec((2, GRID, FS_TN, 128), lambda b, j: (0, 0, j, 0)),
        ],
        out_specs=pl.BlockSpec((None, 2, GRID, FS_TN, D), lambda b, j: (b, 0, 0, j, 0)),
        out_shape=jax.ShapeDtypeStruct((N_LAT, 2, GRID, GRID, D), BF16),
        scratch_shapes=[pltpu.VMEM((2, GRID, FS_TN, D), F32)],
        compiler_params=_cparams(2),
        name="fourier_latent_stage1",
    )(x4, mods, pre, w_in, m1, tw)


def _four_s2_body(y_ref, x_ref, mod_ref, gpost_ref, m2_ref, w2_ref, o_ref):
    rows = GRID * FS_SUB
    yin = y_ref[...].reshape(2 * rows, D)
    z = _dot(m2_ref[...], yin)
    y = _dot(z[:rows].astype(BF16), w2_ref[0]) + _dot(z[rows:].astype(BF16), w2_ref[1])
    x = x_ref[...].reshape(rows, D)
    out = x + mod_ref[5:6, :] * _postnorm(y, gpost_ref[...])
    o_ref[...] = out.reshape(GRID, FS_SUB, D)


def _fourier_latent_stage2(y5, x4, mods, post, m2, w2, layer):
    rows = GRID * FS_SUB
    return pl.pallas_call(
        _four_s2_body,
        grid=(N_LAT, GRID // FS_SUB),
        in_specs=[
            pl.BlockSpec((None, 2, FS_SUB, GRID, D), lambda b, j: (b, 0, j, 0, 0)),
            pl.BlockSpec((None, GRID, FS_SUB, D), lambda b, j: (b + 1, 0, j, 0)),
            _mod_spec(layer, lambda b, j: b + 1),
            _const_spec((1, D), layer, 1),
            _const_spec((2 * rows, 2 * rows)),
            _const_spec((2, D, D)),
        ],
        out_specs=pl.BlockSpec((None, GRID, FS_SUB, D), lambda b, j: (b + 1, 0, j, 0)),
        out_shape=jax.ShapeDtypeStruct((N_GROUPS, GRID, GRID, D), F32),
        input_output_aliases={1: 0},
        compiler_params=_cparams(2),
        name="fourier_latent_stage2",
    )(y5, x4, mods, post, m2, w2)


def _ctx_attn_body(x_ref, mod_ref, gpre_ref, gpost_ref, wqkv_ref, wout_ref, *rest):
    o_ref, kc_ref, vc_ref = rest[-3:]
    x = x_ref[...]
    h = _prenorm(x, gpre_ref[...], mod_ref, 1)
    qkv = _dot(h, wqkv_ref[...])
    k = qkv[:, D:2 * D]
    v = qkv[:, 2 * D:]
    for nb in range(CTX_NB):
        rows = slice(nb * SEQ, (nb + 1) * SEQ)
        for hd in range(N_HEADS):
            sl = slice(hd * HEAD_DIM, (hd + 1) * HEAD_DIM)
            kc_ref[nb, hd] = k[rows, sl]
            vc_ref[nb, hd] = v[rows, sl]
    qb = (qkv[:, :D] * Q_SCALE).astype(BF16)
    kb = k.astype(BF16)
    vb = v.astype(BF16)
    first = lax.broadcasted_iota(jnp.int32, (1, 2 * HEAD_DIM), 1) < HEAD_DIM
    zero = jnp.zeros((), BF16)
    seqs = []
    for nb in range(CTX_NB):
        rows = slice(nb * SEQ, (nb + 1) * SEQ)
        outs = []
        for p in range(N_PAIRS):
            sl = slice(p * 2 * HEAD_DIM, (p + 1) * 2 * HEAD_DIM)
            kp = kb[rows, sl]
            vp = vb[rows, sl]
            k2 = jnp.concatenate([jnp.where(first, kp, zero), jnp.where(first, zero, kp)], axis=0)
            v2 = jnp.concatenate([jnp.where(first, vp, zero), jnp.where(first, zero, vp)], axis=0)
            s = _dot_nt(qb[rows, sl], k2)
            probs = []
            for hh in range(2):
                sh = s[:, hh * SEQ:(hh + 1) * SEQ]
                e = jnp.exp2(sh - jnp.max(sh, axis=-1, keepdims=True))
                probs.append(e / jnp.sum(e, axis=-1, keepdims=True))
            outs.append(_dot(jnp.concatenate(probs, axis=1).astype(BF16), v2))
        seqs.append(jnp.concatenate(outs, axis=1).astype(BF16))
    y = _dot(jnp.concatenate(seqs, axis=0), wout_ref[...])
    o_ref[...] = x + mod_ref[5:6, :] * _postnorm(y, gpost_ref[...])


def _context_attention(x, mods, pre, post, w_qkv, w_out, layer, caches):
    j = layer // 2
    rows = CTX_NB * SEQ
    cache = jax.ShapeDtypeStruct((BATCH, DEPTH // 2, N_HEADS, SEQ, HEAD_DIM), F32)
    cache_spec = pl.BlockSpec((CTX_NB, None, N_HEADS, SEQ, HEAD_DIM), lambda b: (b, j, 0, 0, 0))
    in_specs = [
        pl.BlockSpec((rows, D), lambda b: (b, 0)),
        _mod_spec(layer, lambda b: 0),
        _const_spec((1, D), layer, 1),
        _const_spec((1, D), layer, 1),
        _const_spec((D, 3 * D), j),
        _const_spec((D, D), j),
    ]
    aliases = {0: 0}
    if caches:
        in_specs += [pl.BlockSpec(memory_space=pl.ANY)] * 2
        aliases.update({6: 1, 7: 2})
    return pl.pallas_call(
        _ctx_attn_body,
        grid=(BATCH // CTX_NB,),
        in_specs=in_specs,
        out_specs=[pl.BlockSpec((rows, D), lambda b: (b, 0)), cache_spec, cache_spec],
        out_shape=[jax.ShapeDtypeStruct((T_ALL, D), F32), cache, cache],
        input_output_aliases=aliases,
        compiler_params=_cparams(1),
        name="context_attention",
    )(x, mods, pre, post, w_qkv, w_out, *caches)


def _qkv_lat_body(x_ref, mod_ref, gpre_ref, wq_ref, wkt_ref, wv_ref, q_ref, kt_ref, v_ref):
    h = _prenorm(x_ref[...], gpre_ref[...], mod_ref, 1)
    q_ref[...] = (_dot(h, wq_ref[...]) * Q_SCALE).astype(BF16)
    v_ref[...] = _dot(h, wv_ref[...]).astype(BF16)
    kt_ref[...] = _dot_nt(wkt_ref[...], h).astype(BF16)


def _qkv_latent(x, mods, pre, w_qkv, wkt, layer):
    j = layer // 2
    per_lat = LAT // QKV_TM
    first = GROUP_ROWS // QKV_TM
    tok = jax.ShapeDtypeStruct((N_LAT * LAT, D), BF16)

    def w_cols(col_block):
        return pl.BlockSpec((None, D, D), lambda i: (j, 0, col_block), pipeline_mode=pl.Buffered(1))

    return pl.pallas_call(
        _qkv_lat_body,
        grid=(N_LAT * per_lat,),
        in_specs=[
            pl.BlockSpec((QKV_TM, D), lambda i: (first + i, 0)),
            _mod_spec(layer, lambda i: 1 + i // per_lat),
            _const_spec((1, D), layer, 1),
            w_cols(0),
            _const_spec((D, D), j),
            w_cols(2),
        ],
        out_specs=[
            pl.BlockSpec((QKV_TM, D), lambda i: (i, 0)),
            pl.BlockSpec((None, D, QKV_TM), lambda i: (i // per_lat, 0, i % per_lat)),
            pl.BlockSpec((QKV_TM, D), lambda i: (i, 0)),
        ],
        out_shape=[tok, jax.ShapeDtypeStruct((N_LAT, D, LAT), BF16), tok],
        compiler_params=_cparams(1),
        name="qkv_latent",
    )(x, mods, pre, w_qkv, wkt, w_qkv)


def _na_block_geometry(blk):
    return min(max(NA_ROWS * blk - WIN_ROWS // 2, 0), GRID - NA_KROWS)


def _na_variant(blk):
    return 0 if blk == 0 else (2 if blk == NA_BLOCKS - 1 else 1)


def _na_tables():
    t = np.arange(NA_DT)[:, None]
    e = np.arange(2)[None, :]
    drow = t - 11 + e
    row_ok = (drow >= -(WIN_ROWS - 1)) & (drow <= WIN_ROWS - 1)
    c = np.arange(GRID)[:, None]
    cp = np.arange(GRID)[None, :]
    cs = np.clip(c - WIN_COLS // 2, 0, GRID - WIN_COLS)
    col_ok = (cp >= cs) & (cp < cs + WIN_COLS)
    ok = row_ok[:, None, :, None] & col_ok[None, :, None, :]
    rowmask = np.zeros((NA_MASK_ROWS, NA_K), np.float32)
    delta0 = np.zeros((NA_VARIANTS, NA_ROWS, NA_KROWS // 2), np.int64)
    seen = set()
    for blk in range(NA_BLOCKS):
        v = _na_variant(blk)
        u0 = _na_block_geometry(blk)
        for rl in range(NA_ROWS):
            r = NA_ROWS * blk + rl
            rs = min(max(r - WIN_ROWS // 2, 0), GRID - WIN_ROWS)
            kr = u0 + np.arange(NA_KROWS)
            valid = (kr >= rs) & (kr < rs + WIN_ROWS)
            mask = np.repeat(np.where(valid, 0.0, NEG), GRID).astype(np.float32)
            d0 = u0 - r + 2 * np.arange(NA_KROWS // 2) + 11
            if (v, rl) in seen:
                assert (rowmask[v * NA_ROWS + rl] == mask).all() and (delta0[v, rl] == d0).all()
            seen.add((v, rl))
            rowmask[v * NA_ROWS + rl] = mask
            delta0[v, rl] = d0
    assert delta0.min() >= 0 and delta0.max() < NA_DT
    return ok, rowmask, delta0


def _na_bias_pairs(rpb, ok):
    nrow = 2 * WIN_ROWS - 1
    period = 2 * GRID
    fill = jnp.full((N_HEADS, nrow, period - (2 * WIN_COLS - 1)), NEG, F32)
    p = jnp.concatenate([rpb[:, :, WIN_COLS - 1:], fill, rpb[:, :, :WIN_COLS - 1]], axis=-1)
    toep = jnp.tile(p, (1, 1, GRID))[..., :GRID * (period - 1)]
    toep = toep.reshape(N_HEADS, nrow, GRID, period - 1)[..., :GRID]
    masked = jnp.full((N_HEADS, GRID, GRID), NEG, F32)

    def rows(drow):
        idx = drow + WIN_ROWS - 1
        return toep[:, idx] if 0 <= idx < nrow else masked

    tab = jnp.stack([jnp.concatenate([rows(t - 11), rows(t - 10)], axis=-1)
                     for t in range(NA_DT)], axis=1)
    return jnp.where(ok.reshape(1, NA_DT, GRID, 2 * GRID), tab * LOG2E, NEG)


def _natten_body(q_ref, kt_ref, v_ref, kct_ref, vc_ref, bp_ref, rm_ref, o_ref,
                 bias_scr, s_scr, p_scr, rden_scr, *, delta0):
    for hh in range(2):
        for var in range(NA_VARIANTS):
            for rl in range(NA_ROWS):
                strip = jnp.concatenate(
                    [bp_ref[hh, int(delta0[var, rl, i])] for i in range(NA_KROWS // 2)], axis=1)
                row = var * NA_ROWS + rl
                bias_scr[hh, var, rl * GRID:(rl + 1) * GRID, :] = strip + rm_ref[row:row + 1, :]

    first = lax.broadcasted_iota(jnp.int32, (1, 2 * HEAD_DIM), 1) < HEAD_DIM
    zero = jnp.zeros((), BF16)
    kct = kct_ref[...]
    vc = vc_ref[...]

    keeps = (first, jnp.logical_not(first))

    def offsets(blk):
        koff = pl.multiple_of(jnp.clip(blk - 1, 0, NA_BLOCKS - 3) * NA_Q, NA_Q)
        qoff = pl.multiple_of(blk * NA_Q, NA_Q)
        return koff, qoff

    def logits(blk, slot):
        koff, qoff = offsets(blk)
        var = jnp.where(blk == 0, 0, jnp.where(blk == NA_BLOCKS - 1, 2, 1))
        qb = q_ref[pl.ds(qoff, NA_Q), :]
        ktb = kt_ref[:, pl.ds(koff, NA_K)]
        for hh in range(2):
            qh = jnp.where(keeps[hh], qb, zero)
            s_scr[slot, hh, :, :NA_K] = _dot(qh, ktb) + bias_scr[hh, var]
            s_scr[slot, hh, :, NA_K:] = _dot(qh, kct)

    def softmax(slot):
        for hh in range(2):
            s = s_scr[slot, hh]
            e = jnp.exp2(s - jnp.max(s, axis=-1, keepdims=True))
            p_scr[slot, hh] = e.astype(BF16)
            rden = 1.0 / jnp.sum(e, axis=-1, keepdims=True)
            rden_scr[slot, hh] = jnp.broadcast_to(rden, (NA_Q, 2 * HEAD_DIM))

    def values(blk, slot):
        koff, qoff = offsets(blk)
        vb = v_ref[pl.ds(koff, NA_K), :]
        o = jnp.zeros((NA_Q, 2 * HEAD_DIM), F32)
        for hh in range(2):
            oh = (_dot(p_scr[slot, hh, :, :NA_K], jnp.where(keeps[hh], vb, zero))
                  + _dot(p_scr[slot, hh, :, NA_K:], jnp.where(keeps[hh], vc, zero)))
            o = o + oh * rden_scr[slot, hh]
        o_ref[pl.ds(qoff, NA_Q), :] = o.astype(BF16)

    logits(0, 0)
    softmax(0)
    logits(1, 1)

    def steady(it, carry):
        t = 2 + 2 * it
        logits(t, 0)
        values(t - 2, 0)
        softmax(1)
        logits(t + 1, 1)
        values(t - 1, 1)
        softmax(0)
        return carry

    lax.fori_loop(0, (NA_BLOCKS - 2) // 2, steady, 0)
    values(NA_BLOCKS - 2, 0)
    softmax(1)
    values(NA_BLOCKS - 1, 1)


def _neighbourhood_attention(q, kt, v, kct, vc, bias_pairs, rowmask, delta0):
    lane = 2 * HEAD_DIM
    return pl.pallas_call(
        functools.partial(_natten_body, delta0=delta0),
        grid=(N_LAT, N_PAIRS),
        in_specs=[
            pl.BlockSpec((LAT, lane), lambda b, p: (b, p)),
            pl.BlockSpec((None, lane, LAT), lambda b, p: (b, p, 0)),
            pl.BlockSpec((LAT, lane), lambda b, p: (b, p)),
            pl.BlockSpec((None, lane, PAST), lambda b, p: (b, p, 0)),
            pl.BlockSpec((None, PAST, lane), lambda b, p: (b, 0, p)),
            pl.BlockSpec((2, NA_DT, GRID, lane), lambda b, p: (p, 0, 0, 0)),
            pl.BlockSpec((NA_MASK_ROWS, NA_K), lambda b, p: (0, 0)),
        ],
        out_specs=pl.BlockSpec((LAT, lane), lambda b, p: (b, p)),
        out_shape=jax.ShapeDtypeStruct((N_LAT * LAT, D), BF16),
        scratch_shapes=[
            pltpu.VMEM((2, NA_VARIANTS, NA_Q, NA_K), F32),
            pltpu.VMEM((2, 2, NA_Q, NA_K + PAST), F32),
            pltpu.VMEM((2, 2, NA_Q, NA_K + PAST), BF16),
            pltpu.VMEM((2, 2, NA_Q, lane), F32),
        ],
        compiler_params=_cparams(2),
        name="neighbourhood_attention",
    )(q, kt, v, kct, vc, bias_pairs, rowmask)


def _attn_out_body(o_ref, x_ref, mod_ref, gpost_ref, wout_ref, out_ref):
    y = _dot(o_ref[...], wout_ref[...])
    out_ref[...] = x_ref[...] + mod_ref[5:6, :] * _postnorm(y, gpost_ref[...])


def _attention_out_latent(o, x, mods, post, w_out, layer):
    per_lat = LAT // OUT_TM
    first = GROUP_ROWS // OUT_TM
    return pl.pallas_call(
        _attn_out_body,
        grid=(N_LAT * per_lat,),
        in_specs=[
            pl.BlockSpec((OUT_TM, D), lambda i: (i, 0)),
            pl.BlockSpec((OUT_TM, D), lambda i: (first + i, 0)),
            _mod_spec(layer, lambda i: 1 + i // per_lat),
            _const_spec((1, D), layer, 1),
            _const_spec((D, D), layer // 2),
        ],
        out_specs=pl.BlockSpec((OUT_TM, D), lambda i: (first + i, 0)),
        out_shape=jax.ShapeDtypeStruct((T_ALL, D), F32),
        input_output_aliases={1: 0},
        compiler_params=_cparams(1),
        name="attention_out_latent",
    )(o, x, mods, post, w_out)


def kernel(x_prompt, x_sample, cache_k, cache_v, c, c_ctx, w_mod, b_mod, norm_pre, norm_post,
           ffn_w1, ffn_w2, four_w_in, four_w_out, na_w_qkv, na_w_out, na_rpb):
    conds = jnp.concatenate([c_ctx[None, :], c, jnp.zeros((8 - 1 - N_LAT, D), F32)], axis=0)
    mods = _modulation(conds, w_mod, b_mod)[:, :N_GROUPS].reshape(DEPTH, N_GROUPS, N_MOD, D)

    chan, dft_prompt, m1, m2, tw = _fourier_constants()
    ok, rowmask, delta0 = _na_tables()
    rowmask = jnp.asarray(rowmask)

    w1 = ffn_w1.astype(BF16)
    w2 = ffn_w2.astype(BF16)
    w_in = four_w_in.astype(BF16)
    w_qkv = na_w_qkv.astype(BF16)
    w_out = na_w_out.astype(BF16)
    wkt = w_qkv[:, :, D:2 * D].transpose(0, 2, 1)
    pre = norm_pre[:, :, None, :]
    post = norm_post[:, :, None, :]

    n_ctx = BATCH * SEQ
    n_lat = N_LAT * LAT
    ffn = functools.partial(_half_ffn, mods=mods, pre=pre, post=post, w1=w1, w2=w2)
    caches = ()
    for i in range(DEPTH):
        j = i // 2
        if i == 0:
            x = ffn(x_prompt.reshape(n_ctx, D), layer=0, slot=0, n_rows=n_ctx)
            x = ffn(x_sample.reshape(n_lat, D), layer=0, slot=0, first_row=n_ctx, n_rows=n_lat,
                    src_row=0, carry=x)
        else:
            x = ffn(x, layer=i, slot=0)
        if i % 2 == 0:
            wf = _fold_channel_dft(chan, four_w_out[j])
            x = _fourier_prompt(x, mods, pre, post, w_in, dft_prompt, wf, i)
            x4 = x.reshape(N_GROUPS, GRID, GRID, D)
            y5 = _fourier_latent_stage1(x4, mods, pre, w_in, m1, tw, i)
            x4 = _fourier_latent_stage2(y5, x4, mods, post, m2, wf, i)
            x = x4.reshape(T_ALL, D)
        else:
            x, *caches = _context_attention(x, mods, pre, post, w_qkv, w_out, i, caches)
            q, kt, v = _qkv_latent(x, mods, pre, w_qkv, wkt, i)
            kct = cache_k[:, j].transpose(0, 1, 3, 2).reshape(N_LAT, D, PAST).astype(BF16)
            vct = cache_v[:, j].transpose(0, 2, 1, 3).reshape(N_LAT, PAST, D).astype(BF16)
            bias_pairs = _na_bias_pairs(na_rpb[j], ok)
            o = _neighbourhood_attention(q, kt, v, kct, vct, bias_pairs, rowmask, delta0)
            x = _attention_out_latent(o, x, mods, post, w_out, i)
        if i < DEPTH - 1:
            x = ffn(x, layer=i, slot=1)

    last = DEPTH - 1
    y_prompt = ffn(x, layer=last, slot=1, n_rows=n_ctx, dst_rows=n_ctx)
    y_sample = ffn(x, layer=last, slot=1, first_row=n_ctx, n_rows=n_lat, dst_rows=n_lat, dst_row=0)
    return (y_prompt.reshape(BATCH, SEQ, D), y_sample.reshape(N_LAT, LAT, D), caches[0], caches[1])
```

```python
import functools

import numpy as np
import jax
import jax.numpy as jnp
from jax import lax
from jax.experimental import pallas as pl
from jax.experimental.pallas import tpu as pltpu

F32 = jnp.float32
BF16 = jnp.bfloat16

D = 1024
D_FF = 2816
DEPTH = 4
N_MOD = 9
N_HEADS = 16
HEAD_DIM = 64
N_PAIRS = N_HEADS // 2
SEQ = 256
BATCH = 16
GRID = 64
LAT = GRID * GRID
N_LAT = 2
GROUP_ROWS = 4096
N_GROUPS = 3
T_ALL = N_GROUPS * GROUP_ROWS
PAST = 256
WIN_ROWS = 8
WIN_COLS = 16
N_FG = 8
FG_DIM = D // N_FG
EPS = 1e-6
NEG = -1e30
LOG2E = 1.4426950408889634
Q_SCALE = HEAD_DIM ** -0.5 * LOG2E

VMEM_LIMIT = 56 * 1024 * 1024

FFN_TM = 512
FFN_SUB = 256
FFN_NC = 1
FFN_CONV_STEPS = 16
MOD_TN = 2304
QKV_TM = 512
CTX_NB = 1
FS_TN = 16
FS_SUB = 8
NA_ROWS = 4
NA_Q = NA_ROWS * GRID
NA_KROWS = 12
NA_K = NA_KROWS * GRID
NA_BLOCKS = GRID // NA_ROWS
NA_VARIANTS = 3
NA_MASK_ROWS = 16
NA_DT = 22


def _cparams(n_axes):
    return pltpu.CompilerParams(
        dimension_semantics=("arbitrary",) * n_axes, vmem_limit_bytes=VMEM_LIMIT)


def _const_spec(shape, *lead):
    nd = len(shape)
    return pl.BlockSpec((None,) * len(lead) + tuple(shape), lambda *_: tuple(lead) + (0,) * nd,
                        pipeline_mode=pl.Buffered(1))


def _mod_spec(layer, group_of):
    return pl.BlockSpec((None, None, N_MOD, D), lambda *ids: (layer, group_of(*ids), 0, 0))


def _silu(x):
    return x * jax.nn.sigmoid(x)


def _prenorm(x, g, mod_ref, k):
    shift = mod_ref[3 * k:3 * k + 1, :]
    scale = mod_ref[3 * k + 1:3 * k + 2, :]
    y = x * lax.rsqrt(jnp.mean(x * x, axis=-1, keepdims=True) + EPS)
    return ((y * g) * (1.0 + scale) + shift).astype(BF16)


def _postnorm(y, g):
    return (y * lax.rsqrt(jnp.mean(y * y, axis=-1, keepdims=True) + EPS)) * g


def _dot(a, b):
    return jnp.dot(a, b, preferred_element_type=F32)


def _dot_nt(a, b):
    return lax.dot_general(a, b, (((1,), (1,)), ((), ())), preferred_element_type=F32)


def _mod_body(cond_ref, w_ref, b_ref, o_ref):
    s = _silu(cond_ref[...]).astype(BF16)
    o_ref[...] = _dot(s, w_ref[...].astype(BF16)) + b_ref[...]


def _modulation(conds, w_mod, b_mod):
    n = N_MOD * D
    return pl.pallas_call(
        _mod_body,
        grid=(DEPTH, n // MOD_TN),
        in_specs=[
            pl.BlockSpec((8, D), lambda l, j: (0, 0)),
            pl.BlockSpec((None, D, MOD_TN), lambda l, j: (l, 0, j)),
            pl.BlockSpec((None, 1, MOD_TN), lambda l, j: (l, 0, j)),
        ],
        out_specs=pl.BlockSpec((None, 8, MOD_TN), lambda l, j: (l, 0, j)),
        out_shape=jax.ShapeDtypeStruct((DEPTH, 8, n), F32),
        compiler_params=_cparams(2),
        name="modulation",
    )(conds, w_mod, b_mod.reshape(DEPTH, 1, n))


def _ffn_tile(load_x, mod_ref, gpre_ref, gpost_ref, w1_ref, w2_ref, o_ref, k):
    ck = D_FF // FFN_NC
    gate = mod_ref[3 * k + 2:3 * k + 3, :]
    for r in range(FFN_TM // FFN_SUB):
        rows = slice(r * FFN_SUB, (r + 1) * FFN_SUB)
        x = load_x(rows)
        h = _prenorm(x, gpre_ref[...], mod_ref, k)
        y = None
        for c in range(FFN_NC):
            g = _dot(h, w1_ref[:, c * ck:(c + 1) * ck])
            u = _dot(h, w1_ref[:, D_FF + c * ck:D_FF + (c + 1) * ck])
            a = (_silu(g) * u).astype(BF16)
            yc = _dot(a, w2_ref[c * ck:(c + 1) * ck, :])
            y = yc if y is None else y + yc
        o_ref[rows, :] = x + (0.5 * gate) * _postnorm(y, gpost_ref[...])


def _ffn_body(*refs, k, tile0, tiles_a, attn, convert):
    refs = list(refs)
    x_ref = refs.pop(0)
    xb_ref = refs.pop(0) if tiles_a is not None else None
    ao_ref = refs.pop(0) if attn else None
    mod_ref, gpre_ref, gpost_ref, w1_ref, w2_ref = refs[:5]
    refs = refs[5:]
    if attn:
        gmix_ref, wout_ref = refs[:2]
        refs = refs[2:]
    if convert:
        nw1_ref, nw2_ref, o_ref, nw1b_ref, nw2b_ref = refs
        nw1b_ref[...] = nw1_ref[...].astype(BF16)
        nw2b_ref[...] = nw2_ref[...].astype(BF16)
    else:
        (o_ref,) = refs
    tile = tile0 + pl.program_id(0)
    tail = (mod_ref, gpre_ref, gpost_ref, w1_ref, w2_ref, o_ref, k)

    def plain(rows):
        if xb_ref is None:
            return x_ref[rows, :]
        return jnp.where(tile < tiles_a, x_ref[rows, :], xb_ref[rows, :])

    def mixed(rows):
        y = _dot(ao_ref[rows, :], wout_ref[...])
        return x_ref[rows, :] + mod_ref[5:6, :] * _postnorm(y, gmix_ref[...])

    ctx_tiles = GROUP_ROWS // FFN_TM
    if not attn:
        _ffn_tile(plain, *tail)
    elif tile0 >= ctx_tiles:
        _ffn_tile(mixed, *tail)
    else:
        pl.when(tile < ctx_tiles)(lambda: _ffn_tile(plain, *tail))
        pl.when(tile >= ctx_tiles)(lambda: _ffn_tile(mixed, *tail))


def _half_ffn(x, w1b, w2b, mods, pre, post, layer, slot, *, x_b=None, first_row=0, n_rows=T_ALL,
              dst_rows=T_ALL, dst_row=None, attn=None, convert=None):
    k = 2 * slot
    tiles_per_group = GROUP_ROWS // FFN_TM
    n_tiles = n_rows // FFN_TM
    t0 = first_row // FFN_TM
    d0 = t0 if dst_row is None else dst_row // FFN_TM
    tiles_a = None
    in_specs = [pl.BlockSpec((FFN_TM, D), lambda i: (t0 + i, 0))]
    args = [x]
    if x_b is not None:
        tiles_a = x.shape[0] // FFN_TM
        in_specs = [pl.BlockSpec((FFN_TM, D), lambda i: (jnp.minimum(i, tiles_a - 1), 0)),
                    pl.BlockSpec((FFN_TM, D), lambda i: (jnp.maximum(i - tiles_a, 0), 0))]
        args.append(x_b)
    if attn:
        in_specs.append(pl.BlockSpec(
            (FFN_TM, D), lambda i: (jnp.maximum(t0 + i - tiles_per_group, 0), 0)))
        args.append(attn[0])
    in_specs += [
        _mod_spec(layer, lambda i: (t0 + i) // tiles_per_group),
        _const_spec((1, D), layer, k),
        _const_spec((1, D), layer, k),
        _const_spec((D, 2 * D_FF)),
        _const_spec((D_FF, D)),
    ]
    args += [mods, pre, post, w1b, w2b]
    if attn:
        in_specs += [_const_spec((1, D), layer, 1), _const_spec((D, D), layer // 2)]
        args += [post, attn[1]]
    out_specs = [pl.BlockSpec((FFN_TM, D), lambda i: (d0 + i, 0))]
    out_shape = [jax.ShapeDtypeStruct((dst_rows, D), F32)]
    if convert:
        nw1, nw2, nl, ns = convert
        assert n_tiles >= FFN_CONV_STEPS
        r1, r2 = D // FFN_CONV_STEPS, D_FF // FFN_CONV_STEPS
        last = FFN_CONV_STEPS - 1
        in_specs += [
            pl.BlockSpec((None, None, r1, 2 * D_FF), lambda i: (nl, ns, jnp.minimum(i, last), 0)),
            pl.BlockSpec((None, None, r2, D), lambda i: (nl, ns, jnp.minimum(i, last), 0)),
        ]
        args += [nw1, nw2]
        out_specs += [pl.BlockSpec((r1, 2 * D_FF), lambda i: (jnp.minimum(i, last), 0)),
                      pl.BlockSpec((r2, D), lambda i: (jnp.minimum(i, last), 0))]
        out_shape += [jax.ShapeDtypeStruct((D, 2 * D_FF), BF16),
                      jax.ShapeDtypeStruct((D_FF, D), BF16)]
    res = pl.pallas_call(
        functools.partial(_ffn_body, k=k, tile0=t0, tiles_a=tiles_a, attn=bool(attn),
                          convert=bool(convert)),
        grid=(n_tiles,),
        in_specs=in_specs,
        out_specs=out_specs,
        out_shape=out_shape,
        compiler_params=_cparams(1),
        name="half_ffn",
    )(*args)
    return (res[0], res[1], res[2]) if convert else (res[0], None, None)


def _dft_cos_sin(n):
    idx = np.arange(n)
    ang = 2.0 * np.pi * ((idx[:, None] * idx[None, :]) % n) / n
    return np.cos(ang), np.sin(ang)


def _fourier_constants():
    c128, s128 = _dft_cos_sin(FG_DIM)
    chan = np.stack([c128, s128]) / np.sqrt(FG_DIM)

    c256, s256 = _dft_cos_sin(SEQ)
    dft_prompt = np.concatenate([c256, -s256], axis=0) / np.sqrt(SEQ)

    c64, s64 = _dft_cos_sin(GRID)
    c64 = c64 / np.sqrt(GRID)
    s64 = s64 / np.sqrt(GRID)
    eye = np.eye(FS_SUB)
    rows = GRID * FS_SUB
    m1 = np.concatenate([np.kron(c64, eye), np.kron(-s64, eye)], axis=0)
    def spread(f):
        return np.einsum("kn,jl->kjln", f, eye).reshape(rows, rows)
    m2 = np.block([[spread(c64), spread(s64)], [spread(-s64), spread(c64)]])
    k1 = np.arange(GRID)[:, None]
    n2 = np.arange(GRID)[None, :]
    ang = 2.0 * np.pi * (k1 * n2) / LAT
    tw = np.stack([np.cos(ang), np.sin(ang)])[..., None] * np.ones((1, 1, 1, 128))

    def mxu_const(a):
        return jnp.asarray(a, F32).astype(BF16)

    return (jnp.asarray(chan, F32), mxu_const(dft_prompt), mxu_const(m1), mxu_const(m2),
            jnp.asarray(tw, F32))


def _fold_body(cs_ref, w_ref, o_ref):
    w = w_ref[...]
    for t in range(2):
        o_ref[t] = jnp.dot(cs_ref[t], w, preferred_element_type=F32,
                           precision=lax.Precision.HIGHEST).astype(BF16)


def _fold_channel_dft(chan, w_out):
    return pl.pallas_call(
        _fold_body,
        grid=(N_FG,),
        in_specs=[
            pl.BlockSpec((2, FG_DIM, FG_DIM), lambda g: (0, 0, 0)),
            pl.BlockSpec((FG_DIM, D), lambda g: (g, 0)),
        ],
        out_specs=pl.BlockSpec((2, FG_DIM, D), lambda g: (0, g, 0)),
        out_shape=jax.ShapeDtypeStruct((2, D, D), BF16),
        compiler_params=_cparams(1),
        name="fold_channel_dft",
    )(chan, w_out)


def _four_prompt_body(x_ref, mod_ref, gpre_ref, gpost_ref, win_ref, dft_ref, w2_ref, o_ref):
    x = x_ref[...]
    h = _prenorm(x, gpre_ref[...], mod_ref, 1)
    u = _dot(h, win_ref[...]).astype(BF16)
    p = _dot(dft_ref[...], u)
    y = _dot(p[:SEQ].astype(BF16), w2_ref[0]) + _dot(p[SEQ:].astype(BF16), w2_ref[1])
    o_ref[...] = x + mod_ref[5:6, :] * _postnorm(y, gpost_ref[...])


def _fourier_prompt(x, mods, pre, post, w_in, dft_prompt, w2, layer):
    return pl.pallas_call(
        _four_prompt_body,
        grid=(BATCH,),
        in_specs=[
            pl.BlockSpec((SEQ, D), lambda b: (b, 0)),
            _mod_spec(layer, lambda b: 0),
            _const_spec((1, D), layer, 1),
            _const_spec((1, D), layer, 1),
            _const_spec((D, D), layer // 2),
            _const_spec((2 * SEQ, SEQ)),
            _const_spec((2, D, D)),
        ],
        out_specs=pl.BlockSpec((SEQ, D), lambda b: (b, 0)),
        out_shape=jax.ShapeDtypeStruct((T_ALL, D), F32),
        input_output_aliases={0: 0},
        compiler_params=_cparams(1),
        name="fourier_prompt",
    )(x, mods, pre, post, w_in, dft_prompt, w2)


def _four_s1_body(x_ref, mod_ref, gpre_ref, win_ref, m1_ref, tw_ref, y_ref, y_scr):
    rows = GRID * FS_SUB
    for s in range(FS_TN // FS_SUB):
        cols = slice(s * FS_SUB, (s + 1) * FS_SUB)
        x = x_ref[:, cols, :].reshape(rows, D)
        h = _prenorm(x, gpre_ref[...], mod_ref, 1)
        u = _dot(h, win_ref[...]).astype(BF16)
        y = _dot(m1_ref[...], u)
        tc = tw_ref[0, :, cols, :].reshape(rows, 128)
        ts = tw_ref[1, :, cols, :].reshape(rows, 128)
        for l in range(D // 128):
            sl = slice(l * 128, (l + 1) * 128)
            yr = y[:rows, sl]
            yi = y[rows:, sl]
            y_scr[0, :, cols, sl] = (yr * tc + yi * ts).reshape(GRID, FS_SUB, 128)
            y_scr[1, :, cols, sl] = (yi * tc - yr * ts).reshape(GRID, FS_SUB, 128)
    y_ref[...] = y_scr[...].astype(BF16)


def _fourier_latent_stage1(x4, mods, pre, w_in, m1, tw, layer):
    rows = GRID * FS_SUB
    return pl.pallas_call(
        _four_s1_body,
        grid=(N_LAT, GRID // FS_TN),
        in_specs=[
            pl.BlockSpec((None, GRID, FS_TN, D), lambda b, j: (b + 1, 0, j, 0)),
            _mod_spec(layer, lambda b, j: b + 1),
            _const_spec((1, D), layer, 1),
            _const_spec((D, D), layer // 2),
            _const_spec((2 * rows, rows)),
            pl.BlockSpec((2, GRID, FS_TN, 128), lambda b, j: (0, 0, j, 0)),
        ],
        out_specs=pl.BlockSpec((None, 2, GRID, FS_TN, D), lambda b, j: (b, 0, 0, j, 0)),
        out_shape=jax.ShapeDtypeStruct((N_LAT, 2, GRID, GRID, D), BF16),
        scratch_shapes=[pltpu.VMEM((2, GRID, FS_TN, D), F32)],
        compiler_params=_cparams(2),
        name="fourier_latent_stage1",
    )(x4, mods, pre, w_in, m1, tw)


def _four_s2_body(y_ref, x_ref, mod_ref, gpost_ref, m2_ref, w2_ref, o_ref):
    rows = GRID * FS_SUB
    yin = y_ref[...].reshape(2 * rows, D)
    z = _dot(m2_ref[...], yin)
    y = _dot(z[:rows].astype(BF16), w2_ref[0]) + _dot(z[rows:].astype(BF16), w2_ref[1])
    x = x_ref[...].reshape(rows, D)
    out = x + mod_ref[5:6, :] * _postnorm(y, gpost_ref[...])
    o_ref[...] = out.reshape(GRID, FS_SUB, D)


def _fourier_latent_stage2(y5, x4, mods, post, m2, w2, layer):
    rows = GRID * FS_SUB
    return pl.pallas_call(
        _four_s2_body,
        grid=(N_LAT, GRID // FS_SUB),
        in_specs=[
            pl.BlockSpec((None, 2, FS_SUB, GRID, D), lambda b, j: (b, 0, j, 0, 0)),
            pl.BlockSpec((None, GRID, FS_SUB, D), lambda b, j: (b + 1, 0, j, 0)),
            _mod_spec(layer, lambda b, j: b + 1),
            _const_spec((1, D), layer, 1),
            _const_spec((2 * rows, 2 * rows)),
            _const_spec((2, D, D)),
        ],
        out_specs=pl.BlockSpec((None, GRID, FS_SUB, D), lambda b, j: (b + 1, 0, j, 0)),
        out_shape=jax.ShapeDtypeStruct((N_GROUPS, GRID, GRID, D), F32),
        input_output_aliases={1: 0},
        compiler_params=_cparams(2),
        name="fourier_latent_stage2",
    )(y5, x4, mods, post, m2, w2)


def _ctx_attn_body(x_ref, mod_ref, gpre_ref, gpost_ref, wqkv_ref, wout_ref, *rest, fresh_slot):
    o_ref, kc_ref, vc_ref = rest[-3:]
    x = x_ref[...]
    h = _prenorm(x, gpre_ref[...], mod_ref, 1)
    qkv = _dot(h, wqkv_ref[...])
    k = qkv[:, D:2 * D]
    v = qkv[:, 2 * D:]
    for nb in range(CTX_NB):
        rows = slice(nb * SEQ, (nb + 1) * SEQ)
        for hd in range(N_HEADS):
            sl = slice(hd * HEAD_DIM, (hd + 1) * HEAD_DIM)
            if fresh_slot is None:
                kc_ref[nb, hd] = k[rows, sl]
                vc_ref[nb, hd] = v[rows, sl]
            else:
                kc_ref[nb, fresh_slot, hd] = k[rows, sl]
                vc_ref[nb, fresh_slot, hd] = v[rows, sl]
        if fresh_slot is not None:
            other = jnp.zeros((N_HEADS, SEQ, HEAD_DIM), F32)
            kc_ref[nb, 1 - fresh_slot] = other
            vc_ref[nb, 1 - fresh_slot] = other
    qb = (qkv[:, :D] * Q_SCALE).astype(BF16)
    kb = k.astype(BF16)
    vb = v.astype(BF16)
    first = lax.broadcasted_iota(jnp.int32, (1, 2 * HEAD_DIM), 1) < HEAD_DIM
    zero = jnp.zeros((), BF16)
    seqs = []
    for nb in range(CTX_NB):
        rows = slice(nb * SEQ, (nb + 1) * SEQ)
        outs = []
        for p in range(N_PAIRS):
            sl = slice(p * 2 * HEAD_DIM, (p + 1) * 2 * HEAD_DIM)
            kp = kb[rows, sl]
            vp = vb[rows, sl]
            k2 = jnp.concatenate([jnp.where(first, kp, zero), jnp.where(first, zero, kp)], axis=0)
            v2 = jnp.concatenate([jnp.where(first, vp, zero), jnp.where(first, zero, vp)], axis=0)
            s = _dot_nt(qb[rows, sl], k2)
            probs = []
            for hh in range(2):
                sh = s[:, hh * SEQ:(hh + 1) * SEQ]
                e = jnp.exp2(sh - jnp.max(sh, axis=-1, keepdims=True))
                probs.append(e / jnp.sum(e, axis=-1, keepdims=True))
            outs.append(_dot(jnp.concatenate(probs, axis=1).astype(BF16), v2))
        seqs.append(jnp.concatenate(outs, axis=1).astype(BF16))
    y = _dot(jnp.concatenate(seqs, axis=0), wout_ref[...])
    o_ref[...] = x + mod_ref[5:6, :] * _postnorm(y, gpost_ref[...])


def _context_attention(x, mods, pre, post, w_qkv, w_out, layer, caches):
    j = layer // 2
    rows = CTX_NB * SEQ
    cache = jax.ShapeDtypeStruct((BATCH, DEPTH // 2, N_HEADS, SEQ, HEAD_DIM), F32)
    if caches:
        cache_spec = pl.BlockSpec((CTX_NB, None, N_HEADS, SEQ, HEAD_DIM), lambda b: (b, j, 0, 0, 0))
    else:
        cache_spec = pl.BlockSpec((CTX_NB, DEPTH // 2, N_HEADS, SEQ, HEAD_DIM),
                                  lambda b: (b, 0, 0, 0, 0))
    in_specs = [
        pl.BlockSpec((rows, D), lambda b: (b, 0)),
        _mod_spec(layer, lambda b: 0),
        _const_spec((1, D), layer, 1),
        _const_spec((1, D), layer, 1),
        _const_spec((D, 3 * D), j),
        _const_spec((D, D), j),
    ]
    aliases = {0: 0}
    if caches:
        in_specs += [pl.BlockSpec(memory_space=pl.ANY)] * 2
        aliases.update({6: 1, 7: 2})
    return pl.pallas_call(
        functools.partial(_ctx_attn_body, fresh_slot=None if caches else j),
        grid=(BATCH // CTX_NB,),
        in_specs=in_specs,
        out_specs=[pl.BlockSpec((rows, D), lambda b: (b, 0)), cache_spec, cache_spec],
        out_shape=[jax.ShapeDtypeStruct((T_ALL, D), F32), cache, cache],
        input_output_aliases=aliases,
        compiler_params=_cparams(1),
        name="context_attention",
    )(x, mods, pre, post, w_qkv, w_out, *caches)


def _qkv_lat_body(x_ref, mod_ref, gpre_ref, wq_ref, wkt_ref, wv_ref, q_ref, kt_ref, v_ref):
    h = _prenorm(x_ref[...], gpre_ref[...], mod_ref, 1)
    q_ref[...] = (_dot(h, wq_ref[...]) * Q_SCALE).astype(BF16)
    v_ref[...] = _dot(h, wv_ref[...]).astype(BF16)
    kt_ref[...] = _dot_nt(wkt_ref[...], h).astype(BF16)


def _qkv_latent(x, mods, pre, w_qkv, wkt, layer):
    j = layer // 2
    per_lat = LAT // QKV_TM
    first = GROUP_ROWS // QKV_TM
    tok = jax.ShapeDtypeStruct((N_LAT * LAT, D), BF16)

    def w_cols(col_block):
        return pl.BlockSpec((None, D, D), lambda i: (j, 0, col_block), pipeline_mode=pl.Buffered(1))

    return pl.pallas_call(
        _qkv_lat_body,
        grid=(N_LAT * per_lat,),
        in_specs=[
            pl.BlockSpec((QKV_TM, D), lambda i: (first + i, 0)),
            _mod_spec(layer, lambda i: 1 + i // per_lat),
            _const_spec((1, D), layer, 1),
            w_cols(0),
            _const_spec((D, D), j),
            w_cols(2),
        ],
        out_specs=[
            pl.BlockSpec((QKV_TM, D), lambda i: (i, 0)),
            pl.BlockSpec((None, D, QKV_TM), lambda i: (i // per_lat, 0, i % per_lat)),
            pl.BlockSpec((QKV_TM, D), lambda i: (i, 0)),
        ],
        out_shape=[tok, jax.ShapeDtypeStruct((N_LAT, D, LAT), BF16), tok],
        compiler_params=_cparams(1),
        name="qkv_latent",
    )(x, mods, pre, w_qkv, wkt, w_qkv)


def _na_block_geometry(blk):
    return min(max(NA_ROWS * blk - WIN_ROWS // 2, 0), GRID - NA_KROWS)


def _na_variant(blk):
    return 0 if blk == 0 else (2 if blk == NA_BLOCKS - 1 else 1)


def _na_tables():
    t = np.arange(NA_DT)[:, None]
    e = np.arange(2)[None, :]
    drow = t - 11 + e
    row_ok = (drow >= -(WIN_ROWS - 1)) & (drow <= WIN_ROWS - 1)
    c = np.arange(GRID)[:, None]
    cp = np.arange(GRID)[None, :]
    cs = np.clip(c - WIN_COLS // 2, 0, GRID - WIN_COLS)
    col_ok = (cp >= cs) & (cp < cs + WIN_COLS)
    ok = row_ok[:, None, :, None] & col_ok[None, :, None, :]
    rowmask = np.zeros((NA_MASK_ROWS, NA_K), np.float32)
    delta0 = np.zeros((NA_VARIANTS, NA_ROWS, NA_KROWS // 2), np.int64)
    seen = set()
    for blk in range(NA_BLOCKS):
        v = _na_variant(blk)
        u0 = _na_block_geometry(blk)
        for rl in range(NA_ROWS):
            r = NA_ROWS * blk + rl
            rs = min(max(r - WIN_ROWS // 2, 0), GRID - WIN_ROWS)
            kr = u0 + np.arange(NA_KROWS)
            valid = (kr >= rs) & (kr < rs + WIN_ROWS)
            mask = np.repeat(np.where(valid, 0.0, NEG), GRID).astype(np.float32)
            d0 = u0 - r + 2 * np.arange(NA_KROWS // 2) + 11
            if (v, rl) in seen:
                assert (rowmask[v * NA_ROWS + rl] == mask).all() and (delta0[v, rl] == d0).all()
            seen.add((v, rl))
            rowmask[v * NA_ROWS + rl] = mask
            delta0[v, rl] = d0
    assert delta0.min() >= 0 and delta0.max() < NA_DT
    return ok, rowmask, delta0


def _na_bias_pairs(rpb, ok):
    nrow = 2 * WIN_ROWS - 1
    period = 2 * GRID
    fill = jnp.full((N_HEADS, nrow, period - (2 * WIN_COLS - 1)), NEG, F32)
    p = jnp.concatenate([rpb[:, :, WIN_COLS - 1:], fill, rpb[:, :, :WIN_COLS - 1]], axis=-1)
    toep = jnp.tile(p, (1, 1, GRID))[..., :GRID * (period - 1)]
    toep = toep.reshape(N_HEADS, nrow, GRID, period - 1)[..., :GRID]
    masked = jnp.full((N_HEADS, GRID, GRID), NEG, F32)

    def rows(drow):
        idx = drow + WIN_ROWS - 1
        return toep[:, idx] if 0 <= idx < nrow else masked

    tab = jnp.stack([jnp.concatenate([rows(t - 11), rows(t - 10)], axis=-1)
                     for t in range(NA_DT)], axis=1)
    return jnp.where(ok.reshape(1, NA_DT, GRID, 2 * GRID), tab * LOG2E, NEG)


def _natten_body(q_ref, kt_ref, v_ref, kct_ref, vc_ref, bp_ref, rm_ref, o_ref,
                 bias_scr, s_scr, p_scr, rden_scr, *, delta0):
    for hh in range(2):
        for var in range(NA_VARIANTS):
            for rl in range(NA_ROWS):
                strip = jnp.concatenate(
                    [bp_ref[hh, int(delta0[var, rl, i])] for i in range(NA_KROWS // 2)], axis=1)
                row = var * NA_ROWS + rl
                bias_scr[hh, var, rl * GRID:(rl + 1) * GRID, :] = strip + rm_ref[row:row + 1, :]

    first = lax.broadcasted_iota(jnp.int32, (1, 2 * HEAD_DIM), 1) < HEAD_DIM
    zero = jnp.zeros((), BF16)
    kct = kct_ref[...]
    vc = vc_ref[...]

    keeps = (first, jnp.logical_not(first))

    def offsets(blk):
        koff = pl.multiple_of(jnp.clip(blk - 1, 0, NA_BLOCKS - 3) * NA_Q, NA_Q)
        qoff = pl.multiple_of(blk * NA_Q, NA_Q)
        return koff, qoff

    def logits(blk, slot):
        koff, qoff = offsets(blk)
        var = jnp.where(blk == 0, 0, jnp.where(blk == NA_BLOCKS - 1, 2, 1))
        qb = q_ref[pl.ds(qoff, NA_Q), :]
        ktb = kt_ref[:, pl.ds(koff, NA_K)]
        for hh in range(2):
            qh = jnp.where(keeps[hh], qb, zero)
            s_scr[slot, hh, :, :NA_K] = _dot(qh, ktb) + bias_scr[hh, var]
            s_scr[slot, hh, :, NA_K:] = _dot(qh, kct)

    def softmax(slot):
        for hh in range(2):
            s = s_scr[slot, hh]
            e = jnp.exp2(s - jnp.max(s, axis=-1, keepdims=True))
            p_scr[slot, hh] = e.astype(BF16)
            rden = 1.0 / jnp.sum(e, axis=-1, keepdims=True)
            rden_scr[slot, hh] = jnp.broadcast_to(rden, (NA_Q, 2 * HEAD_DIM))

    def values(blk, slot):
        koff, qoff = offsets(blk)
        vb = v_ref[pl.ds(koff, NA_K), :]
        o = jnp.zeros((NA_Q, 2 * HEAD_DIM), F32)
        for hh in range(2):
            oh = (_dot(p_scr[slot, hh, :, :NA_K], jnp.where(keeps[hh], vb, zero))
                  + _dot(p_scr[slot, hh, :, NA_K:], jnp.where(keeps[hh], vc, zero)))
            o = o + oh * rden_scr[slot, hh]
        o_ref[pl.ds(qoff, NA_Q), :] = o.astype(BF16)

    logits(0, 0)
    softmax(0)
    logits(1, 1)

    def steady(it, carry):
        t = 2 + 2 * it
        logits(t, 0)
        values(t - 2, 0)
        softmax(1)
        logits(t + 1, 1)
        values(t - 1, 1)
        softmax(0)
        return carry

    lax.fori_loop(0, (NA_BLOCKS - 2) // 2, steady, 0)
    values(NA_BLOCKS - 2, 0)
    softmax(1)
    values(NA_BLOCKS - 1, 1)


def _neighbourhood_attention(q, kt, v, kct, vc, bias_pairs, rowmask, delta0):
    lane = 2 * HEAD_DIM
    return pl.pallas_call(
        functools.partial(_natten_body, delta0=delta0),
        grid=(N_LAT, N_PAIRS),
        in_specs=[
            pl.BlockSpec((LAT, lane), lambda b, p: (b, p)),
            pl.BlockSpec((None, lane, LAT), lambda b, p: (b, p, 0)),
            pl.BlockSpec((LAT, lane), lambda b, p: (b, p)),
            pl.BlockSpec((None, lane, PAST), lambda b, p: (b, p, 0)),
            pl.BlockSpec((None, PAST, lane), lambda b, p: (b, 0, p)),
            pl.BlockSpec((2, NA_DT, GRID, lane), lambda b, p: (p, 0, 0, 0)),
            pl.BlockSpec((NA_MASK_ROWS, NA_K), lambda b, p: (0, 0)),
        ],
        out_specs=pl.BlockSpec((LAT, lane), lambda b, p: (b, p)),
        out_shape=jax.ShapeDtypeStruct((N_LAT * LAT, D), BF16),
        scratch_shapes=[
            pltpu.VMEM((2, NA_VARIANTS, NA_Q, NA_K), F32),
            pltpu.VMEM((2, 2, NA_Q, NA_K + PAST), F32),
            pltpu.VMEM((2, 2, NA_Q, NA_K + PAST), BF16),
            pltpu.VMEM((2, 2, NA_Q, lane), F32),
        ],
        compiler_params=_cparams(2),
        name="neighbourhood_attention",
    )(q, kt, v, kct, vc, bias_pairs, rowmask)


def kernel(x_prompt, x_sample, cache_k, cache_v, c, c_ctx, w_mod, b_mod, norm_pre, norm_post,
           ffn_w1, ffn_w2, four_w_in, four_w_out, na_w_qkv, na_w_out, na_rpb):
    conds = jnp.concatenate([c_ctx[None, :], c, jnp.zeros((8 - 1 - N_LAT, D), F32)], axis=0)
    mods = _modulation(conds, w_mod, b_mod)[:, :N_GROUPS].reshape(DEPTH, N_GROUPS, N_MOD, D)

    chan, dft_prompt, m1, m2, tw = _fourier_constants()
    ok, rowmask, delta0 = _na_tables()
    rowmask = jnp.asarray(rowmask)

    w_in = four_w_in.astype(BF16)
    w_qkv = na_w_qkv.astype(BF16)
    w_out = na_w_out.astype(BF16)
    wkt = w_qkv[:, :, D:2 * D].transpose(0, 2, 1)
    pre = norm_pre[:, :, None, :]
    post = norm_post[:, :, None, :]

    n_ctx = BATCH * SEQ
    n_lat = N_LAT * LAT
    ffn = functools.partial(_half_ffn, mods=mods, pre=pre, post=post)
    w1b = ffn_w1[0, 0].astype(BF16)
    w2b = ffn_w2[0, 0].astype(BF16)
    caches = ()
    for i in range(DEPTH):
        j = i // 2
        nxt = (ffn_w1, ffn_w2, i, 1)
        if i == 0:
            x, w1b, w2b = ffn(x_prompt.reshape(n_ctx, D), w1b, w2b, layer=0, slot=0,
                              x_b=x_sample.reshape(n_lat, D), convert=nxt)
        else:
            x, w1b, w2b = ffn(x, w1b, w2b, layer=i, slot=0, convert=nxt)
        attn = None
        if i % 2 == 0:
            wf = _fold_channel_dft(chan, four_w_out[j])
            x = _fourier_prompt(x, mods, pre, post, w_in, dft_prompt, wf, i)
            x4 = x.reshape(N_GROUPS, GRID, GRID, D)
            y5 = _fourier_latent_stage1(x4, mods, pre, w_in, m1, tw, i)
            x4 = _fourier_latent_stage2(y5, x4, mods, post, m2, wf, i)
            x = x4.reshape(T_ALL, D)
        else:
            x, *caches = _context_attention(x, mods, pre, post, w_qkv, w_out, i, caches)
            q, kt, v = _qkv_latent(x, mods, pre, w_qkv, wkt, i)
            kct = cache_k[:, j].transpose(0, 1, 3, 2).reshape(N_LAT, D, PAST).astype(BF16)
            vct = cache_v[:, j].transpose(0, 2, 1, 3).reshape(N_LAT, PAST, D).astype(BF16)
            bias_pairs = _na_bias_pairs(na_rpb[j], ok)
            o = _neighbourhood_attention(q, kt, v, kct, vct, bias_pairs, rowmask, delta0)
            attn = (o, w_out)
        if i < DEPTH - 1:
            x, w1b, w2b = ffn(x, w1b, w2b, layer=i, slot=1, attn=attn,
                              convert=(ffn_w1, ffn_w2, i + 1, 0))

    last = DEPTH - 1
    y_prompt, _, _ = ffn(x, w1b, w2b, layer=last, slot=1, n_rows=n_ctx, dst_rows=n_ctx)
    y_sample, _, _ = ffn(x, w1b, w2b, layer=last, slot=1, first_row=n_ctx, n_rows=n_lat,
                         dst_rows=n_lat, dst_row=0, attn=attn)
    return (y_prompt.reshape(BATCH, SEQ, D), y_sample.reshape(N_LAT, LAT, D), caches[0], caches[1])
```

```python
import functools

import numpy as np
import jax
import jax.numpy as jnp
from jax import lax
from jax.experimental import pallas as pl
from jax.experimental.pallas import tpu as pltpu

F32 = jnp.float32
BF16 = jnp.bfloat16

D = 1024
D_FF = 2816
DEPTH = 4
N_MOD = 9
N_HEADS = 16
HEAD_DIM = 64
N_PAIRS = N_HEADS // 2
SEQ = 256
BATCH = 16
GRID = 64
LAT = GRID * GRID
N_LAT = 2
GROUP_ROWS = 4096
N_GROUPS = 3
T_ALL = N_GROUPS * GROUP_ROWS
PAST = 256
WIN_ROWS = 8
WIN_COLS = 16
N_FG = 8
FG_DIM = D // N_FG
EPS = 1e-6
NEG = -1e30
LOG2E = 1.4426950408889634
Q_SCALE = HEAD_DIM ** -0.5 * LOG2E

VMEM_LIMIT = 56 * 1024 * 1024

FFN_TM = 512
FFN_SUB = 256
FFN_NC = 1
FFN_CONV_STEPS = 16
MOD_TN = 2304
QKV_TM = 512
CTX_NB = 1
FS_TN = 16
FS_SUB = 8
NA_ROWS = 4
NA_Q = NA_ROWS * GRID
NA_KROWS = 12
NA_K = NA_KROWS * GRID
NA_BLOCKS = GRID // NA_ROWS
NA_VARIANTS = 3
NA_MASK_ROWS = 16
NA_DT = 22


def _cparams(n_axes):
    return pltpu.CompilerParams(
        dimension_semantics=("arbitrary",) * n_axes, vmem_limit_bytes=VMEM_LIMIT)


def _const_spec(shape, *lead):
    nd = len(shape)
    return pl.BlockSpec((None,) * len(lead) + tuple(shape), lambda *_: tuple(lead) + (0,) * nd,
                        pipeline_mode=pl.Buffered(1))


def _mod_spec(layer, group_of):
    return pl.BlockSpec((None, None, N_MOD, D), lambda *ids: (layer, group_of(*ids), 0, 0))


def _silu(x):
    return x * jax.nn.sigmoid(x)


def _prenorm(x, g, mod_ref, k):
    shift = mod_ref[3 * k:3 * k + 1, :]
    scale = mod_ref[3 * k + 1:3 * k + 2, :]
    y = x * lax.rsqrt(jnp.mean(x * x, axis=-1, keepdims=True) + EPS)
    return ((y * g) * (1.0 + scale) + shift).astype(BF16)


def _postnorm(y, g):
    return (y * lax.rsqrt(jnp.mean(y * y, axis=-1, keepdims=True) + EPS)) * g


def _dot(a, b):
    return jnp.dot(a, b, preferred_element_type=F32)


def _dot_nt(a, b):
    return lax.dot_general(a, b, (((1,), (1,)), ((), ())), preferred_element_type=F32)


def _mod_body(cond_ref, w_ref, b_ref, o_ref):
    s = _silu(cond_ref[...]).astype(BF16)
    o_ref[...] = _dot(s, w_ref[...].astype(BF16)) + b_ref[...]


def _modulation(conds, w_mod, b_mod):
    n = N_MOD * D
    return pl.pallas_call(
        _mod_body,
        grid=(DEPTH, n // MOD_TN),
        in_specs=[
            pl.BlockSpec((8, D), lambda l, j: (0, 0)),
            pl.BlockSpec((None, D, MOD_TN), lambda l, j: (l, 0, j)),
            pl.BlockSpec((None, 1, MOD_TN), lambda l, j: (l, 0, j)),
        ],
        out_specs=pl.BlockSpec((None, 8, MOD_TN), lambda l, j: (l, 0, j)),
        out_shape=jax.ShapeDtypeStruct((DEPTH, 8, n), F32),
        compiler_params=_cparams(2),
        name="modulation",
    )(conds, w_mod, b_mod.reshape(DEPTH, 1, n))


def _ffn_tile(load_x, mod_ref, gpre_ref, gpost_ref, w1_ref, w2_ref, o_ref, k):
    ck = D_FF // FFN_NC
    gate = mod_ref[3 * k + 2:3 * k + 3, :]
    for r in range(FFN_TM // FFN_SUB):
        rows = slice(r * FFN_SUB, (r + 1) * FFN_SUB)
        x = load_x(rows)
        h = _prenorm(x, gpre_ref[...], mod_ref, k)
        y = None
        for c in range(FFN_NC):
            g = _dot(h, w1_ref[:, c * ck:(c + 1) * ck])
            u = _dot(h, w1_ref[:, D_FF + c * ck:D_FF + (c + 1) * ck])
            a = (_silu(g) * u).astype(BF16)
            yc = _dot(a, w2_ref[c * ck:(c + 1) * ck, :])
            y = yc if y is None else y + yc
        o_ref[rows, :] = x + (0.5 * gate) * _postnorm(y, gpost_ref[...])


def _ffn_body(*refs, k, tile0, tiles_a, attn, convert):
    refs = list(refs)
    x_ref = refs.pop(0)
    xb_ref = refs.pop(0) if tiles_a is not None else None
    ao_ref = refs.pop(0) if attn else None
    mod_ref, gpre_ref, gpost_ref, w1_ref, w2_ref = refs[:5]
    refs = refs[5:]
    if attn:
        gmix_ref, wout_ref = refs[:2]
        refs = refs[2:]
    if convert:
        nw1_ref, nw2_ref, o_ref, nw1b_ref, nw2b_ref = refs
        nw1b_ref[...] = nw1_ref[...].astype(BF16)
        nw2b_ref[...] = nw2_ref[...].astype(BF16)
    else:
        (o_ref,) = refs
    tile = tile0 + pl.program_id(0)
    tail = (mod_ref, gpre_ref, gpost_ref, w1_ref, w2_ref, o_ref, k)

    def plain(rows):
        if xb_ref is None:
            return x_ref[rows, :]
        return jnp.where(tile < tiles_a, x_ref[rows, :], xb_ref[rows, :])

    def mixed(rows):
        y = _dot(ao_ref[rows, :], wout_ref[...])
        return x_ref[rows, :] + mod_ref[5:6, :] * _postnorm(y, gmix_ref[...])

    ctx_tiles = GROUP_ROWS // FFN_TM
    if not attn:
        _ffn_tile(plain, *tail)
    elif tile0 >= ctx_tiles:
        _ffn_tile(mixed, *tail)
    else:
        pl.when(tile < ctx_tiles)(lambda: _ffn_tile(plain, *tail))
        pl.when(tile >= ctx_tiles)(lambda: _ffn_tile(mixed, *tail))


def _half_ffn(x, w1b, w2b, mods, pre, post, layer, slot, *, x_b=None, first_row=0, n_rows=T_ALL,
              dst_rows=T_ALL, dst_row=None, attn=None, convert=None):
    k = 2 * slot
    tiles_per_group = GROUP_ROWS // FFN_TM
    n_tiles = n_rows // FFN_TM
    t0 = first_row // FFN_TM
    d0 = t0 if dst_row is None else dst_row // FFN_TM
    tiles_a = None
    in_specs = [pl.BlockSpec((FFN_TM, D), lambda i: (t0 + i, 0))]
    args = [x]
    if x_b is not None:
        tiles_a = x.shape[0] // FFN_TM
        in_specs = [pl.BlockSpec((FFN_TM, D), lambda i: (jnp.minimum(i, tiles_a - 1), 0)),
                    pl.BlockSpec((FFN_TM, D), lambda i: (jnp.maximum(i - tiles_a, 0), 0))]
        args.append(x_b)
    if attn:
        in_specs.append(pl.BlockSpec(
            (FFN_TM, D), lambda i: (jnp.maximum(t0 + i - tiles_per_group, 0), 0)))
        args.append(attn[0])
    in_specs += [
        _mod_spec(layer, lambda i: (t0 + i) // tiles_per_group),
        _const_spec((1, D), layer, k),
        _const_spec((1, D), layer, k),
        _const_spec((D, 2 * D_FF)),
        _const_spec((D_FF, D)),
    ]
    args += [mods, pre, post, w1b, w2b]
    if attn:
        in_specs += [_const_spec((1, D), layer, 1), _const_spec((D, D), layer // 2)]
        args += [post, attn[1]]
    out_specs = [pl.BlockSpec((FFN_TM, D), lambda i: (d0 + i, 0))]
    out_shape = [jax.ShapeDtypeStruct((dst_rows, D), F32)]
    if convert:
        nw1, nw2, nl, ns = convert
        assert n_tiles >= FFN_CONV_STEPS
        r1, r2 = D // FFN_CONV_STEPS, D_FF // FFN_CONV_STEPS
        last = FFN_CONV_STEPS - 1
        in_specs += [
            pl.BlockSpec((None, None, r1, 2 * D_FF), lambda i: (nl, ns, jnp.minimum(i, last), 0)),
            pl.BlockSpec((None, None, r2, D), lambda i: (nl, ns, jnp.minimum(i, last), 0)),
        ]
        args += [nw1, nw2]
        out_specs += [pl.BlockSpec((r1, 2 * D_FF), lambda i: (jnp.minimum(i, last), 0)),
                      pl.BlockSpec((r2, D), lambda i: (jnp.minimum(i, last), 0))]
        out_shape += [jax.ShapeDtypeStruct((D, 2 * D_FF), BF16),
                      jax.ShapeDtypeStruct((D_FF, D), BF16)]
    res = pl.pallas_call(
        functools.partial(_ffn_body, k=k, tile0=t0, tiles_a=tiles_a, attn=bool(attn),
                          convert=bool(convert)),
        grid=(n_tiles,),
        in_specs=in_specs,
        out_specs=out_specs,
        out_shape=out_shape,
        compiler_params=_cparams(1),
        name="half_ffn",
    )(*args)
    return (res[0], res[1], res[2]) if convert else (res[0], None, None)


def _dft_cos_sin(n):
    idx = np.arange(n)
    ang = 2.0 * np.pi * ((idx[:, None] * idx[None, :]) % n) / n
    return np.cos(ang), np.sin(ang)


def _fourier_constants():
    c128, s128 = _dft_cos_sin(FG_DIM)
    chan = np.stack([c128, s128]) / np.sqrt(FG_DIM)

    c256, s256 = _dft_cos_sin(SEQ)
    dft_prompt = np.concatenate([c256, -s256], axis=0) / np.sqrt(SEQ)

    c64, s64 = _dft_cos_sin(GRID)
    c64 = c64 / np.sqrt(GRID)
    s64 = s64 / np.sqrt(GRID)
    eye = np.eye(FS_SUB)
    rows = GRID * FS_SUB
    m1 = np.concatenate([np.kron(c64, eye), np.kron(-s64, eye)], axis=0)
    def spread(f):
        return np.einsum("kn,jl->kjln", f, eye).reshape(rows, rows)
    m2 = np.block([[spread(c64), spread(s64)], [spread(-s64), spread(c64)]])
    k1 = np.arange(GRID)[:, None]
    n2 = np.arange(GRID)[None, :]
    ang = 2.0 * np.pi * (k1 * n2) / LAT
    tw = np.stack([np.cos(ang), np.sin(ang)])[..., None] * np.ones((1, 1, 1, 128))

    def mxu_const(a):
        return jnp.asarray(a, F32).astype(BF16)

    return (jnp.asarray(chan, F32), mxu_const(dft_prompt), mxu_const(m1), mxu_const(m2),
            jnp.asarray(tw, F32))


def _fold_body(cs_ref, w_ref, o_ref):
    w = w_ref[...]
    for t in range(2):
        o_ref[t] = jnp.dot(cs_ref[t], w, preferred_element_type=F32,
                           precision=lax.Precision.HIGHEST).astype(BF16)


def _fold_channel_dft(chan, w_out):
    return pl.pallas_call(
        _fold_body,
        grid=(N_FG,),
        in_specs=[
            pl.BlockSpec((2, FG_DIM, FG_DIM), lambda g: (0, 0, 0)),
            pl.BlockSpec((FG_DIM, D), lambda g: (g, 0)),
        ],
        out_specs=pl.BlockSpec((2, FG_DIM, D), lambda g: (0, g, 0)),
        out_shape=jax.ShapeDtypeStruct((2, D, D), BF16),
        compiler_params=_cparams(1),
        name="fold_channel_dft",
    )(chan, w_out)


def _four_prompt_body(x_ref, mod_ref, gpre_ref, gpost_ref, win_ref, dft_ref, w2_ref, o_ref):
    x = x_ref[...]
    h = _prenorm(x, gpre_ref[...], mod_ref, 1)
    u = _dot(h, win_ref[...]).astype(BF16)
    p = _dot(dft_ref[...], u)
    y = _dot(p[:SEQ].astype(BF16), w2_ref[0]) + _dot(p[SEQ:].astype(BF16), w2_ref[1])
    o_ref[...] = x + mod_ref[5:6, :] * _postnorm(y, gpost_ref[...])


def _fourier_prompt(x, mods, pre, post, w_in, dft_prompt, w2, layer):
    return pl.pallas_call(
        _four_prompt_body,
        grid=(BATCH,),
        in_specs=[
            pl.BlockSpec((SEQ, D), lambda b: (b, 0)),
            _mod_spec(layer, lambda b: 0),
            _const_spec((1, D), layer, 1),
            _const_spec((1, D), layer, 1),
            _const_spec((D, D), layer // 2),
            _const_spec((2 * SEQ, SEQ)),
            _const_spec((2, D, D)),
        ],
        out_specs=pl.BlockSpec((SEQ, D), lambda b: (b, 0)),
        out_shape=jax.ShapeDtypeStruct((T_ALL, D), F32),
        input_output_aliases={0: 0},
        compiler_params=_cparams(1),
        name="fourier_prompt",
    )(x, mods, pre, post, w_in, dft_prompt, w2)


def _four_s1_body(x_ref, mod_ref, gpre_ref, win_ref, m1_ref, tw_ref, y_ref, y_scr):
    rows = GRID * FS_SUB
    for s in range(FS_TN // FS_SUB):
        cols = slice(s * FS_SUB, (s + 1) * FS_SUB)
        x = x_ref[:, cols, :].reshape(rows, D)
        h = _prenorm(x, gpre_ref[...], mod_ref, 1)
        u = _dot(h, win_ref[...]).astype(BF16)
        y = _dot(m1_ref[...], u)
        tc = tw_ref[0, :, cols, :].reshape(rows, 128)
        ts = tw_ref[1, :, cols, :].reshape(rows, 128)
        for l in range(D // 128):
            sl = slice(l * 128, (l + 1) * 128)
            yr = y[:rows, sl]
            yi = y[rows:, sl]
            y_scr[0, :, cols, sl] = (yr * tc + yi * ts).reshape(GRID, FS_SUB, 128)
            y_scr[1, :, cols, sl] = (yi * tc - yr * ts).reshape(GRID, FS_SUB, 128)
    y_ref[...] = y_scr[...].astype(BF16)


def _fourier_latent_stage1(x4, mods, pre, w_in, m1, tw, layer):
    rows = GRID * FS_SUB
    return pl.pallas_call(
        _four_s1_body,
        grid=(N_LAT, GRID // FS_TN),
        in_specs=[
            pl.BlockSpec((None, GRID, FS_TN, D), lambda b, j: (b + 1, 0, j, 0)),
            _mod_spec(layer, lambda b, j: b + 1),
            _const_spec((1, D), layer, 1),
            _const_spec((D, D), layer // 2),
            _const_spec((2 * rows, rows)),
            pl.BlockSpec((2, GRID, FS_TN, 128), lambda b, j: (0, 0, j, 0)),
        ],
        out_specs=pl.BlockSpec((None, 2, GRID, FS_TN, D), lambda b, j: (b, 0, 0, j, 0)),
        out_shape=jax.ShapeDtypeStruct((N_LAT, 2, GRID, GRID, D), BF16),
        scratch_shapes=[pltpu.VMEM((2, GRID, FS_TN, D), F32)],
        compiler_params=_cparams(2),
        name="fourier_latent_stage1",
    )(x4, mods, pre, w_in, m1, tw)


def _four_s2_body(y_ref, x_ref, mod_ref, gpost_ref, m2_ref, w2_ref, o_ref):
    rows = GRID * FS_SUB
    yin = y_ref[...].reshape(2 * rows, D)
    z = _dot(m2_ref[...], yin)
    y = _dot(z[:rows].astype(BF16), w2_ref[0]) + _dot(z[rows:].astype(BF16), w2_ref[1])
    x = x_ref[...].reshape(rows, D)
    out = x + mod_ref[5:6, :] * _postnorm(y, gpost_ref[...])
    o_ref[...] = out.reshape(GRID, FS_SUB, D)


def _fourier_latent_stage2(y5, x4, mods, post, m2, w2, layer):
    rows = GRID * FS_SUB
    return pl.pallas_call(
        _four_s2_body,
        grid=(N_LAT, GRID // FS_SUB),
        in_specs=[
            pl.BlockSpec((None, 2, FS_SUB, GRID, D), lambda b, j: (b, 0, j, 0, 0)),
            pl.BlockSpec((None, GRID, FS_SUB, D), lambda b, j: (b + 1, 0, j, 0)),
            _mod_spec(layer, lambda b, j: b + 1),
            _const_spec((1, D), layer, 1),
            _const_spec((2 * rows, 2 * rows)),
            _const_spec((2, D, D)),
        ],
        out_specs=pl.BlockSpec((None, GRID, FS_SUB, D), lambda b, j: (b + 1, 0, j, 0)),
        out_shape=jax.ShapeDtypeStruct((N_GROUPS, GRID, GRID, D), F32),
        input_output_aliases={1: 0},
        compiler_params=_cparams(2),
        name="fourier_latent_stage2",
    )(y5, x4, mods, post, m2, w2)


def _ctx_attn_body(x_ref, mod_ref, gpre_ref, gpost_ref, wqkv_ref, wout_ref, *rest, fresh_slot):
    o_ref, kc_ref, vc_ref = rest[-3:]
    x = x_ref[...]
    h = _prenorm(x, gpre_ref[...], mod_ref, 1)
    qkv = _dot(h, wqkv_ref[...])
    k = qkv[:, D:2 * D]
    v = qkv[:, 2 * D:]
    for nb in range(CTX_NB):
        rows = slice(nb * SEQ, (nb + 1) * SEQ)
        for hd in range(N_HEADS):
            sl = slice(hd * HEAD_DIM, (hd + 1) * HEAD_DIM)
            if fresh_slot is None:
                kc_ref[nb, hd] = k[rows, sl]
                vc_ref[nb, hd] = v[rows, sl]
            else:
                kc_ref[nb, fresh_slot, hd] = k[rows, sl]
                vc_ref[nb, fresh_slot, hd] = v[rows, sl]
        if fresh_slot is not None:
            other = jnp.zeros((N_HEADS, SEQ, HEAD_DIM), F32)
            kc_ref[nb, 1 - fresh_slot] = other
            vc_ref[nb, 1 - fresh_slot] = other
    qb = (qkv[:, :D] * Q_SCALE).astype(BF16)
    kb = k.astype(BF16)
    vb = v.astype(BF16)
    first = lax.broadcasted_iota(jnp.int32, (1, 2 * HEAD_DIM), 1) < HEAD_DIM
    zero = jnp.zeros((), BF16)
    seqs = []
    for nb in range(CTX_NB):
        rows = slice(nb * SEQ, (nb + 1) * SEQ)
        outs = []
        for p in range(N_PAIRS):
            sl = slice(p * 2 * HEAD_DIM, (p + 1) * 2 * HEAD_DIM)
            kp = kb[rows, sl]
            vp = vb[rows, sl]
            k2 = jnp.concatenate([jnp.where(first, kp, zero), jnp.where(first, zero, kp)], axis=0)
            v2 = jnp.concatenate([jnp.where(first, vp, zero), jnp.where(first, zero, vp)], axis=0)
            s = _dot_nt(qb[rows, sl], k2)
            probs = []
            for hh in range(2):
                sh = s[:, hh * SEQ:(hh + 1) * SEQ]
                e = jnp.exp2(sh - jnp.max(sh, axis=-1, keepdims=True))
                probs.append(e / jnp.sum(e, axis=-1, keepdims=True))
            outs.append(_dot(jnp.concatenate(probs, axis=1).astype(BF16), v2))
        seqs.append(jnp.concatenate(outs, axis=1).astype(BF16))
    y = _dot(jnp.concatenate(seqs, axis=0), wout_ref[...])
    o_ref[...] = x + mod_ref[5:6, :] * _postnorm(y, gpost_ref[...])


def _context_attention(x, mods, pre, post, w_qkv, w_out, layer, caches):
    j = layer // 2
    rows = CTX_NB * SEQ
    cache = jax.ShapeDtypeStruct((BATCH, DEPTH // 2, N_HEADS, SEQ, HEAD_DIM), F32)
    if caches:
        cache_spec = pl.BlockSpec((CTX_NB, None, N_HEADS, SEQ, HEAD_DIM), lambda b: (b, j, 0, 0, 0))
    else:
        cache_spec = pl.BlockSpec((CTX_NB, DEPTH // 2, N_HEADS, SEQ, HEAD_DIM),
                                  lambda b: (b, 0, 0, 0, 0))
    in_specs = [
        pl.BlockSpec((rows, D), lambda b: (b, 0)),
        _mod_spec(layer, lambda b: 0),
        _const_spec((1, D), layer, 1),
        _const_spec((1, D), layer, 1),
        _const_spec((D, 3 * D), j),
        _const_spec((D, D), j),
    ]
    aliases = {0: 0}
    if caches:
        in_specs += [pl.BlockSpec(memory_space=pl.ANY)] * 2
        aliases.update({6: 1, 7: 2})
    return pl.pallas_call(
        functools.partial(_ctx_attn_body, fresh_slot=None if caches else j),
        grid=(BATCH // CTX_NB,),
        in_specs=in_specs,
        out_specs=[pl.BlockSpec((rows, D), lambda b: (b, 0)), cache_spec, cache_spec],
        out_shape=[jax.ShapeDtypeStruct((T_ALL, D), F32), cache, cache],
        input_output_aliases=aliases,
        compiler_params=_cparams(1),
        name="context_attention",
    )(x, mods, pre, post, w_qkv, w_out, *caches)


def _qkv_lat_body(x_ref, mod_ref, gpre_ref, wq_ref, wkt_ref, wv_ref, q_ref, kt_ref, v_ref):
    h = _prenorm(x_ref[...], gpre_ref[...], mod_ref, 1)
    q_ref[...] = (_dot(h, wq_ref[...]) * Q_SCALE).astype(BF16)
    v_ref[...] = _dot(h, wv_ref[...]).astype(BF16)
    kt_ref[...] = _dot_nt(wkt_ref[...], h).astype(BF16)


def _qkv_latent(x, mods, pre, w_qkv, wkt, layer):
    j = layer // 2
    per_lat = LAT // QKV_TM
    first = GROUP_ROWS // QKV_TM
    tok = jax.ShapeDtypeStruct((N_LAT * LAT, D), BF16)

    def w_cols(col_block):
        return pl.BlockSpec((None, D, D), lambda i: (j, 0, col_block), pipeline_mode=pl.Buffered(1))

    return pl.pallas_call(
        _qkv_lat_body,
        grid=(N_LAT * per_lat,),
        in_specs=[
            pl.BlockSpec((QKV_TM, D), lambda i: (first + i, 0)),
            _mod_spec(layer, lambda i: 1 + i // per_lat),
            _const_spec((1, D), layer, 1),
            w_cols(0),
            _const_spec((D, D), j),
            w_cols(2),
        ],
        out_specs=[
            pl.BlockSpec((QKV_TM, D), lambda i: (i, 0)),
            pl.BlockSpec((None, D, QKV_TM), lambda i: (i // per_lat, 0, i % per_lat)),
            pl.BlockSpec((QKV_TM, D), lambda i: (i, 0)),
        ],
        out_shape=[tok, jax.ShapeDtypeStruct((N_LAT, D, LAT), BF16), tok],
        compiler_params=_cparams(1),
        name="qkv_latent",
    )(x, mods, pre, w_qkv, wkt, w_qkv)


def _na_block_geometry(blk):
    return min(max(NA_ROWS * blk - WIN_ROWS // 2, 0), GRID - NA_KROWS)


def _na_variant(blk):
    return 0 if blk == 0 else (2 if blk == NA_BLOCKS - 1 else 1)


def _na_tables():
    t = np.arange(NA_DT)[:, None]
    e = np.arange(2)[None, :]
    drow = t - 11 + e
    row_ok = (drow >= -(WIN_ROWS - 1)) & (drow <= WIN_ROWS - 1)
    c = np.arange(GRID)[:, None]
    cp = np.arange(GRID)[None, :]
    cs = np.clip(c - WIN_COLS // 2, 0, GRID - WIN_COLS)
    col_ok = (cp >= cs) & (cp < cs + WIN_COLS)
    ok = row_ok[:, None, :, None] & col_ok[None, :, None, :]
    rowmask = np.zeros((NA_MASK_ROWS, NA_K), np.float32)
    delta0 = np.zeros((NA_VARIANTS, NA_ROWS, NA_KROWS // 2), np.int64)
    seen = set()
    for blk in range(NA_BLOCKS):
        v = _na_variant(blk)
        u0 = _na_block_geometry(blk)
        for rl in range(NA_ROWS):
            r = NA_ROWS * blk + rl
            rs = min(max(r - WIN_ROWS // 2, 0), GRID - WIN_ROWS)
            kr = u0 + np.arange(NA_KROWS)
            valid = (kr >= rs) & (kr < rs + WIN_ROWS)
            mask = np.repeat(np.where(valid, 0.0, NEG), GRID).astype(np.float32)
            d0 = u0 - r + 2 * np.arange(NA_KROWS // 2) + 11
            if (v, rl) in seen:
                assert (rowmask[v * NA_ROWS + rl] == mask).all() and (delta0[v, rl] == d0).all()
            seen.add((v, rl))
            rowmask[v * NA_ROWS + rl] = mask
            delta0[v, rl] = d0
    assert delta0.min() >= 0 and delta0.max() < NA_DT
    pairmask = np.where(ok, 0.0, NEG).astype(np.float32).reshape(NA_DT, GRID, 2 * GRID)
    return pairmask, rowmask, delta0


def _na_bias_rows(rpb):
    nrow = 2 * WIN_ROWS - 1
    fill = jnp.full((N_HEADS, nrow, 2 * GRID - (2 * WIN_COLS - 1)), NEG, F32)
    p = jnp.concatenate([rpb[:, :, WIN_COLS - 1:] * LOG2E, fill, rpb[:, :, :WIN_COLS - 1] * LOG2E],
                        axis=-1)
    return jnp.concatenate([p, jnp.full((N_HEADS, 1, 2 * GRID), NEG, F32)], axis=1)


def _na_expand_bias(p_ref, pm_ref, rm_ref, pair_scr, bias_scr, delta0):
    nrow = 2 * WIN_ROWS - 1
    left = lax.broadcasted_iota(jnp.int32, (1, 2 * GRID), 1) < GRID
    for hh in range(2):
        toep = [pltpu.roll(jnp.broadcast_to(p_ref[hh, i:i + 1, :], (GRID, 2 * GRID)), 0, 1,
                           stride=1, stride_axis=0) for i in range(nrow + 1)]
        shifted = [pltpu.roll(t, GRID, 1) for t in toep]
        for t in range(NA_DT):
            i0, i1 = (d + WIN_ROWS - 1 if abs(d) < WIN_ROWS else nrow for d in (t - 11, t - 10))
            pair_scr[hh, t] = jnp.where(left, toep[i0], shifted[i1]) + pm_ref[t]
        for var in range(NA_VARIANTS):
            for rl in range(NA_ROWS):
                strip = jnp.concatenate(
                    [pair_scr[hh, int(delta0[var, rl, i])] for i in range(NA_KROWS // 2)], axis=1)
                row = var * NA_ROWS + rl
                bias_scr[hh, var, rl * GRID:(rl + 1) * GRID, :] = strip + rm_ref[row:row + 1, :]


def _natten_body(q_ref, kt_ref, v_ref, kct_ref, vc_ref, p_ref, pm_ref, rm_ref, o_ref,
                 pair_scr, bias_scr, s_scr, p_scr, *, delta0):
    _na_expand_bias(p_ref, pm_ref, rm_ref, pair_scr, bias_scr, delta0)

    first = lax.broadcasted_iota(jnp.int32, (1, 2 * HEAD_DIM), 1) < HEAD_DIM
    zero = jnp.zeros((), BF16)
    kct = kct_ref[...]
    vc = vc_ref[...]

    keeps = (first, jnp.logical_not(first))

    def offsets(blk):
        koff = pl.multiple_of(jnp.clip(blk - 1, 0, NA_BLOCKS - 3) * NA_Q, NA_Q)
        qoff = pl.multiple_of(blk * NA_Q, NA_Q)
        return koff, qoff

    def logits(blk, slot):
        koff, qoff = offsets(blk)
        var = jnp.where(blk == 0, 0, jnp.where(blk == NA_BLOCKS - 1, 2, 1))
        qb = q_ref[pl.ds(qoff, NA_Q), :]
        ktb = kt_ref[:, pl.ds(koff, NA_K)]
        for hh in range(2):
            qh = jnp.where(keeps[hh], qb, zero)
            s_scr[slot, hh, :, :NA_K] = _dot(qh, ktb) + bias_scr[hh, var]
            s_scr[slot, hh, :, NA_K:] = _dot(qh, kct)

    def softmax(slot):
        for hh in range(2):
            s = s_scr[slot, hh]
            p_scr[slot, hh] = jnp.exp2(s - jnp.max(s, axis=-1, keepdims=True)).astype(BF16)

    lane_id = lax.broadcasted_iota(jnp.int32, (1, 2 * HEAD_DIM), 1)

    def values(blk, slot):
        koff, qoff = offsets(blk)
        vb = v_ref[pl.ds(koff, NA_K), :]
        o = jnp.zeros((NA_Q, 2 * HEAD_DIM), F32)
        for hh in range(2):
            den_lane = HEAD_DIM if hh == 0 else 0
            ones_col = jnp.where(lane_id == den_lane, 1.0, 0.0).astype(BF16)
            oh = (_dot(p_scr[slot, hh, :, :NA_K], jnp.where(keeps[hh], vb, ones_col))
                  + _dot(p_scr[slot, hh, :, NA_K:], jnp.where(keeps[hh], vc, ones_col)))
            o = o + jnp.where(keeps[hh], oh / oh[:, den_lane:den_lane + 1], 0.0)
        o_ref[pl.ds(qoff, NA_Q), :] = o.astype(BF16)

    logits(0, 0)
    softmax(0)
    logits(1, 1)

    def steady(it, carry):
        t = 2 + 2 * it
        logits(t, 0)
        values(t - 2, 0)
        softmax(1)
        logits(t + 1, 1)
        values(t - 1, 1)
        softmax(0)
        return carry

    lax.fori_loop(0, (NA_BLOCKS - 2) // 2, steady, 0)
    values(NA_BLOCKS - 2, 0)
    softmax(1)
    values(NA_BLOCKS - 1, 1)


def _neighbourhood_attention(q, kt, v, kct, vc, bias_rows, pairmask, rowmask, delta0):
    lane = 2 * HEAD_DIM
    return pl.pallas_call(
        functools.partial(_natten_body, delta0=delta0),
        grid=(N_LAT, N_PAIRS),
        in_specs=[
            pl.BlockSpec((LAT, lane), lambda b, p: (b, p)),
            pl.BlockSpec((None, lane, LAT), lambda b, p: (b, p, 0)),
            pl.BlockSpec((LAT, lane), lambda b, p: (b, p)),
            pl.BlockSpec((None, lane, PAST), lambda b, p: (b, p, 0)),
            pl.BlockSpec((None, PAST, lane), lambda b, p: (b, 0, p)),
            pl.BlockSpec((2, 2 * WIN_ROWS, lane), lambda b, p: (p, 0, 0)),
            pl.BlockSpec((NA_DT, GRID, lane), lambda b, p: (0, 0, 0)),
            pl.BlockSpec((NA_MASK_ROWS, NA_K), lambda b, p: (0, 0)),
        ],
        out_specs=pl.BlockSpec((LAT, lane), lambda b, p: (b, p)),
        out_shape=jax.ShapeDtypeStruct((N_LAT * LAT, D), BF16),
        scratch_shapes=[
            pltpu.VMEM((2, NA_DT, GRID, lane), F32),
            pltpu.VMEM((2, NA_VARIANTS, NA_Q, NA_K), F32),
            pltpu.VMEM((2, 2, NA_Q, NA_K + PAST), F32),
            pltpu.VMEM((2, 2, NA_Q, NA_K + PAST), BF16),
        ],
        compiler_params=_cparams(2),
        name="neighbourhood_attention",
    )(q, kt, v, kct, vc, bias_rows, pairmask, rowmask)


def kernel(x_prompt, x_sample, cache_k, cache_v, c, c_ctx, w_mod, b_mod, norm_pre, norm_post,
           ffn_w1, ffn_w2, four_w_in, four_w_out, na_w_qkv, na_w_out, na_rpb):
    conds = jnp.concatenate([c_ctx[None, :], c, jnp.zeros((8 - 1 - N_LAT, D), F32)], axis=0)
    mods = _modulation(conds, w_mod, b_mod)[:, :N_GROUPS].reshape(DEPTH, N_GROUPS, N_MOD, D)

    chan, dft_prompt, m1, m2, tw = _fourier_constants()
    pairmask, rowmask, delta0 = _na_tables()
    pairmask = jnp.asarray(pairmask)
    rowmask = jnp.asarray(rowmask)

    w_in = four_w_in.astype(BF16)
    w_qkv = na_w_qkv.astype(BF16)
    w_out = na_w_out.astype(BF16)
    wkt = w_qkv[:, :, D:2 * D].transpose(0, 2, 1)
    pre = norm_pre[:, :, None, :]
    post = norm_post[:, :, None, :]

    n_ctx = BATCH * SEQ
    n_lat = N_LAT * LAT
    ffn = functools.partial(_half_ffn, mods=mods, pre=pre, post=post)
    w1b = ffn_w1[0, 0].astype(BF16)
    w2b = ffn_w2[0, 0].astype(BF16)
    caches = ()
    for i in range(DEPTH):
        j = i // 2
        nxt = (ffn_w1, ffn_w2, i, 1)
        if i == 0:
            x, w1b, w2b = ffn(x_prompt.reshape(n_ctx, D), w1b, w2b, layer=0, slot=0,
                              x_b=x_sample.reshape(n_lat, D), convert=nxt)
        else:
            x, w1b, w2b = ffn(x, w1b, w2b, layer=i, slot=0, convert=nxt)
        attn = None
        if i % 2 == 0:
            wf = _fold_channel_dft(chan, four_w_out[j])
            x = _fourier_prompt(x, mods, pre, post, w_in, dft_prompt, wf, i)
            x4 = x.reshape(N_GROUPS, GRID, GRID, D)
            y5 = _fourier_latent_stage1(x4, mods, pre, w_in, m1, tw, i)
            x4 = _fourier_latent_stage2(y5, x4, mods, post, m2, wf, i)
            x = x4.reshape(T_ALL, D)
        else:
            x, *caches = _context_attention(x, mods, pre, post, w_qkv, w_out, i, caches)
            q, kt, v = _qkv_latent(x, mods, pre, w_qkv, wkt, i)
            kct = cache_k[:, j].transpose(0, 1, 3, 2).reshape(N_LAT, D, PAST).astype(BF16)
            vct = cache_v[:, j].transpose(0, 2, 1, 3).reshape(N_LAT, PAST, D).astype(BF16)
            o = _neighbourhood_attention(q, kt, v, kct, vct, _na_bias_rows(na_rpb[j]), pairmask,
                                         rowmask, delta0)
            attn = (o, w_out)
        if i < DEPTH - 1:
            x, w1b, w2b = ffn(x, w1b, w2b, layer=i, slot=1, attn=attn,
                              convert=(ffn_w1, ffn_w2, i + 1, 0))

    last = DEPTH - 1
    y_prompt, _, _ = ffn(x, w1b, w2b, layer=last, slot=1, n_rows=n_ctx, dst_rows=n_ctx)
    y_sample, _, _ = ffn(x, w1b, w2b, layer=last, slot=1, first_row=n_ctx, n_rows=n_lat,
                         dst_rows=n_lat, dst_row=0, attn=attn)
    return (y_prompt.reshape(BATCH, SEQ, D), y_sample.reshape(N_LAT, LAT, D), caches[0], caches[1])
```

```python
import functools

import numpy as np
import jax
import jax.numpy as jnp
from jax import lax
from jax.experimental import pallas as pl
from jax.experimental.pallas import tpu as pltpu

F32 = jnp.float32
BF16 = jnp.bfloat16

D = 1024
D_FF = 2816
DEPTH = 4
N_MOD = 9
N_HEADS = 16
HEAD_DIM = 64
N_PAIRS = N_HEADS // 2
SEQ = 256
BATCH = 16
GRID = 64
LAT = GRID * GRID
N_LAT = 2
GROUP_ROWS = 4096
N_GROUPS = 3
T_ALL = N_GROUPS * GROUP_ROWS
PAST = 256
WIN_ROWS = 8
WIN_COLS = 16
N_FG = 8
FG_DIM = D // N_FG
EPS = 1e-6
NEG = -1e30
LOG2E = 1.4426950408889634
Q_SCALE = HEAD_DIM ** -0.5 * LOG2E

VMEM_LIMIT = 56 * 1024 * 1024

FFN_TM = 512
FFN_SUB = 256
FFN_NC = 1
FFN_CONV_STEPS = 16
MOD_TN = 2304
QKV_TM = 1024
FP_NB = 4
CTX_NB = 1
FS_TN = 16
FS_SUB = 8
NA_ROWS = 4
NA_Q = NA_ROWS * GRID
NA_KROWS = 12
NA_K = NA_KROWS * GRID
NA_BLOCKS = GRID // NA_ROWS
NA_VARIANTS = 3
NA_MASK_ROWS = 16
NA_DT = 22


def _cparams(n_axes):
    return pltpu.CompilerParams(
        dimension_semantics=("arbitrary",) * n_axes, vmem_limit_bytes=VMEM_LIMIT)


def _const_spec(shape, *lead):
    nd = len(shape)
    return pl.BlockSpec((None,) * len(lead) + tuple(shape), lambda *_: tuple(lead) + (0,) * nd,
                        pipeline_mode=pl.Buffered(1))


def _mod_spec(layer, group_of):
    return pl.BlockSpec((None, None, N_MOD, D), lambda *ids: (layer, group_of(*ids), 0, 0))


def _silu(x):
    return x * jax.nn.sigmoid(x)


def _prenorm(x, g, mod_ref, k):
    shift = mod_ref[3 * k:3 * k + 1, :]
    scale = mod_ref[3 * k + 1:3 * k + 2, :]
    y = x * lax.rsqrt(jnp.mean(x * x, axis=-1, keepdims=True) + EPS)
    return ((y * g) * (1.0 + scale) + shift).astype(BF16)


def _postnorm(y, g):
    return (y * lax.rsqrt(jnp.mean(y * y, axis=-1, keepdims=True) + EPS)) * g


def _dot(a, b):
    return jnp.dot(a, b, preferred_element_type=F32)


def _dot_nt(a, b):
    return lax.dot_general(a, b, (((1,), (1,)), ((), ())), preferred_element_type=F32)


def _mod_body(cond_ref, w_ref, b_ref, o_ref):
    s = _silu(cond_ref[...]).astype(BF16)
    o_ref[...] = _dot(s, w_ref[...].astype(BF16)) + b_ref[...]


def _modulation(conds, w_mod, b_mod):
    n = N_MOD * D
    return pl.pallas_call(
        _mod_body,
        grid=(DEPTH, n // MOD_TN),
        in_specs=[
            pl.BlockSpec((8, D), lambda l, j: (0, 0)),
            pl.BlockSpec((None, D, MOD_TN), lambda l, j: (l, 0, j)),
            pl.BlockSpec((None, 1, MOD_TN), lambda l, j: (l, 0, j)),
        ],
        out_specs=pl.BlockSpec((None, 8, MOD_TN), lambda l, j: (l, 0, j)),
        out_shape=jax.ShapeDtypeStruct((DEPTH, 8, n), F32),
        compiler_params=_cparams(2),
        name="modulation",
    )(conds, w_mod, b_mod.reshape(DEPTH, 1, n))


def _ffn_tile(load_x, mod_ref, gpre_ref, gpost_ref, w1_ref, w2_ref, o_ref, k):
    ck = D_FF // FFN_NC
    gate = mod_ref[3 * k + 2:3 * k + 3, :]
    for r in range(FFN_TM // FFN_SUB):
        rows = slice(r * FFN_SUB, (r + 1) * FFN_SUB)
        x = load_x(rows)
        h = _prenorm(x, gpre_ref[...], mod_ref, k)
        y = None
        for c in range(FFN_NC):
            g = _dot(h, w1_ref[:, c * ck:(c + 1) * ck])
            u = _dot(h, w1_ref[:, D_FF + c * ck:D_FF + (c + 1) * ck])
            a = (_silu(g) * u).astype(BF16)
            yc = _dot(a, w2_ref[c * ck:(c + 1) * ck, :])
            y = yc if y is None else y + yc
        o_ref[rows, :] = x + (0.5 * gate) * _postnorm(y, gpost_ref[...])


def _ffn_body(*refs, k, tile0, tiles_a, attn, convert):
    refs = list(refs)
    x_ref = refs.pop(0)
    xb_ref = refs.pop(0) if tiles_a is not None else None
    ao_ref = refs.pop(0) if attn else None
    mod_ref, gpre_ref, gpost_ref, w1_ref, w2_ref = refs[:5]
    refs = refs[5:]
    if attn:
        gmix_ref, wout_ref = refs[:2]
        refs = refs[2:]
    if convert:
        nw1_ref, nw2_ref, o_ref, nw1b_ref, nw2b_ref = refs
        nw1b_ref[...] = nw1_ref[...].astype(BF16)
        nw2b_ref[...] = nw2_ref[...].astype(BF16)
    else:
        (o_ref,) = refs
    tile = tile0 + pl.program_id(0)
    tail = (mod_ref, gpre_ref, gpost_ref, w1_ref, w2_ref, o_ref, k)

    def plain(rows):
        if xb_ref is None:
            return x_ref[rows, :]
        return jnp.where(tile < tiles_a, x_ref[rows, :], xb_ref[rows, :])

    def mixed(rows):
        y = _dot(ao_ref[rows, :], wout_ref[...])
        return x_ref[rows, :] + mod_ref[5:6, :] * _postnorm(y, gmix_ref[...])

    ctx_tiles = GROUP_ROWS // FFN_TM
    if not attn:
        _ffn_tile(plain, *tail)
    elif tile0 >= ctx_tiles:
        _ffn_tile(mixed, *tail)
    else:
        pl.when(tile < ctx_tiles)(lambda: _ffn_tile(plain, *tail))
        pl.when(tile >= ctx_tiles)(lambda: _ffn_tile(mixed, *tail))


def _half_ffn(x, w1b, w2b, mods, pre, post, layer, slot, *, x_b=None, first_row=0, n_rows=T_ALL,
              dst_rows=T_ALL, dst_row=None, attn=None, convert=None):
    k = 2 * slot
    tiles_per_group = GROUP_ROWS // FFN_TM
    n_tiles = n_rows // FFN_TM
    t0 = first_row // FFN_TM
    d0 = t0 if dst_row is None else dst_row // FFN_TM
    tiles_a = None
    in_specs = [pl.BlockSpec((FFN_TM, D), lambda i: (t0 + i, 0))]
    args = [x]
    if x_b is not None:
        tiles_a = x.shape[0] // FFN_TM
        in_specs = [pl.BlockSpec((FFN_TM, D), lambda i: (jnp.minimum(i, tiles_a - 1), 0)),
                    pl.BlockSpec((FFN_TM, D), lambda i: (jnp.maximum(i - tiles_a, 0), 0))]
        args.append(x_b)
    if attn:
        in_specs.append(pl.BlockSpec(
            (FFN_TM, D), lambda i: (jnp.maximum(t0 + i - tiles_per_group, 0), 0)))
        args.append(attn[0])
    in_specs += [
        _mod_spec(layer, lambda i: (t0 + i) // tiles_per_group),
        _const_spec((1, D), layer, k),
        _const_spec((1, D), layer, k),
        _const_spec((D, 2 * D_FF)),
        _const_spec((D_FF, D)),
    ]
    args += [mods, pre, post, w1b, w2b]
    if attn:
        in_specs += [_const_spec((1, D), layer, 1), _const_spec((D, D), layer // 2)]
        args += [post, attn[1]]
    out_specs = [pl.BlockSpec((FFN_TM, D), lambda i: (d0 + i, 0))]
    out_shape = [jax.ShapeDtypeStruct((dst_rows, D), F32)]
    if convert:
        nw1, nw2, nl, ns = convert
        assert n_tiles >= FFN_CONV_STEPS
        r1, r2 = D // FFN_CONV_STEPS, D_FF // FFN_CONV_STEPS
        last = FFN_CONV_STEPS - 1
        in_specs += [
            pl.BlockSpec((None, None, r1, 2 * D_FF), lambda i: (nl, ns, jnp.minimum(i, last), 0)),
            pl.BlockSpec((None, None, r2, D), lambda i: (nl, ns, jnp.minimum(i, last), 0)),
        ]
        args += [nw1, nw2]
        out_specs += [pl.BlockSpec((r1, 2 * D_FF), lambda i: (jnp.minimum(i, last), 0)),
                      pl.BlockSpec((r2, D), lambda i: (jnp.minimum(i, last), 0))]
        out_shape += [jax.ShapeDtypeStruct((D, 2 * D_FF), BF16),
                      jax.ShapeDtypeStruct((D_FF, D), BF16)]
    res = pl.pallas_call(
        functools.partial(_ffn_body, k=k, tile0=t0, tiles_a=tiles_a, attn=bool(attn),
                          convert=bool(convert)),
        grid=(n_tiles,),
        in_specs=in_specs,
        out_specs=out_specs,
        out_shape=out_shape,
        compiler_params=_cparams(1),
        name="half_ffn",
    )(*args)
    return (res[0], res[1], res[2]) if convert else (res[0], None, None)


def _dft_cos_sin(n):
    idx = np.arange(n)
    ang = 2.0 * np.pi * ((idx[:, None] * idx[None, :]) % n) / n
    return np.cos(ang), np.sin(ang)


def _fourier_constants():
    c128, s128 = _dft_cos_sin(FG_DIM)
    chan = np.stack([c128, s128]) / np.sqrt(FG_DIM)

    c256, s256 = _dft_cos_sin(SEQ)
    dft_prompt = np.concatenate([c256, -s256], axis=0) / np.sqrt(SEQ)

    c64, s64 = _dft_cos_sin(GRID)
    c64 = c64 / np.sqrt(GRID)
    s64 = s64 / np.sqrt(GRID)
    eye = np.eye(FS_SUB)
    rows = GRID * FS_SUB
    m1 = np.concatenate([np.kron(c64, eye), np.kron(-s64, eye)], axis=0)
    def spread(f):
        return np.einsum("kn,jl->kjln", f, eye).reshape(rows, rows)
    m2 = np.block([[spread(c64), spread(s64)], [spread(-s64), spread(c64)]])
    k1 = np.arange(GRID)[:, None]
    n2 = np.arange(GRID)[None, :]
    ang = 2.0 * np.pi * (k1 * n2) / LAT
    tw = np.stack([np.cos(ang), np.sin(ang)])[..., None] * np.ones((1, 1, 1, 128))

    def mxu_const(a):
        return jnp.asarray(a, F32).astype(BF16)

    return (jnp.asarray(chan, F32), mxu_const(dft_prompt), mxu_const(m1), mxu_const(m2),
            jnp.asarray(tw, F32))


def _fold_body(cs_ref, w_ref, o_ref):
    w = w_ref[...]
    for t in range(2):
        o_ref[t] = jnp.dot(cs_ref[t], w, preferred_element_type=F32,
                           precision=lax.Precision.HIGHEST).astype(BF16)


def _fold_channel_dft(chan, w_out):
    return pl.pallas_call(
        _fold_body,
        grid=(N_FG,),
        in_specs=[
            pl.BlockSpec((2, FG_DIM, FG_DIM), lambda g: (0, 0, 0)),
            pl.BlockSpec((FG_DIM, D), lambda g: (g, 0)),
        ],
        out_specs=pl.BlockSpec((2, FG_DIM, D), lambda g: (0, g, 0)),
        out_shape=jax.ShapeDtypeStruct((2, D, D), BF16),
        compiler_params=_cparams(1),
        name="fold_channel_dft",
    )(chan, w_out)


def _four_prompt_body(x_ref, mod_ref, gpre_ref, gpost_ref, win_ref, dft_ref, w2_ref, o_ref):
    x = x_ref[...]
    h = _prenorm(x, gpre_ref[...], mod_ref, 1)
    u = _dot(h, win_ref[...]).astype(BF16)
    p = [_dot(dft_ref[...], u[nb * SEQ:(nb + 1) * SEQ]) for nb in range(FP_NB)]
    re = jnp.concatenate([t[:SEQ] for t in p], axis=0).astype(BF16)
    im = jnp.concatenate([t[SEQ:] for t in p], axis=0).astype(BF16)
    y = _dot(re, w2_ref[0]) + _dot(im, w2_ref[1])
    o_ref[...] = x + mod_ref[5:6, :] * _postnorm(y, gpost_ref[...])


def _fourier_prompt(x, mods, pre, post, w_in, dft_prompt, w2, layer):
    return pl.pallas_call(
        _four_prompt_body,
        grid=(BATCH // FP_NB,),
        in_specs=[
            pl.BlockSpec((FP_NB * SEQ, D), lambda b: (b, 0)),
            _mod_spec(layer, lambda b: 0),
            _const_spec((1, D), layer, 1),
            _const_spec((1, D), layer, 1),
            _const_spec((D, D), layer // 2),
            _const_spec((2 * SEQ, SEQ)),
            _const_spec((2, D, D)),
        ],
        out_specs=pl.BlockSpec((FP_NB * SEQ, D), lambda b: (b, 0)),
        out_shape=jax.ShapeDtypeStruct((T_ALL, D), F32),
        input_output_aliases={0: 0},
        compiler_params=_cparams(1),
        name="fourier_prompt",
    )(x, mods, pre, post, w_in, dft_prompt, w2)


def _four_s1_body(x_ref, mod_ref, gpre_ref, win_ref, m1_ref, tw_ref, y_ref, y_scr):
    rows = GRID * FS_SUB
    for s in range(FS_TN // FS_SUB):
        cols = slice(s * FS_SUB, (s + 1) * FS_SUB)
        x = x_ref[:, cols, :].reshape(rows, D)
        h = _prenorm(x, gpre_ref[...], mod_ref, 1)
        u = _dot(h, win_ref[...]).astype(BF16)
        y = _dot(m1_ref[...], u)
        tc = tw_ref[0, :, cols, :].reshape(rows, 128)
        ts = tw_ref[1, :, cols, :].reshape(rows, 128)
        for l in range(D // 128):
            sl = slice(l * 128, (l + 1) * 128)
            yr = y[:rows, sl]
            yi = y[rows:, sl]
            y_scr[0, :, cols, sl] = (yr * tc + yi * ts).reshape(GRID, FS_SUB, 128)
            y_scr[1, :, cols, sl] = (yi * tc - yr * ts).reshape(GRID, FS_SUB, 128)
    y_ref[...] = y_scr[...].astype(BF16)


def _fourier_latent_stage1(x4, mods, pre, w_in, m1, tw, layer):
    rows = GRID * FS_SUB
    return pl.pallas_call(
        _four_s1_body,
        grid=(N_LAT, GRID // FS_TN),
        in_specs=[
            pl.BlockSpec((None, GRID, FS_TN, D), lambda b, j: (b + 1, 0, j, 0)),
            _mod_spec(layer, lambda b, j: b + 1),
            _const_spec((1, D), layer, 1),
            _const_spec((D, D), layer // 2),
            _const_spec((2 * rows, rows)),
            pl.BlockSpec((2, GRID, FS_TN, 128), lambda b, j: (0, 0, j, 0)),
        ],
        out_specs=pl.BlockSpec((None, 2, GRID, FS_TN, D), lambda b, j: (b, 0, 0, j, 0)),
        out_shape=jax.ShapeDtypeStruct((N_LAT, 2, GRID, GRID, D), BF16),
        scratch_shapes=[pltpu.VMEM((2, GRID, FS_TN, D), F32)],
        compiler_params=_cparams(2),
        name="fourier_latent_stage1",
    )(x4, mods, pre, w_in, m1, tw)


def _four_s2_body(y_ref, x_ref, mod_ref, gpost_ref, m2_ref, w2_ref, o_ref):
    rows = GRID * FS_SUB
    yin = y_ref[...].reshape(2 * rows, D)
    z = _dot(m2_ref[...], yin)
    y = _dot(z[:rows].astype(BF16), w2_ref[0]) + _dot(z[rows:].astype(BF16), w2_ref[1])
    x = x_ref[...].reshape(rows, D)
    out = x + mod_ref[5:6, :] * _postnorm(y, gpost_ref[...])
    o_ref[...] = out.reshape(GRID, FS_SUB, D)


def _fourier_latent_stage2(y5, x4, mods, post, m2, w2, layer):
    rows = GRID * FS_SUB
    return pl.pallas_call(
        _four_s2_body,
        grid=(N_LAT, GRID // FS_SUB),
        in_specs=[
            pl.BlockSpec((None, 2, FS_SUB, GRID, D), lambda b, j: (b, 0, j, 0, 0)),
            pl.BlockSpec((None, GRID, FS_SUB, D), lambda b, j: (b + 1, 0, j, 0)),
            _mod_spec(layer, lambda b, j: b + 1),
            _const_spec((1, D), layer, 1),
            _const_spec((2 * rows, 2 * rows)),
            _const_spec((2, D, D)),
        ],
        out_specs=pl.BlockSpec((None, GRID, FS_SUB, D), lambda b, j: (b + 1, 0, j, 0)),
        out_shape=jax.ShapeDtypeStruct((N_GROUPS, GRID, GRID, D), F32),
        input_output_aliases={1: 0},
        compiler_params=_cparams(2),
        name="fourier_latent_stage2",
    )(y5, x4, mods, post, m2, w2)


def _ctx_attn_body(x_ref, mod_ref, gpre_ref, gpost_ref, wqkv_ref, wout_ref, *rest, fresh_slot):
    o_ref, kc_ref, vc_ref = rest[-3:]
    x = x_ref[...]
    h = _prenorm(x, gpre_ref[...], mod_ref, 1)
    qkv = _dot(h, wqkv_ref[...])
    k = qkv[:, D:2 * D]
    v = qkv[:, 2 * D:]
    for nb in range(CTX_NB):
        rows = slice(nb * SEQ, (nb + 1) * SEQ)
        for hd in range(N_HEADS):
            sl = slice(hd * HEAD_DIM, (hd + 1) * HEAD_DIM)
            if fresh_slot is None:
                kc_ref[nb, hd] = k[rows, sl]
                vc_ref[nb, hd] = v[rows, sl]
            else:
                kc_ref[nb, fresh_slot, hd] = k[rows, sl]
                vc_ref[nb, fresh_slot, hd] = v[rows, sl]
        if fresh_slot is not None:
            other = jnp.zeros((N_HEADS, SEQ, HEAD_DIM), F32)
            kc_ref[nb, 1 - fresh_slot] = other
            vc_ref[nb, 1 - fresh_slot] = other
    qb = (qkv[:, :D] * Q_SCALE).astype(BF16)
    kb = k.astype(BF16)
    vb = v.astype(BF16)
    first = lax.broadcasted_iota(jnp.int32, (1, 2 * HEAD_DIM), 1) < HEAD_DIM
    zero = jnp.zeros((), BF16)
    seqs = []
    for nb in range(CTX_NB):
        rows = slice(nb * SEQ, (nb + 1) * SEQ)
        outs = []
        for p in range(N_PAIRS):
            sl = slice(p * 2 * HEAD_DIM, (p + 1) * 2 * HEAD_DIM)
            kp = kb[rows, sl]
            vp = vb[rows, sl]
            k2 = jnp.concatenate([jnp.where(first, kp, zero), jnp.where(first, zero, kp)], axis=0)
            v2 = jnp.concatenate([jnp.where(first, vp, zero), jnp.where(first, zero, vp)], axis=0)
            s = _dot_nt(qb[rows, sl], k2)
            probs = []
            for hh in range(2):
                sh = s[:, hh * SEQ:(hh + 1) * SEQ]
                e = jnp.exp2(sh - jnp.max(sh, axis=-1, keepdims=True))
                probs.append(e / jnp.sum(e, axis=-1, keepdims=True))
            outs.append(_dot(jnp.concatenate(probs, axis=1).astype(BF16), v2))
        seqs.append(jnp.concatenate(outs, axis=1).astype(BF16))
    y = _dot(jnp.concatenate(seqs, axis=0), wout_ref[...])
    o_ref[...] = x + mod_ref[5:6, :] * _postnorm(y, gpost_ref[...])


def _context_attention(x, mods, pre, post, w_qkv, w_out, layer, caches):
    j = layer // 2
    rows = CTX_NB * SEQ
    cache = jax.ShapeDtypeStruct((BATCH, DEPTH // 2, N_HEADS, SEQ, HEAD_DIM), F32)
    if caches:
        cache_spec = pl.BlockSpec((CTX_NB, None, N_HEADS, SEQ, HEAD_DIM), lambda b: (b, j, 0, 0, 0))
    else:
        cache_spec = pl.BlockSpec((CTX_NB, DEPTH // 2, N_HEADS, SEQ, HEAD_DIM),
                                  lambda b: (b, 0, 0, 0, 0))
    in_specs = [
        pl.BlockSpec((rows, D), lambda b: (b, 0)),
        _mod_spec(layer, lambda b: 0),
        _const_spec((1, D), layer, 1),
        _const_spec((1, D), layer, 1),
        _const_spec((D, 3 * D), j),
        _const_spec((D, D), j),
    ]
    aliases = {0: 0}
    if caches:
        in_specs += [pl.BlockSpec(memory_space=pl.ANY)] * 2
        aliases.update({6: 1, 7: 2})
    return pl.pallas_call(
        functools.partial(_ctx_attn_body, fresh_slot=None if caches else j),
        grid=(BATCH // CTX_NB,),
        in_specs=in_specs,
        out_specs=[pl.BlockSpec((rows, D), lambda b: (b, 0)), cache_spec, cache_spec],
        out_shape=[jax.ShapeDtypeStruct((T_ALL, D), F32), cache, cache],
        input_output_aliases=aliases,
        compiler_params=_cparams(1),
        name="context_attention",
    )(x, mods, pre, post, w_qkv, w_out, *caches)


def _qkv_lat_body(x_ref, mod_ref, gpre_ref, wq_ref, wkt_ref, wv_ref, q_ref, kt_ref, v_ref):
    h = _prenorm(x_ref[...], gpre_ref[...], mod_ref, 1)
    q_ref[...] = (_dot(h, wq_ref[...]) * Q_SCALE).astype(BF16)
    v_ref[...] = _dot(h, wv_ref[...]).astype(BF16)
    kt_ref[...] = _dot_nt(wkt_ref[...], h).astype(BF16)


def _qkv_latent(x, mods, pre, w_qkv, wkt, layer):
    j = layer // 2
    per_lat = LAT // QKV_TM
    first = GROUP_ROWS // QKV_TM
    tok = jax.ShapeDtypeStruct((N_LAT * LAT, D), BF16)

    def w_cols(col_block):
        return pl.BlockSpec((None, D, D), lambda i: (j, 0, col_block), pipeline_mode=pl.Buffered(1))

    return pl.pallas_call(
        _qkv_lat_body,
        grid=(N_LAT * per_lat,),
        in_specs=[
            pl.BlockSpec((QKV_TM, D), lambda i: (first + i, 0)),
            _mod_spec(layer, lambda i: 1 + i // per_lat),
            _const_spec((1, D), layer, 1),
            w_cols(0),
            _const_spec((D, D), j),
            w_cols(2),
        ],
        out_specs=[
            pl.BlockSpec((QKV_TM, D), lambda i: (i, 0)),
            pl.BlockSpec((None, D, QKV_TM), lambda i: (i // per_lat, 0, i % per_lat)),
            pl.BlockSpec((QKV_TM, D), lambda i: (i, 0)),
        ],
        out_shape=[tok, jax.ShapeDtypeStruct((N_LAT, D, LAT), BF16), tok],
        compiler_params=_cparams(1),
        name="qkv_latent",
    )(x, mods, pre, w_qkv, wkt, w_qkv)


def _na_block_geometry(blk):
    return min(max(NA_ROWS * blk - WIN_ROWS // 2, 0), GRID - NA_KROWS)


def _na_variant(blk):
    return 0 if blk == 0 else (2 if blk == NA_BLOCKS - 1 else 1)


def _na_tables():
    t = np.arange(NA_DT)[:, None]
    e = np.arange(2)[None, :]
    drow = t - 11 + e
    row_ok = (drow >= -(WIN_ROWS - 1)) & (drow <= WIN_ROWS - 1)
    c = np.arange(GRID)[:, None]
    cp = np.arange(GRID)[None, :]
    cs = np.clip(c - WIN_COLS // 2, 0, GRID - WIN_COLS)
    col_ok = (cp >= cs) & (cp < cs + WIN_COLS)
    ok = row_ok[:, None, :, None] & col_ok[None, :, None, :]
    rowmask = np.zeros((NA_MASK_ROWS, NA_K), np.float32)
    delta0 = np.zeros((NA_VARIANTS, NA_ROWS, NA_KROWS // 2), np.int64)
    seen = set()
    for blk in range(NA_BLOCKS):
        v = _na_variant(blk)
        u0 = _na_block_geometry(blk)
        for rl in range(NA_ROWS):
            r = NA_ROWS * blk + rl
            rs = min(max(r - WIN_ROWS // 2, 0), GRID - WIN_ROWS)
            kr = u0 + np.arange(NA_KROWS)
            valid = (kr >= rs) & (kr < rs + WIN_ROWS)
            mask = np.repeat(np.where(valid, 0.0, NEG), GRID).astype(np.float32)
            d0 = u0 - r + 2 * np.arange(NA_KROWS // 2) + 11
            if (v, rl) in seen:
                assert (rowmask[v * NA_ROWS + rl] == mask).all() and (delta0[v, rl] == d0).all()
            seen.add((v, rl))
            rowmask[v * NA_ROWS + rl] = mask
            delta0[v, rl] = d0
    assert delta0.min() >= 0 and delta0.max() < NA_DT
    pairmask = np.where(ok, 0.0, NEG).astype(np.float32).reshape(NA_DT, GRID, 2 * GRID)
    return pairmask, rowmask, delta0


def _na_bias_rows(rpb):
    nrow = 2 * WIN_ROWS - 1
    fill = jnp.full((N_HEADS, nrow, 2 * GRID - (2 * WIN_COLS - 1)), NEG, F32)
    p = jnp.concatenate([rpb[:, :, WIN_COLS - 1:] * LOG2E, fill, rpb[:, :, :WIN_COLS - 1] * LOG2E],
                        axis=-1)
    return jnp.concatenate([p, jnp.full((N_HEADS, 1, 2 * GRID), NEG, F32)], axis=1)


def _na_expand_bias(p_ref, pm_ref, rm_ref, pair_scr, bias_scr, delta0):
    nrow = 2 * WIN_ROWS - 1
    left = lax.broadcasted_iota(jnp.int32, (1, 2 * GRID), 1) < GRID
    for hh in range(2):
        toep = [pltpu.roll(jnp.broadcast_to(p_ref[hh, i:i + 1, :], (GRID, 2 * GRID)), 0, 1,
                           stride=1, stride_axis=0) for i in range(nrow + 1)]
        shifted = [pltpu.roll(t, GRID, 1) for t in toep]
        for t in range(NA_DT):
            i0, i1 = (d + WIN_ROWS - 1 if abs(d) < WIN_ROWS else nrow for d in (t - 11, t - 10))
            pair_scr[hh, t] = jnp.where(left, toep[i0], shifted[i1]) + pm_ref[t]
        for var in range(NA_VARIANTS):
            for rl in range(NA_ROWS):
                strip = jnp.concatenate(
                    [pair_scr[hh, int(delta0[var, rl, i])] for i in range(NA_KROWS // 2)], axis=1)
                row = var * NA_ROWS + rl
                bias_scr[hh, var, rl * GRID:(rl + 1) * GRID, :] = strip + rm_ref[row:row + 1, :]


def _natten_body(q_ref, kt_ref, v_ref, kct_ref, vc_ref, p_ref, pm_ref, rm_ref, o_ref,
                 pair_scr, bias_scr, s_scr, p_scr, *, delta0):
    _na_expand_bias(p_ref, pm_ref, rm_ref, pair_scr, bias_scr, delta0)

    first = lax.broadcasted_iota(jnp.int32, (1, 2 * HEAD_DIM), 1) < HEAD_DIM
    zero = jnp.zeros((), BF16)
    kct = kct_ref[...]
    vc = vc_ref[...]

    keeps = (first, jnp.logical_not(first))

    def offsets(blk):
        koff = pl.multiple_of(jnp.clip(blk - 1, 0, NA_BLOCKS - 3) * NA_Q, NA_Q)
        qoff = pl.multiple_of(blk * NA_Q, NA_Q)
        return koff, qoff

    def logits(blk, slot):
        koff, qoff = offsets(blk)
        var = jnp.where(blk == 0, 0, jnp.where(blk == NA_BLOCKS - 1, 2, 1))
        qb = q_ref[pl.ds(qoff, NA_Q), :]
        ktb = kt_ref[:, pl.ds(koff, NA_K)]
        for hh in range(2):
            qh = jnp.where(keeps[hh], qb, zero)
            s_scr[slot, hh, :, :NA_K] = _dot(qh, ktb) + bias_scr[hh, var]
            s_scr[slot, hh, :, NA_K:] = _dot(qh, kct)

    def softmax(slot):
        for hh in range(2):
            s = s_scr[slot, hh]
            p_scr[slot, hh] = jnp.exp2(s - jnp.max(s, axis=-1, keepdims=True)).astype(BF16)

    lane_id = lax.broadcasted_iota(jnp.int32, (1, 2 * HEAD_DIM), 1)

    def values(blk, slot):
        koff, qoff = offsets(blk)
        vb = v_ref[pl.ds(koff, NA_K), :]
        o = jnp.zeros((NA_Q, 2 * HEAD_DIM), F32)
        for hh in range(2):
            den_lane = HEAD_DIM if hh == 0 else 0
            ones_col = jnp.where(lane_id == den_lane, 1.0, 0.0).astype(BF16)
            oh = (_dot(p_scr[slot, hh, :, :NA_K], jnp.where(keeps[hh], vb, ones_col))
                  + _dot(p_scr[slot, hh, :, NA_K:], jnp.where(keeps[hh], vc, ones_col)))
            o = o + jnp.where(keeps[hh], oh / oh[:, den_lane:den_lane + 1], 0.0)
        o_ref[pl.ds(qoff, NA_Q), :] = o.astype(BF16)

    logits(0, 0)
    softmax(0)
    logits(1, 1)

    def steady(it, carry):
        t = 2 + 2 * it
        logits(t, 0)
        values(t - 2, 0)
        softmax(1)
        logits(t + 1, 1)
        values(t - 1, 1)
        softmax(0)
        return carry

    lax.fori_loop(0, (NA_BLOCKS - 2) // 2, steady, 0, unroll=True)
    values(NA_BLOCKS - 2, 0)
    softmax(1)
    values(NA_BLOCKS - 1, 1)


def _neighbourhood_attention(q, kt, v, kct, vc, bias_rows, pairmask, rowmask, delta0):
    lane = 2 * HEAD_DIM
    return pl.pallas_call(
        functools.partial(_natten_body, delta0=delta0),
        grid=(N_LAT, N_PAIRS),
        in_specs=[
            pl.BlockSpec((LAT, lane), lambda b, p: (b, p)),
            pl.BlockSpec((None, lane, LAT), lambda b, p: (b, p, 0)),
            pl.BlockSpec((LAT, lane), lambda b, p: (b, p)),
            pl.BlockSpec((None, lane, PAST), lambda b, p: (b, p, 0)),
            pl.BlockSpec((None, PAST, lane), lambda b, p: (b, 0, p)),
            pl.BlockSpec((2, 2 * WIN_ROWS, lane), lambda b, p: (p, 0, 0)),
            pl.BlockSpec((NA_DT, GRID, lane), lambda b, p: (0, 0, 0)),
            pl.BlockSpec((NA_MASK_ROWS, NA_K), lambda b, p: (0, 0)),
        ],
        out_specs=pl.BlockSpec((LAT, lane), lambda b, p: (b, p)),
        out_shape=jax.ShapeDtypeStruct((N_LAT * LAT, D), BF16),
        scratch_shapes=[
            pltpu.VMEM((2, NA_DT, GRID, lane), F32),
            pltpu.VMEM((2, NA_VARIANTS, NA_Q, NA_K), F32),
            pltpu.VMEM((2, 2, NA_Q, NA_K + PAST), F32),
            pltpu.VMEM((2, 2, NA_Q, NA_K + PAST), BF16),
        ],
        compiler_params=_cparams(2),
        name="neighbourhood_attention",
    )(q, kt, v, kct, vc, bias_rows, pairmask, rowmask)


def kernel(x_prompt, x_sample, cache_k, cache_v, c, c_ctx, w_mod, b_mod, norm_pre, norm_post,
           ffn_w1, ffn_w2, four_w_in, four_w_out, na_w_qkv, na_w_out, na_rpb):
    conds = jnp.concatenate([c_ctx[None, :], c, jnp.zeros((8 - 1 - N_LAT, D), F32)], axis=0)
    mods = _modulation(conds, w_mod, b_mod)[:, :N_GROUPS].reshape(DEPTH, N_GROUPS, N_MOD, D)

    chan, dft_prompt, m1, m2, tw = _fourier_constants()
    pairmask, rowmask, delta0 = _na_tables()
    pairmask = jnp.asarray(pairmask)
    rowmask = jnp.asarray(rowmask)

    w_in = four_w_in.astype(BF16)
    w_qkv = na_w_qkv.astype(BF16)
    w_out = na_w_out.astype(BF16)
    wkt = w_qkv[:, :, D:2 * D].transpose(0, 2, 1)
    pre = norm_pre[:, :, None, :]
    post = norm_post[:, :, None, :]

    n_ctx = BATCH * SEQ
    n_lat = N_LAT * LAT
    ffn = functools.partial(_half_ffn, mods=mods, pre=pre, post=post)
    w1b = ffn_w1[0, 0].astype(BF16)
    w2b = ffn_w2[0, 0].astype(BF16)
    caches = ()
    for i in range(DEPTH):
        j = i // 2
        nxt = (ffn_w1, ffn_w2, i, 1)
        if i == 0:
            x, w1b, w2b = ffn(x_prompt.reshape(n_ctx, D), w1b, w2b, layer=0, slot=0,
                              x_b=x_sample.reshape(n_lat, D), convert=nxt)
        else:
            x, w1b, w2b = ffn(x, w1b, w2b, layer=i, slot=0, convert=nxt)
        attn = None
        if i % 2 == 0:
            wf = _fold_channel_dft(chan, four_w_out[j])
            x = _fourier_prompt(x, mods, pre, post, w_in, dft_prompt, wf, i)
            x4 = x.reshape(N_GROUPS, GRID, GRID, D)
            y5 = _fourier_latent_stage1(x4, mods, pre, w_in, m1, tw, i)
            x4 = _fourier_latent_stage2(y5, x4, mods, post, m2, wf, i)
            x = x4.reshape(T_ALL, D)
        else:
            x, *caches = _context_attention(x, mods, pre, post, w_qkv, w_out, i, caches)
            q, kt, v = _qkv_latent(x, mods, pre, w_qkv, wkt, i)
            kct = cache_k[:, j].transpose(0, 1, 3, 2).reshape(N_LAT, D, PAST).astype(BF16)
            vct = cache_v[:, j].transpose(0, 2, 1, 3).reshape(N_LAT, PAST, D).astype(BF16)
            o = _neighbourhood_attention(q, kt, v, kct, vct, _na_bias_rows(na_rpb[j]), pairmask,
                                         rowmask, delta0)
            attn = (o, w_out)
        if i < DEPTH - 1:
            x, w1b, w2b = ffn(x, w1b, w2b, layer=i, slot=1, attn=attn,
                              convert=(ffn_w1, ffn_w2, i + 1, 0))

    last = DEPTH - 1
    y_prompt, _, _ = ffn(x, w1b, w2b, layer=last, slot=1, n_rows=n_ctx, dst_rows=n_ctx)
    y_sample, _, _ = ffn(x, w1b, w2b, layer=last, slot=1, first_row=n_ctx, n_rows=n_lat,
                         dst_rows=n_lat, dst_row=0, attn=attn)
    return (y_prompt.reshape(BATCH, SEQ, D), y_sample.reshape(N_LAT, LAT, D), caches[0], caches[1])
```

```python
import functools

import numpy as np
import jax
import jax.numpy as jnp
from jax import lax
from jax.experimental import pallas as pl
from jax.experimental.pallas import tpu as pltpu

F32 = jnp.float32
BF16 = jnp.bfloat16

D = 1024
D_FF = 2816
DEPTH = 4
N_MOD = 9
N_HEADS = 16
HEAD_DIM = 64
N_PAIRS = N_HEADS // 2
SEQ = 256
BATCH = 16
GRID = 64
LAT = GRID * GRID
N_LAT = 2
GROUP_ROWS = 4096
N_GROUPS = 3
T_ALL = N_GROUPS * GROUP_ROWS
PAST = 256
WIN_ROWS = 8
WIN_COLS = 16
N_FG = 8
FG_DIM = D // N_FG
EPS = 1e-6
NEG = -1e30
LOG2E = 1.4426950408889634
Q_SCALE = HEAD_DIM ** -0.5 * LOG2E

VMEM_LIMIT = 56 * 1024 * 1024

FFN_TM = 512
FFN_SUB = 256
FFN_NC = 1
FFN_CONV_STEPS = 16
MOD_TN = 2304
FOLD_ROWS = 512
QKV_TM = 1024
FP_NB = 4
CTX_NB = 1
FS_TN = 16
FS_SUB = 8
NA_ROWS = 4
NA_Q = NA_ROWS * GRID
NA_KROWS = 12
NA_K = NA_KROWS * GRID
NA_BLOCKS = GRID // NA_ROWS
NA_VARIANTS = 3
NA_MASK_ROWS = 16
NA_DT = 22


def _cparams(n_axes):
    return pltpu.CompilerParams(
        dimension_semantics=("arbitrary",) * n_axes, vmem_limit_bytes=VMEM_LIMIT)


def _const_spec(shape, *lead):
    nd = len(shape)
    return pl.BlockSpec((None,) * len(lead) + tuple(shape), lambda *_: tuple(lead) + (0,) * nd,
                        pipeline_mode=pl.Buffered(1))


def _mod_spec(layer, group_of):
    return pl.BlockSpec((None, None, N_MOD, D), lambda *ids: (layer, group_of(*ids), 0, 0))


def _silu(x):
    return x * jax.nn.sigmoid(x)


def _prenorm(x, g, mod_ref, k):
    shift = mod_ref[3 * k:3 * k + 1, :]
    scale = mod_ref[3 * k + 1:3 * k + 2, :]
    y = x * lax.rsqrt(jnp.mean(x * x, axis=-1, keepdims=True) + EPS)
    return ((y * g) * (1.0 + scale) + shift).astype(BF16)


def _postnorm(y, g):
    return (y * lax.rsqrt(jnp.mean(y * y, axis=-1, keepdims=True) + EPS)) * g


def _dot(a, b):
    return jnp.dot(a, b, preferred_element_type=F32)


def _dot_nt(a, b):
    return lax.dot_general(a, b, (((1,), (1,)), ((), ())), preferred_element_type=F32)


def _mod_body(cond_ref, w_ref, b_ref, o_ref):
    s = _silu(cond_ref[...]).astype(BF16)
    o_ref[...] = _dot(s, w_ref[...].astype(BF16)) + b_ref[...]


def _modulation(conds, w_mod, b_mod):
    n = N_MOD * D
    return pl.pallas_call(
        _mod_body,
        grid=(DEPTH, n // MOD_TN),
        in_specs=[
            pl.BlockSpec((8, D), lambda l, j: (0, 0)),
            pl.BlockSpec((None, D, MOD_TN), lambda l, j: (l, 0, j)),
            pl.BlockSpec((None, 1, MOD_TN), lambda l, j: (l, 0, j)),
        ],
        out_specs=pl.BlockSpec((None, 8, MOD_TN), lambda l, j: (l, 0, j)),
        out_shape=jax.ShapeDtypeStruct((DEPTH, 8, n), F32),
        compiler_params=_cparams(2),
        name="modulation",
    )(conds, w_mod, b_mod.reshape(DEPTH, 1, n))


def _ffn_tile(load_x, mod_ref, gpre_ref, gpost_ref, w1_ref, w2_ref, o_ref, k):
    ck = D_FF // FFN_NC
    gate = mod_ref[3 * k + 2:3 * k + 3, :]
    for r in range(FFN_TM // FFN_SUB):
        rows = slice(r * FFN_SUB, (r + 1) * FFN_SUB)
        x = load_x(rows)
        h = _prenorm(x, gpre_ref[...], mod_ref, k)
        y = None
        for c in range(FFN_NC):
            g = _dot(h, w1_ref[:, c * ck:(c + 1) * ck])
            u = _dot(h, w1_ref[:, D_FF + c * ck:D_FF + (c + 1) * ck])
            a = (_silu(g) * u).astype(BF16)
            yc = _dot(a, w2_ref[c * ck:(c + 1) * ck, :])
            y = yc if y is None else y + yc
        o_ref[rows, :] = x + (0.5 * gate) * _postnorm(y, gpost_ref[...])


def _ffn_body(*refs, k, tile0, tiles_a, attn, convert):
    refs = list(refs)
    x_ref = refs.pop(0)
    xb_ref = refs.pop(0) if tiles_a is not None else None
    ao_ref = refs.pop(0) if attn else None
    mod_ref, gpre_ref, gpost_ref, w1_ref, w2_ref = refs[:5]
    refs = refs[5:]
    if attn:
        gmix_ref, wout_ref = refs[:2]
        refs = refs[2:]
    if convert:
        nw1_ref, nw2_ref, o_ref, nw1b_ref, nw2b_ref = refs
        nw1b_ref[...] = nw1_ref[...].astype(BF16)
        nw2b_ref[...] = nw2_ref[...].astype(BF16)
    else:
        (o_ref,) = refs
    tile = tile0 + pl.program_id(0)
    tail = (mod_ref, gpre_ref, gpost_ref, w1_ref, w2_ref, o_ref, k)

    def plain(rows):
        if xb_ref is None:
            return x_ref[rows, :]
        return jnp.where(tile < tiles_a, x_ref[rows, :], xb_ref[rows, :])

    def mixed(rows):
        y = _dot(ao_ref[rows, :], wout_ref[...])
        return x_ref[rows, :] + mod_ref[5:6, :] * _postnorm(y, gmix_ref[...])

    ctx_tiles = GROUP_ROWS // FFN_TM
    if not attn:
        _ffn_tile(plain, *tail)
    elif tile0 >= ctx_tiles:
        _ffn_tile(mixed, *tail)
    else:
        pl.when(tile < ctx_tiles)(lambda: _ffn_tile(plain, *tail))
        pl.when(tile >= ctx_tiles)(lambda: _ffn_tile(mixed, *tail))


def _half_ffn(x, w1b, w2b, mods, pre, post, layer, slot, *, x_b=None, first_row=0, n_rows=T_ALL,
              dst_rows=T_ALL, dst_row=None, attn=None, convert=None):
    k = 2 * slot
    tiles_per_group = GROUP_ROWS // FFN_TM
    n_tiles = n_rows // FFN_TM
    t0 = first_row // FFN_TM
    d0 = t0 if dst_row is None else dst_row // FFN_TM
    tiles_a = None
    in_specs = [pl.BlockSpec((FFN_TM, D), lambda i: (t0 + i, 0))]
    args = [x]
    if x_b is not None:
        tiles_a = x.shape[0] // FFN_TM
        in_specs = [pl.BlockSpec((FFN_TM, D), lambda i: (jnp.minimum(i, tiles_a - 1), 0)),
                    pl.BlockSpec((FFN_TM, D), lambda i: (jnp.maximum(i - tiles_a, 0), 0))]
        args.append(x_b)
    if attn:
        in_specs.append(pl.BlockSpec(
            (FFN_TM, D), lambda i: (jnp.maximum(t0 + i - tiles_per_group, 0), 0)))
        args.append(attn[0])
    in_specs += [
        _mod_spec(layer, lambda i: (t0 + i) // tiles_per_group),
        _const_spec((1, D), layer, k),
        _const_spec((1, D), layer, k),
        _const_spec((D, 2 * D_FF)),
        _const_spec((D_FF, D)),
    ]
    args += [mods, pre, post, w1b, w2b]
    if attn:
        in_specs += [_const_spec((1, D), layer, 1), _const_spec((D, D), layer // 2)]
        args += [post, attn[1]]
    out_specs = [pl.BlockSpec((FFN_TM, D), lambda i: (d0 + i, 0))]
    out_shape = [jax.ShapeDtypeStruct((dst_rows, D), F32)]
    if convert:
        nw1, nw2, nl, ns = convert
        assert n_tiles >= FFN_CONV_STEPS
        r1, r2 = D // FFN_CONV_STEPS, D_FF // FFN_CONV_STEPS
        last = FFN_CONV_STEPS - 1
        in_specs += [
            pl.BlockSpec((None, None, r1, 2 * D_FF), lambda i: (nl, ns, jnp.minimum(i, last), 0)),
            pl.BlockSpec((None, None, r2, D), lambda i: (nl, ns, jnp.minimum(i, last), 0)),
        ]
        args += [nw1, nw2]
        out_specs += [pl.BlockSpec((r1, 2 * D_FF), lambda i: (jnp.minimum(i, last), 0)),
                      pl.BlockSpec((r2, D), lambda i: (jnp.minimum(i, last), 0))]
        out_shape += [jax.ShapeDtypeStruct((D, 2 * D_FF), BF16),
                      jax.ShapeDtypeStruct((D_FF, D), BF16)]
    res = pl.pallas_call(
        functools.partial(_ffn_body, k=k, tile0=t0, tiles_a=tiles_a, attn=bool(attn),
                          convert=bool(convert)),
        grid=(n_tiles,),
        in_specs=in_specs,
        out_specs=out_specs,
        out_shape=out_shape,
        compiler_params=_cparams(1),
        name="half_ffn",
    )(*args)
    return (res[0], res[1], res[2]) if convert else (res[0], None, None)


def _dft_cos_sin(n):
    idx = np.arange(n)
    ang = 2.0 * np.pi * ((idx[:, None] * idx[None, :]) % n) / n
    return np.cos(ang), np.sin(ang)


def _fourier_constants():
    c128, s128 = _dft_cos_sin(FG_DIM)
    chan = np.stack([c128, s128]) / np.sqrt(FG_DIM)

    c256, s256 = _dft_cos_sin(SEQ)
    dft_prompt = np.concatenate([c256, -s256], axis=0) / np.sqrt(SEQ)

    c64, s64 = _dft_cos_sin(GRID)
    c64 = c64 / np.sqrt(GRID)
    s64 = s64 / np.sqrt(GRID)
    eye = np.eye(FS_SUB)
    rows = GRID * FS_SUB
    m1 = np.concatenate([np.kron(c64, eye), np.kron(-s64, eye)], axis=0)
    def spread(f):
        return np.einsum("kn,jl->kjln", f, eye).reshape(rows, rows)
    m2 = np.block([[spread(c64), spread(s64)], [spread(-s64), spread(c64)]])
    k1 = np.arange(GRID)[:, None]
    n2 = np.arange(GRID)[None, :]
    ang = 2.0 * np.pi * (k1 * n2) / LAT
    tw = np.stack([np.cos(ang), np.sin(ang)])[..., None] * np.ones((1, 1, 1, 128))

    def mxu_const(a):
        return jnp.asarray(a, F32).astype(BF16)

    return (jnp.asarray(chan, F32), mxu_const(dft_prompt), mxu_const(m1), mxu_const(m2),
            jnp.asarray(tw, F32))


def _fold_body(cs_ref, w_ref, o_ref):
    for g in range(FOLD_ROWS // FG_DIM):
        rows = slice(g * FG_DIM, (g + 1) * FG_DIM)
        w = w_ref[rows, :]
        for t in range(2):
            o_ref[t, rows, :] = jnp.dot(cs_ref[t], w, preferred_element_type=F32,
                                        precision=lax.Precision.HIGHEST).astype(BF16)


def _fold_channel_dft(chan, w_out):
    n_layers = w_out.shape[0]
    return pl.pallas_call(
        _fold_body,
        grid=(n_layers, D // FOLD_ROWS),
        in_specs=[
            pl.BlockSpec((2, FG_DIM, FG_DIM), lambda l, r: (0, 0, 0)),
            pl.BlockSpec((None, FOLD_ROWS, D), lambda l, r: (l, r, 0)),
        ],
        out_specs=pl.BlockSpec((None, 2, FOLD_ROWS, D), lambda l, r: (l, 0, r, 0)),
        out_shape=jax.ShapeDtypeStruct((n_layers, 2, D, D), BF16),
        compiler_params=_cparams(2),
        name="fold_channel_dft",
    )(chan, w_out)


def _four_prompt_body(x_ref, mod_ref, gpre_ref, gpost_ref, win_ref, dft_ref, w2_ref, o_ref):
    x = x_ref[...]
    h = _prenorm(x, gpre_ref[...], mod_ref, 1)
    u = _dot(h, win_ref[...]).astype(BF16)
    p = [_dot(dft_ref[...], u[nb * SEQ:(nb + 1) * SEQ]) for nb in range(FP_NB)]
    re = jnp.concatenate([t[:SEQ] for t in p], axis=0).astype(BF16)
    im = jnp.concatenate([t[SEQ:] for t in p], axis=0).astype(BF16)
    y = _dot(re, w2_ref[0]) + _dot(im, w2_ref[1])
    o_ref[...] = x + mod_ref[5:6, :] * _postnorm(y, gpost_ref[...])


def _fourier_prompt(x, mods, pre, post, w_in, dft_prompt, w2, layer):
    return pl.pallas_call(
        _four_prompt_body,
        grid=(BATCH // FP_NB,),
        in_specs=[
            pl.BlockSpec((FP_NB * SEQ, D), lambda b: (b, 0)),
            _mod_spec(layer, lambda b: 0),
            _const_spec((1, D), layer, 1),
            _const_spec((1, D), layer, 1),
            _const_spec((D, D), layer // 2),
            _const_spec((2 * SEQ, SEQ)),
            _const_spec((2, D, D), layer // 2),
        ],
        out_specs=pl.BlockSpec((FP_NB * SEQ, D), lambda b: (b, 0)),
        out_shape=jax.ShapeDtypeStruct((T_ALL, D), F32),
        input_output_aliases={0: 0},
        compiler_params=_cparams(1),
        name="fourier_prompt",
    )(x, mods, pre, post, w_in, dft_prompt, w2)


def _four_s1_body(x_ref, mod_ref, gpre_ref, win_ref, m1_ref, tw_ref, y_ref, y_scr):
    rows = GRID * FS_SUB
    for s in range(FS_TN // FS_SUB):
        cols = slice(s * FS_SUB, (s + 1) * FS_SUB)
        x = x_ref[:, cols, :].reshape(rows, D)
        h = _prenorm(x, gpre_ref[...], mod_ref, 1)
        u = _dot(h, win_ref[...]).astype(BF16)
        y = _dot(m1_ref[...], u)
        tc = tw_ref[0, :, cols, :].reshape(rows, 128)
        ts = tw_ref[1, :, cols, :].reshape(rows, 128)
        for l in range(D // 128):
            sl = slice(l * 128, (l + 1) * 128)
            yr = y[:rows, sl]
            yi = y[rows:, sl]
            y_scr[0, :, cols, sl] = (yr * tc + yi * ts).reshape(GRID, FS_SUB, 128)
            y_scr[1, :, cols, sl] = (yi * tc - yr * ts).reshape(GRID, FS_SUB, 128)
    y_ref[...] = y_scr[...].astype(BF16)


def _fourier_latent_stage1(x4, mods, pre, w_in, m1, tw, layer):
    rows = GRID * FS_SUB
    return pl.pallas_call(
        _four_s1_body,
        grid=(N_LAT, GRID // FS_TN),
        in_specs=[
            pl.BlockSpec((None, GRID, FS_TN, D), lambda b, j: (b + 1, 0, j, 0)),
            _mod_spec(layer, lambda b, j: b + 1),
            _const_spec((1, D), layer, 1),
            _const_spec((D, D), layer // 2),
            _const_spec((2 * rows, rows)),
            pl.BlockSpec((2, GRID, FS_TN, 128), lambda b, j: (0, 0, j, 0)),
        ],
        out_specs=pl.BlockSpec((None, 2, GRID, FS_TN, D), lambda b, j: (b, 0, 0, j, 0)),
        out_shape=jax.ShapeDtypeStruct((N_LAT, 2, GRID, GRID, D), BF16),
        scratch_shapes=[pltpu.VMEM((2, GRID, FS_TN, D), F32)],
        compiler_params=_cparams(2),
        name="fourier_latent_stage1",
    )(x4, mods, pre, w_in, m1, tw)


def _four_s2_body(y_ref, x_ref, mod_ref, gpost_ref, m2_ref, w2_ref, o_ref):
    rows = GRID * FS_SUB
    yin = y_ref[...].reshape(2 * rows, D)
    z = _dot(m2_ref[...], yin)
    y = _dot(z[:rows].astype(BF16), w2_ref[0]) + _dot(z[rows:].astype(BF16), w2_ref[1])
    x = x_ref[...].reshape(rows, D)
    out = x + mod_ref[5:6, :] * _postnorm(y, gpost_ref[...])
    o_ref[...] = out.reshape(GRID, FS_SUB, D)


def _fourier_latent_stage2(y5, x4, mods, post, m2, w2, layer):
    rows = GRID * FS_SUB
    return pl.pallas_call(
        _four_s2_body,
        grid=(N_LAT, GRID // FS_SUB),
        in_specs=[
            pl.BlockSpec((None, 2, FS_SUB, GRID, D), lambda b, j: (b, 0, j, 0, 0)),
            pl.BlockSpec((None, GRID, FS_SUB, D), lambda b, j: (b + 1, 0, j, 0)),
            _mod_spec(layer, lambda b, j: b + 1),
            _const_spec((1, D), layer, 1),
            _const_spec((2 * rows, 2 * rows)),
            _const_spec((2, D, D), layer // 2),
        ],
        out_specs=pl.BlockSpec((None, GRID, FS_SUB, D), lambda b, j: (b + 1, 0, j, 0)),
        out_shape=jax.ShapeDtypeStruct((N_GROUPS, GRID, GRID, D), F32),
        input_output_aliases={1: 0},
        compiler_params=_cparams(2),
        name="fourier_latent_stage2",
    )(y5, x4, mods, post, m2, w2)


def _ctx_attn_body(x_ref, mod_ref, gpre_ref, gpost_ref, wqkv_ref, wout_ref, *rest, fresh_slot):
    o_ref, kc_ref, vc_ref = rest[-3:]
    x = x_ref[...]
    h = _prenorm(x, gpre_ref[...], mod_ref, 1)
    qkv = _dot(h, wqkv_ref[...])
    k = qkv[:, D:2 * D]
    v = qkv[:, 2 * D:]
    for nb in range(CTX_NB):
        rows = slice(nb * SEQ, (nb + 1) * SEQ)
        for hd in range(N_HEADS):
            sl = slice(hd * HEAD_DIM, (hd + 1) * HEAD_DIM)
            if fresh_slot is None:
                kc_ref[nb, hd] = k[rows, sl]
                vc_ref[nb, hd] = v[rows, sl]
            else:
                kc_ref[nb, fresh_slot, hd] = k[rows, sl]
                vc_ref[nb, fresh_slot, hd] = v[rows, sl]
        if fresh_slot is not None:
            other = jnp.zeros((N_HEADS, SEQ, HEAD_DIM), F32)
            kc_ref[nb, 1 - fresh_slot] = other
            vc_ref[nb, 1 - fresh_slot] = other
    qb = (qkv[:, :D] * Q_SCALE).astype(BF16)
    kb = k.astype(BF16)
    vb = v.astype(BF16)
    first = lax.broadcasted_iota(jnp.int32, (1, 2 * HEAD_DIM), 1) < HEAD_DIM
    zero = jnp.zeros((), BF16)
    seqs = []
    for nb in range(CTX_NB):
        rows = slice(nb * SEQ, (nb + 1) * SEQ)
        outs = []
        for p in range(N_PAIRS):
            sl = slice(p * 2 * HEAD_DIM, (p + 1) * 2 * HEAD_DIM)
            kp = kb[rows, sl]
            vp = vb[rows, sl]
            k2 = jnp.concatenate([jnp.where(first, kp, zero), jnp.where(first, zero, kp)], axis=0)
            v2 = jnp.concatenate([jnp.where(first, vp, zero), jnp.where(first, zero, vp)], axis=0)
            s = _dot_nt(qb[rows, sl], k2)
            probs = []
            for hh in range(2):
                sh = s[:, hh * SEQ:(hh + 1) * SEQ]
                e = jnp.exp2(sh - jnp.max(sh, axis=-1, keepdims=True))
                probs.append(e / jnp.sum(e, axis=-1, keepdims=True))
            outs.append(_dot(jnp.concatenate(probs, axis=1).astype(BF16), v2))
        seqs.append(jnp.concatenate(outs, axis=1).astype(BF16))
    y = _dot(jnp.concatenate(seqs, axis=0), wout_ref[...])
    o_ref[...] = x + mod_ref[5:6, :] * _postnorm(y, gpost_ref[...])


def _context_attention(x, mods, pre, post, w_qkv, w_out, layer, caches):
    j = layer // 2
    rows = CTX_NB * SEQ
    cache = jax.ShapeDtypeStruct((BATCH, DEPTH // 2, N_HEADS, SEQ, HEAD_DIM), F32)
    if caches:
        cache_spec = pl.BlockSpec((CTX_NB, None, N_HEADS, SEQ, HEAD_DIM), lambda b: (b, j, 0, 0, 0))
    else:
        cache_spec = pl.BlockSpec((CTX_NB, DEPTH // 2, N_HEADS, SEQ, HEAD_DIM),
                                  lambda b: (b, 0, 0, 0, 0))
    in_specs = [
        pl.BlockSpec((rows, D), lambda b: (b, 0)),
        _mod_spec(layer, lambda b: 0),
        _const_spec((1, D), layer, 1),
        _const_spec((1, D), layer, 1),
        _const_spec((D, 3 * D), j),
        _const_spec((D, D), j),
    ]
    aliases = {0: 0}
    if caches:
        in_specs += [pl.BlockSpec(memory_space=pl.ANY)] * 2
        aliases.update({6: 1, 7: 2})
    return pl.pallas_call(
        functools.partial(_ctx_attn_body, fresh_slot=None if caches else j),
        grid=(BATCH // CTX_NB,),
        in_specs=in_specs,
        out_specs=[pl.BlockSpec((rows, D), lambda b: (b, 0)), cache_spec, cache_spec],
        out_shape=[jax.ShapeDtypeStruct((T_ALL, D), F32), cache, cache],
        input_output_aliases=aliases,
        compiler_params=_cparams(1),
        name="context_attention",
    )(x, mods, pre, post, w_qkv, w_out, *caches)


def _qkv_lat_body(x_ref, mod_ref, gpre_ref, wq_ref, wkt_ref, wv_ref, q_ref, kt_ref, v_ref):
    h = _prenorm(x_ref[...], gpre_ref[...], mod_ref, 1)
    q_ref[...] = (_dot(h, wq_ref[...]) * Q_SCALE).astype(BF16)
    v_ref[...] = _dot(h, wv_ref[...]).astype(BF16)
    kt_ref[...] = _dot_nt(wkt_ref[...], h).astype(BF16)


def _qkv_latent(x, mods, pre, w_qkv, wkt, layer):
    j = layer // 2
    per_lat = LAT // QKV_TM
    first = GROUP_ROWS // QKV_TM
    tok = jax.ShapeDtypeStruct((N_LAT * LAT, D), BF16)

    def w_cols(col_block):
        return pl.BlockSpec((None, D, D), lambda i: (j, 0, col_block), pipeline_mode=pl.Buffered(1))

    return pl.pallas_call(
        _qkv_lat_body,
        grid=(N_LAT * per_lat,),
        in_specs=[
            pl.BlockSpec((QKV_TM, D), lambda i: (first + i, 0)),
            _mod_spec(layer, lambda i: 1 + i // per_lat),
            _const_spec((1, D), layer, 1),
            w_cols(0),
            _const_spec((D, D), j),
            w_cols(2),
        ],
        out_specs=[
            pl.BlockSpec((QKV_TM, D), lambda i: (i, 0)),
            pl.BlockSpec((None, D, QKV_TM), lambda i: (i // per_lat, 0, i % per_lat)),
            pl.BlockSpec((QKV_TM, D), lambda i: (i, 0)),
        ],
        out_shape=[tok, jax.ShapeDtypeStruct((N_LAT, D, LAT), BF16), tok],
        compiler_params=_cparams(1),
        name="qkv_latent",
    )(x, mods, pre, w_qkv, wkt, w_qkv)


def _na_block_geometry(blk):
    return min(max(NA_ROWS * blk - WIN_ROWS // 2, 0), GRID - NA_KROWS)


def _na_variant(blk):
    return 0 if blk == 0 else (2 if blk == NA_BLOCKS - 1 else 1)


def _na_tables():
    t = np.arange(NA_DT)[:, None]
    e = np.arange(2)[None, :]
    drow = t - 11 + e
    row_ok = (drow >= -(WIN_ROWS - 1)) & (drow <= WIN_ROWS - 1)
    c = np.arange(GRID)[:, None]
    cp = np.arange(GRID)[None, :]
    cs = np.clip(c - WIN_COLS // 2, 0, GRID - WIN_COLS)
    col_ok = (cp >= cs) & (cp < cs + WIN_COLS)
    ok = row_ok[:, None, :, None] & col_ok[None, :, None, :]
    rowmask = np.zeros((NA_MASK_ROWS, NA_K), np.float32)
    delta0 = np.zeros((NA_VARIANTS, NA_ROWS, NA_KROWS // 2), np.int64)
    seen = set()
    for blk in range(NA_BLOCKS):
        v = _na_variant(blk)
        u0 = _na_block_geometry(blk)
        for rl in range(NA_ROWS):
            r = NA_ROWS * blk + rl
            rs = min(max(r - WIN_ROWS // 2, 0), GRID - WIN_ROWS)
            kr = u0 + np.arange(NA_KROWS)
            valid = (kr >= rs) & (kr < rs + WIN_ROWS)
            mask = np.repeat(np.where(valid, 0.0, NEG), GRID).astype(np.float32)
            d0 = u0 - r + 2 * np.arange(NA_KROWS // 2) + 11
            if (v, rl) in seen:
                assert (rowmask[v * NA_ROWS + rl] == mask).all() and (delta0[v, rl] == d0).all()
            seen.add((v, rl))
            rowmask[v * NA_ROWS + rl] = mask
            delta0[v, rl] = d0
    assert delta0.min() >= 0 and delta0.max() < NA_DT
    pairmask = np.where(ok, 0.0, NEG).astype(np.float32).reshape(NA_DT, GRID, 2 * GRID)
    return pairmask, rowmask, delta0


def _na_bias_rows(rpb):
    nrow = 2 * WIN_ROWS - 1
    fill = jnp.full((N_HEADS, nrow, 2 * GRID - (2 * WIN_COLS - 1)), NEG, F32)
    p = jnp.concatenate([rpb[:, :, WIN_COLS - 1:] * LOG2E, fill, rpb[:, :, :WIN_COLS - 1] * LOG2E],
                        axis=-1)
    return jnp.concatenate([p, jnp.full((N_HEADS, 1, 2 * GRID), NEG, F32)], axis=1)


def _na_expand_bias(p_ref, pm_ref, rm_ref, pair_scr, bias_scr, delta0):
    nrow = 2 * WIN_ROWS - 1
    left = lax.broadcasted_iota(jnp.int32, (1, 2 * GRID), 1) < GRID
    for hh in range(2):
        toep = [pltpu.roll(jnp.broadcast_to(p_ref[hh, i:i + 1, :], (GRID, 2 * GRID)), 0, 1,
                           stride=1, stride_axis=0) for i in range(nrow + 1)]
        shifted = [pltpu.roll(t, GRID, 1) for t in toep]
        for t in range(NA_DT):
            i0, i1 = (d + WIN_ROWS - 1 if abs(d) < WIN_ROWS else nrow for d in (t - 11, t - 10))
            pair_scr[hh, t] = jnp.where(left, toep[i0], shifted[i1]) + pm_ref[t]
        for var in range(NA_VARIANTS):
            for rl in range(NA_ROWS):
                strip = jnp.concatenate(
                    [pair_scr[hh, int(delta0[var, rl, i])] for i in range(NA_KROWS // 2)], axis=1)
                row = var * NA_ROWS + rl
                bias_scr[hh, var, rl * GRID:(rl + 1) * GRID, :] = strip + rm_ref[row:row + 1, :]


def _natten_body(q_ref, kt_ref, v_ref, kct_ref, vc_ref, p_ref, pm_ref, rm_ref, o_ref,
                 pair_scr, bias_scr, s_scr, p_scr, *, delta0):
    @pl.when(pl.program_id(1) == 0)
    def _():
        _na_expand_bias(p_ref, pm_ref, rm_ref, pair_scr, bias_scr, delta0)

    first = lax.broadcasted_iota(jnp.int32, (1, 2 * HEAD_DIM), 1) < HEAD_DIM
    zero = jnp.zeros((), BF16)
    kct = kct_ref[...]
    vc = vc_ref[...]

    keeps = (first, jnp.logical_not(first))

    def offsets(blk):
        koff = pl.multiple_of(jnp.clip(blk - 1, 0, NA_BLOCKS - 3) * NA_Q, NA_Q)
        qoff = pl.multiple_of(blk * NA_Q, NA_Q)
        return koff, qoff

    def logits(blk, slot):
        koff, qoff = offsets(blk)
        var = jnp.where(blk == 0, 0, jnp.where(blk == NA_BLOCKS - 1, 2, 1))
        qb = q_ref[pl.ds(qoff, NA_Q), :]
        ktb = kt_ref[:, pl.ds(koff, NA_K)]
        for hh in range(2):
            qh = jnp.where(keeps[hh], qb, zero)
            s_scr[slot, hh, :, :NA_K] = _dot(qh, ktb) + bias_scr[hh, var]
            s_scr[slot, hh, :, NA_K:] = _dot(qh, kct)

    def softmax(slot):
        for hh in range(2):
            s = s_scr[slot, hh]
            p_scr[slot, hh] = jnp.exp2(s - jnp.max(s, axis=-1, keepdims=True)).astype(BF16)

    lane_id = lax.broadcasted_iota(jnp.int32, (1, 2 * HEAD_DIM), 1)

    def values(blk, slot):
        koff, qoff = offsets(blk)
        vb = v_ref[pl.ds(koff, NA_K), :]
        o = jnp.zeros((NA_Q, 2 * HEAD_DIM), F32)
        for hh in range(2):
            den_lane = HEAD_DIM if hh == 0 else 0
            ones_col = jnp.where(lane_id == den_lane, 1.0, 0.0).astype(BF16)
            oh = (_dot(p_scr[slot, hh, :, :NA_K], jnp.where(keeps[hh], vb, ones_col))
                  + _dot(p_scr[slot, hh, :, NA_K:], jnp.where(keeps[hh], vc, ones_col)))
            o = o + jnp.where(keeps[hh], oh / oh[:, den_lane:den_lane + 1], 0.0)
        o_ref[pl.ds(qoff, NA_Q), :] = o.astype(BF16)

    logits(0, 0)
    softmax(0)
    logits(1, 1)

    def steady(it, carry):
        t = 2 + 2 * it
        logits(t, 0)
        values(t - 2, 0)
        softmax(1)
        logits(t + 1, 1)
        values(t - 1, 1)
        softmax(0)
        return carry

    lax.fori_loop(0, (NA_BLOCKS - 2) // 2, steady, 0, unroll=True)
    values(NA_BLOCKS - 2, 0)
    softmax(1)
    values(NA_BLOCKS - 1, 1)


def _neighbourhood_attention(q, kt, v, kct, vc, bias_rows, pairmask, rowmask, delta0):
    lane = 2 * HEAD_DIM
    return pl.pallas_call(
        functools.partial(_natten_body, delta0=delta0),
        grid=(N_PAIRS, N_LAT),
        in_specs=[
            pl.BlockSpec((LAT, lane), lambda p, b: (b, p)),
            pl.BlockSpec((None, lane, LAT), lambda p, b: (b, p, 0)),
            pl.BlockSpec((LAT, lane), lambda p, b: (b, p)),
            pl.BlockSpec((None, lane, PAST), lambda p, b: (b, p, 0)),
            pl.BlockSpec((None, PAST, lane), lambda p, b: (b, 0, p)),
            pl.BlockSpec((2, 2 * WIN_ROWS, lane), lambda p, b: (p, 0, 0)),
            pl.BlockSpec((NA_DT, GRID, lane), lambda p, b: (0, 0, 0)),
            pl.BlockSpec((NA_MASK_ROWS, NA_K), lambda p, b: (0, 0)),
        ],
        out_specs=pl.BlockSpec((LAT, lane), lambda p, b: (b, p)),
        out_shape=jax.ShapeDtypeStruct((N_LAT * LAT, D), BF16),
        scratch_shapes=[
            pltpu.VMEM((2, NA_DT, GRID, lane), F32),
            pltpu.VMEM((2, NA_VARIANTS, NA_Q, NA_K), F32),
            pltpu.VMEM((2, 2, NA_Q, NA_K + PAST), F32),
            pltpu.VMEM((2, 2, NA_Q, NA_K + PAST), BF16),
        ],
        compiler_params=_cparams(2),
        name="neighbourhood_attention",
    )(q, kt, v, kct, vc, bias_rows, pairmask, rowmask)


def kernel(x_prompt, x_sample, cache_k, cache_v, c, c_ctx, w_mod, b_mod, norm_pre, norm_post,
           ffn_w1, ffn_w2, four_w_in, four_w_out, na_w_qkv, na_w_out, na_rpb):
    conds = jnp.concatenate([c_ctx[None, :], c, jnp.zeros((8 - 1 - N_LAT, D), F32)], axis=0)
    mods = _modulation(conds, w_mod, b_mod)[:, :N_GROUPS].reshape(DEPTH, N_GROUPS, N_MOD, D)

    chan, dft_prompt, m1, m2, tw = _fourier_constants()
    pairmask, rowmask, delta0 = _na_tables()
    pairmask = jnp.asarray(pairmask)
    rowmask = jnp.asarray(rowmask)

    w_in = four_w_in.astype(BF16)
    wf = _fold_channel_dft(chan, four_w_out)
    w_qkv = na_w_qkv.astype(BF16)
    w_out = na_w_out.astype(BF16)
    wkt = w_qkv[:, :, D:2 * D].transpose(0, 2, 1)
    pre = norm_pre[:, :, None, :]
    post = norm_post[:, :, None, :]

    n_ctx = BATCH * SEQ
    n_lat = N_LAT * LAT
    ffn = functools.partial(_half_ffn, mods=mods, pre=pre, post=post)
    w1b = ffn_w1[0, 0].astype(BF16)
    w2b = ffn_w2[0, 0].astype(BF16)
    caches = ()
    for i in range(DEPTH):
        j = i // 2
        nxt = (ffn_w1, ffn_w2, i, 1)
        if i == 0:
            x, w1b, w2b = ffn(x_prompt.reshape(n_ctx, D), w1b, w2b, layer=0, slot=0,
                              x_b=x_sample.reshape(n_lat, D), convert=nxt)
        else:
            x, w1b, w2b = ffn(x, w1b, w2b, layer=i, slot=0, convert=nxt)
        attn = None
        if i % 2 == 0:
            x = _fourier_prompt(x, mods, pre, post, w_in, dft_prompt, wf, i)
            x4 = x.reshape(N_GROUPS, GRID, GRID, D)
            y5 = _fourier_latent_stage1(x4, mods, pre, w_in, m1, tw, i)
            x4 = _fourier_latent_stage2(y5, x4, mods, post, m2, wf, i)
            x = x4.reshape(T_ALL, D)
        else:
            x, *caches = _context_attention(x, mods, pre, post, w_qkv, w_out, i, caches)
            q, kt, v = _qkv_latent(x, mods, pre, w_qkv, wkt, i)
            kct = cache_k[:, j].transpose(0, 1, 3, 2).reshape(N_LAT, D, PAST).astype(BF16)
            vct = cache_v[:, j].transpose(0, 2, 1, 3).reshape(N_LAT, PAST, D).astype(BF16)
            o = _neighbourhood_attention(q, kt, v, kct, vct, _na_bias_rows(na_rpb[j]), pairmask,
                                         rowmask, delta0)
            attn = (o, w_out)
        if i < DEPTH - 1:
            x, w1b, w2b = ffn(x, w1b, w2b, layer=i, slot=1, attn=attn,
                              convert=(ffn_w1, ffn_w2, i + 1, 0))

    last = DEPTH - 1
    y_prompt, _, _ = ffn(x, w1b, w2b, layer=last, slot=1, n_rows=n_ctx, dst_rows=n_ctx)
    y_sample, _, _ = ffn(x, w1b, w2b, layer=last, slot=1, first_row=n_ctx, n_rows=n_lat,
                         dst_rows=n_lat, dst_row=0, attn=attn)
    return (y_prompt.reshape(BATCH, SEQ, D), y_sample.reshape(N_LAT, LAT, D), caches[0], caches[1])
```

```python
import functools

import numpy as np
import jax
import jax.numpy as jnp
from jax import lax
from jax.experimental import pallas as pl
from jax.experimental.pallas import tpu as pltpu

F32 = jnp.float32
BF16 = jnp.bfloat16

D = 1024
D_FF = 2816
DEPTH = 4
N_MOD = 9
N_HEADS = 16
HEAD_DIM = 64
N_PAIRS = N_HEADS // 2
SEQ = 256
BATCH = 16
GRID = 64
LAT = GRID * GRID
N_LAT = 2
GROUP_ROWS = 4096
N_GROUPS = 3
T_ALL = N_GROUPS * GROUP_ROWS
PAST = 256
WIN_ROWS = 8
WIN_COLS = 16
N_FG = 8
FG_DIM = D // N_FG
EPS = 1e-6
NEG = -1e30
LOG2E = 1.4426950408889634
Q_SCALE = HEAD_DIM ** -0.5 * LOG2E

VMEM_LIMIT = 56 * 1024 * 1024

FFN_TILE_LARGE = (1024, 128)
FFN_TILE_SMALL = (512, 256)
FFN_NC = 1
FFN_CONV_SLABS = 11
MOD_TN = 2304
FOLD_ROWS = 512
QKV_TM = 1024
FP_NB = 4
CTX_NB = 1
FS_TN = 16
FS_SUB = 8
NA_ROWS = 4
NA_Q = NA_ROWS * GRID
NA_KROWS = 12
NA_K = NA_KROWS * GRID
NA_BLOCKS = GRID // NA_ROWS
NA_VARIANTS = 3
NA_MASK_ROWS = 16
NA_DT = 22


def _cparams(n_axes):
    return pltpu.CompilerParams(
        dimension_semantics=("arbitrary",) * n_axes, vmem_limit_bytes=VMEM_LIMIT)


def _const_spec(shape, *lead):
    nd = len(shape)
    return pl.BlockSpec((None,) * len(lead) + tuple(shape), lambda *_: tuple(lead) + (0,) * nd,
                        pipeline_mode=pl.Buffered(1))


def _mod_spec(layer, group_of):
    return pl.BlockSpec((None, None, N_MOD, D), lambda *ids: (layer, group_of(*ids), 0, 0))


def _silu(x):
    return x * jax.nn.sigmoid(x)


def _prenorm(x, g, mod_ref, k):
    shift = mod_ref[3 * k:3 * k + 1, :]
    scale = mod_ref[3 * k + 1:3 * k + 2, :]
    y = x * lax.rsqrt(jnp.mean(x * x, axis=-1, keepdims=True) + EPS)
    return ((y * g) * (1.0 + scale) + shift).astype(BF16)


def _postnorm(y, g):
    return (y * lax.rsqrt(jnp.mean(y * y, axis=-1, keepdims=True) + EPS)) * g


def _dot(a, b):
    return jnp.dot(a, b, preferred_element_type=F32)


def _dot_nt(a, b):
    return lax.dot_general(a, b, (((1,), (1,)), ((), ())), preferred_element_type=F32)


def _mod_body(cond_ref, w_ref, b_ref, o_ref):
    s = _silu(cond_ref[...]).astype(BF16)
    o_ref[...] = _dot(s, w_ref[...].astype(BF16)) + b_ref[...]


def _modulation(conds, w_mod, b_mod):
    n = N_MOD * D
    return pl.pallas_call(
        _mod_body,
        grid=(DEPTH, n // MOD_TN),
        in_specs=[
            pl.BlockSpec((8, D), lambda l, j: (0, 0)),
            pl.BlockSpec((None, D, MOD_TN), lambda l, j: (l, 0, j)),
            pl.BlockSpec((None, 1, MOD_TN), lambda l, j: (l, 0, j)),
        ],
        out_specs=pl.BlockSpec((None, 8, MOD_TN), lambda l, j: (l, 0, j)),
        out_shape=jax.ShapeDtypeStruct((DEPTH, 8, n), F32),
        compiler_params=_cparams(2),
        name="modulation",
    )(conds, w_mod, b_mod.reshape(DEPTH, 1, n))


def _ffn_tile(load_x, mod_ref, gpre_ref, gpost_ref, w1_ref, w2_ref, o_ref, k, tile):
    ck = D_FF // FFN_NC
    gate = mod_ref[3 * k + 2:3 * k + 3, :]
    tm, sub = tile
    for r in range(tm // sub):
        rows = slice(r * sub, (r + 1) * sub)
        x = load_x(rows)
        h = _prenorm(x, gpre_ref[...], mod_ref, k)
        y = None
        for c in range(FFN_NC):
            g = _dot(h, w1_ref[:, c * ck:(c + 1) * ck])
            u = _dot(h, w1_ref[:, D_FF + c * ck:D_FF + (c + 1) * ck])
            a = (_silu(g) * u).astype(BF16)
            yc = _dot(a, w2_ref[c * ck:(c + 1) * ck, :])
            y = yc if y is None else y + yc
        o_ref[rows, :] = x + (0.5 * gate) * _postnorm(y, gpost_ref[...])


def _ffn_body(*refs, k, tile, tile0, tiles_a, attn, convert):
    refs = list(refs)
    x_ref = refs.pop(0)
    xb_ref = refs.pop(0) if tiles_a is not None else None
    ao_ref = refs.pop(0) if attn else None
    mod_ref, gpre_ref, gpost_ref, w1_ref, w2_ref = refs[:5]
    refs = refs[5:]
    if attn:
        gmix_ref, wout_ref = refs[:2]
        refs = refs[2:]
    if convert:
        nw1_ref, nw2_ref, o_ref, nw1b_ref, nw2b_ref = refs
        nw1b_ref[...] = nw1_ref[...].astype(BF16)
        nw2b_ref[...] = nw2_ref[...].astype(BF16)
    else:
        (o_ref,) = refs
    tile_id = tile0 + pl.program_id(0)
    tail = (mod_ref, gpre_ref, gpost_ref, w1_ref, w2_ref, o_ref, k, tile)

    def plain(rows):
        if xb_ref is None:
            return x_ref[rows, :]
        return jnp.where(tile_id < tiles_a, x_ref[rows, :], xb_ref[rows, :])

    def mixed(rows):
        y = _dot(ao_ref[rows, :], wout_ref[...])
        return x_ref[rows, :] + mod_ref[5:6, :] * _postnorm(y, gmix_ref[...])

    ctx_tiles = GROUP_ROWS // tile[0]
    if not attn:
        _ffn_tile(plain, *tail)
    elif tile0 >= ctx_tiles:
        _ffn_tile(mixed, *tail)
    else:
        pl.when(tile_id < ctx_tiles)(lambda: _ffn_tile(plain, *tail))
        pl.when(tile_id >= ctx_tiles)(lambda: _ffn_tile(mixed, *tail))


def _half_ffn(x, w1b, w2b, mods, pre, post, layer, slot, *, x_b=None, first_row=0, n_rows=T_ALL,
              dst_rows=T_ALL, dst_row=None, attn=None, convert=None):
    k = 2 * slot
    tile = FFN_TILE_LARGE if (x_b is None and not attn) else FFN_TILE_SMALL
    tm = tile[0]
    tiles_per_group = GROUP_ROWS // tm
    n_tiles = n_rows // tm
    t0 = first_row // tm
    d0 = t0 if dst_row is None else dst_row // tm
    tiles_a = None
    in_specs = [pl.BlockSpec((tm, D), lambda i: (t0 + i, 0))]
    args = [x]
    if x_b is not None:
        tiles_a = x.shape[0] // tm
        in_specs = [pl.BlockSpec((tm, D), lambda i: (jnp.minimum(i, tiles_a - 1), 0)),
                    pl.BlockSpec((tm, D), lambda i: (jnp.maximum(i - tiles_a, 0), 0))]
        args.append(x_b)
    if attn:
        in_specs.append(pl.BlockSpec(
            (tm, D), lambda i: (jnp.maximum(t0 + i - tiles_per_group, 0), 0)))
        args.append(attn[0])
    in_specs += [
        _mod_spec(layer, lambda i: (t0 + i) // tiles_per_group),
        _const_spec((1, D), layer, k),
        _const_spec((1, D), layer, k),
        _const_spec((D, 2 * D_FF)),
        _const_spec((D_FF, D)),
    ]
    args += [mods, pre, post, w1b, w2b]
    if attn:
        in_specs += [_const_spec((1, D), layer, 1), _const_spec((D, D), layer // 2)]
        args += [post, attn[1]]
    out_specs = [pl.BlockSpec((tm, D), lambda i: (d0 + i, 0))]
    out_shape = [jax.ShapeDtypeStruct((dst_rows, D), F32)]
    if convert:
        nw1, nw2, nl, ns = convert
        slabs = FFN_CONV_SLABS * (2 if n_tiles >= 2 * FFN_CONV_SLABS else 1)
        assert n_tiles >= slabs
        c1, r2 = 2 * D_FF // slabs, D_FF // slabs
        last = slabs - 1
        in_specs += [
            pl.BlockSpec((None, None, D, c1), lambda i: (nl, ns, 0, jnp.minimum(i, last))),
            pl.BlockSpec((None, None, r2, D), lambda i: (nl, ns, jnp.minimum(i, last), 0)),
        ]
        args += [nw1, nw2]
        out_specs += [pl.BlockSpec((D, c1), lambda i: (0, jnp.minimum(i, last))),
                      pl.BlockSpec((r2, D), lambda i: (jnp.minimum(i, last), 0))]
        out_shape += [jax.ShapeDtypeStruct((D, 2 * D_FF), BF16),
                      jax.ShapeDtypeStruct((D_FF, D), BF16)]
    res = pl.pallas_call(
        functools.partial(_ffn_body, k=k, tile=tile, tile0=t0, tiles_a=tiles_a, attn=bool(attn),
                          convert=bool(convert)),
        grid=(n_tiles,),
        in_specs=in_specs,
        out_specs=out_specs,
        out_shape=out_shape,
        compiler_params=_cparams(1),
        name="half_ffn",
    )(*args)
    return (res[0], res[1], res[2]) if convert else (res[0], None, None)


def _dft_cos_sin(n):
    idx = np.arange(n)
    ang = 2.0 * np.pi * ((idx[:, None] * idx[None, :]) % n) / n
    return np.cos(ang), np.sin(ang)


def _fourier_constants():
    c128, s128 = _dft_cos_sin(FG_DIM)
    chan = np.stack([c128, s128]) / np.sqrt(FG_DIM)

    c256, s256 = _dft_cos_sin(SEQ)
    dft_prompt = np.concatenate([c256, -s256], axis=0) / np.sqrt(SEQ)

    c64, s64 = _dft_cos_sin(GRID)
    c64 = c64 / np.sqrt(GRID)
    s64 = s64 / np.sqrt(GRID)
    eye = np.eye(FS_SUB)
    rows = GRID * FS_SUB
    m1 = np.concatenate([np.kron(c64, eye), np.kron(-s64, eye)], axis=0)
    def spread(f):
        return np.einsum("kn,jl->kjln", f, eye).reshape(rows, rows)
    m2 = np.block([[spread(c64), spread(s64)], [spread(-s64), spread(c64)]])
    k1 = np.arange(GRID)[:, None]
    n2 = np.arange(GRID)[None, :]
    ang = 2.0 * np.pi * (k1 * n2) / LAT
    tw = np.stack([np.cos(ang), np.sin(ang)])[..., None] * np.ones((1, 1, 1, 128))

    def mxu_const(a):
        return jnp.asarray(a, F32).astype(BF16)

    return (jnp.asarray(chan, F32), mxu_const(dft_prompt), mxu_const(m1), mxu_const(m2),
            jnp.asarray(tw, F32))


def _fold_body(cs_ref, w_ref, o_ref):
    for g in range(FOLD_ROWS // FG_DIM):
        rows = slice(g * FG_DIM, (g + 1) * FG_DIM)
        w = w_ref[rows, :]
        for t in range(2):
            o_ref[t, rows, :] = jnp.dot(cs_ref[t], w, preferred_element_type=F32,
                                        precision=lax.Precision.HIGHEST).astype(BF16)


def _fold_channel_dft(chan, w_out):
    n_layers = w_out.shape[0]
    return pl.pallas_call(
        _fold_body,
        grid=(n_layers, D // FOLD_ROWS),
        in_specs=[
            pl.BlockSpec((2, FG_DIM, FG_DIM), lambda l, r: (0, 0, 0)),
            pl.BlockSpec((None, FOLD_ROWS, D), lambda l, r: (l, r, 0)),
        ],
        out_specs=pl.BlockSpec((None, 2, FOLD_ROWS, D), lambda l, r: (l, 0, r, 0)),
        out_shape=jax.ShapeDtypeStruct((n_layers, 2, D, D), BF16),
        compiler_params=_cparams(2),
        name="fold_channel_dft",
    )(chan, w_out)


def _four_prompt_body(x_ref, mod_ref, gpre_ref, gpost_ref, win_ref, dft_ref, w2_ref, o_ref):
    x = x_ref[...]
    h = _prenorm(x, gpre_ref[...], mod_ref, 1)
    u = _dot(h, win_ref[...]).astype(BF16)
    p = [_dot(dft_ref[...], u[nb * SEQ:(nb + 1) * SEQ]) for nb in range(FP_NB)]
    re = jnp.concatenate([t[:SEQ] for t in p], axis=0).astype(BF16)
    im = jnp.concatenate([t[SEQ:] for t in p], axis=0).astype(BF16)
    y = _dot(re, w2_ref[0]) + _dot(im, w2_ref[1])
    o_ref[...] = x + mod_ref[5:6, :] * _postnorm(y, gpost_ref[...])


def _fourier_prompt(x, mods, pre, post, w_in, dft_prompt, w2, layer):
    return pl.pallas_call(
        _four_prompt_body,
        grid=(BATCH // FP_NB,),
        in_specs=[
            pl.BlockSpec((FP_NB * SEQ, D), lambda b: (b, 0)),
            _mod_spec(layer, lambda b: 0),
            _const_spec((1, D), layer, 1),
            _const_spec((1, D), layer, 1),
            _const_spec((D, D), layer // 2),
            _const_spec((2 * SEQ, SEQ)),
            _const_spec((2, D, D), layer // 2),
        ],
        out_specs=pl.BlockSpec((FP_NB * SEQ, D), lambda b: (b, 0)),
        out_shape=jax.ShapeDtypeStruct((T_ALL, D), F32),
        input_output_aliases={0: 0},
        compiler_params=_cparams(1),
        name="fourier_prompt",
    )(x, mods, pre, post, w_in, dft_prompt, w2)


def _four_s1_body(x_ref, mod_ref, gpre_ref, win_ref, m1_ref, tw_ref, y_ref, y_scr):
    rows = GRID * FS_SUB
    for s in range(FS_TN // FS_SUB):
        cols = slice(s * FS_SUB, (s + 1) * FS_SUB)
        x = x_ref[:, cols, :].reshape(rows, D)
        h = _prenorm(x, gpre_ref[...], mod_ref, 1)
        u = _dot(h, win_ref[...]).astype(BF16)
        y = _dot(m1_ref[...], u)
        tc = tw_ref[0, :, cols, :].reshape(rows, 128)
        ts = tw_ref[1, :, cols, :].reshape(rows, 128)
        for l in range(D // 128):
            sl = slice(l * 128, (l + 1) * 128)
            yr = y[:rows, sl]
            yi = y[rows:, sl]
            y_scr[0, :, cols, sl] = (yr * tc + yi * ts).reshape(GRID, FS_SUB, 128)
            y_scr[1, :, cols, sl] = (yi * tc - yr * ts).reshape(GRID, FS_SUB, 128)
    y_ref[...] = y_scr[...].astype(BF16)


def _fourier_latent_stage1(x4, mods, pre, w_in, m1, tw, layer):
    rows = GRID * FS_SUB
    return pl.pallas_call(
        _four_s1_body,
        grid=(N_LAT, GRID // FS_TN),
        in_specs=[
            pl.BlockSpec((None, GRID, FS_TN, D), lambda b, j: (b + 1, 0, j, 0)),
            _mod_spec(layer, lambda b, j: b + 1),
            _const_spec((1, D), layer, 1),
            _const_spec((D, D), layer // 2),
            _const_spec((2 * rows, rows)),
            pl.BlockSpec((2, GRID, FS_TN, 128), lambda b, j: (0, 0, j, 0)),
        ],
        out_specs=pl.BlockSpec((None, 2, GRID, FS_TN, D), lambda b, j: (b, 0, 0, j, 0)),
        out_shape=jax.ShapeDtypeStruct((N_LAT, 2, GRID, GRID, D), BF16),
        scratch_shapes=[pltpu.VMEM((2, GRID, FS_TN, D), F32)],
        compiler_params=_cparams(2),
        name="fourier_latent_stage1",
    )(x4, mods, pre, w_in, m1, tw)


def _four_s2_body(y_ref, x_ref, mod_ref, gpost_ref, m2_ref, w2_ref, o_ref):
    rows = GRID * FS_SUB
    yin = y_ref[...].reshape(2 * rows, D)
    z = _dot(m2_ref[...], yin)
    y = _dot(z[:rows].astype(BF16), w2_ref[0]) + _dot(z[rows:].astype(BF16), w2_ref[1])
    x = x_ref[...].reshape(rows, D)
    out = x + mod_ref[5:6, :] * _postnorm(y, gpost_ref[...])
    o_ref[...] = out.reshape(GRID, FS_SUB, D)


def _fourier_latent_stage2(y5, x4, mods, post, m2, w2, layer):
    rows = GRID * FS_SUB
    return pl.pallas_call(
        _four_s2_body,
        grid=(N_LAT, GRID // FS_SUB),
        in_specs=[
            pl.BlockSpec((None, 2, FS_SUB, GRID, D), lambda b, j: (b, 0, j, 0, 0)),
            pl.BlockSpec((None, GRID, FS_SUB, D), lambda b, j: (b + 1, 0, j, 0)),
            _mod_spec(layer, lambda b, j: b + 1),
            _const_spec((1, D), layer, 1),
            _const_spec((2 * rows, 2 * rows)),
            _const_spec((2, D, D), layer // 2),
        ],
        out_specs=pl.BlockSpec((None, GRID, FS_SUB, D), lambda b, j: (b + 1, 0, j, 0)),
        out_shape=jax.ShapeDtypeStruct((N_GROUPS, GRID, GRID, D), F32),
        input_output_aliases={1: 0},
        compiler_params=_cparams(2),
        name="fourier_latent_stage2",
    )(y5, x4, mods, post, m2, w2)


def _ctx_attn_body(x_ref, mod_ref, gpre_ref, gpost_ref, wqkv_ref, wout_ref, *rest, fresh_slot):
    o_ref, kc_ref, vc_ref = rest[-3:]
    x = x_ref[...]
    h = _prenorm(x, gpre_ref[...], mod_ref, 1)
    qkv = _dot(h, wqkv_ref[...])
    k = qkv[:, D:2 * D]
    v = qkv[:, 2 * D:]
    for nb in range(CTX_NB):
        rows = slice(nb * SEQ, (nb + 1) * SEQ)
        for hd in range(N_HEADS):
            sl = slice(hd * HEAD_DIM, (hd + 1) * HEAD_DIM)
            if fresh_slot is None:
                kc_ref[nb, hd] = k[rows, sl]
                vc_ref[nb, hd] = v[rows, sl]
            else:
                kc_ref[nb, fresh_slot, hd] = k[rows, sl]
                vc_ref[nb, fresh_slot, hd] = v[rows, sl]
        if fresh_slot is not None:
            other = jnp.zeros((N_HEADS, SEQ, HEAD_DIM), F32)
            kc_ref[nb, 1 - fresh_slot] = other
            vc_ref[nb, 1 - fresh_slot] = other
    qb = (qkv[:, :D] * Q_SCALE).astype(BF16)
    kb = k.astype(BF16)
    vb = v.astype(BF16)
    first = lax.broadcasted_iota(jnp.int32, (1, 2 * HEAD_DIM), 1) < HEAD_DIM
    zero = jnp.zeros((), BF16)
    seqs = []
    for nb in range(CTX_NB):
        rows = slice(nb * SEQ, (nb + 1) * SEQ)
        outs = []
        for p in range(N_PAIRS):
            sl = slice(p * 2 * HEAD_DIM, (p + 1) * 2 * HEAD_DIM)
            kp = kb[rows, sl]
            vp = vb[rows, sl]
            k2 = jnp.concatenate([jnp.where(first, kp, zero), jnp.where(first, zero, kp)], axis=0)
            v2 = jnp.concatenate([jnp.where(first, vp, zero), jnp.where(first, zero, vp)], axis=0)
            s = _dot_nt(qb[rows, sl], k2)
            probs = []
            for hh in range(2):
                sh = s[:, hh * SEQ:(hh + 1) * SEQ]
                e = jnp.exp2(sh - jnp.max(sh, axis=-1, keepdims=True))
                probs.append(e / jnp.sum(e, axis=-1, keepdims=True))
            outs.append(_dot(jnp.concatenate(probs, axis=1).astype(BF16), v2))
        seqs.append(jnp.concatenate(outs, axis=1).astype(BF16))
    y = _dot(jnp.concatenate(seqs, axis=0), wout_ref[...])
    o_ref[...] = x + mod_ref[5:6, :] * _postnorm(y, gpost_ref[...])


def _context_attention(x, mods, pre, post, w_qkv, w_out, layer, caches):
    j = layer // 2
    rows = CTX_NB * SEQ
    cache = jax.ShapeDtypeStruct((BATCH, DEPTH // 2, N_HEADS, SEQ, HEAD_DIM), F32)
    if caches:
        cache_spec = pl.BlockSpec((CTX_NB, None, N_HEADS, SEQ, HEAD_DIM), lambda b: (b, j, 0, 0, 0))
    else:
        cache_spec = pl.BlockSpec((CTX_NB, DEPTH // 2, N_HEADS, SEQ, HEAD_DIM),
                                  lambda b: (b, 0, 0, 0, 0))
    in_specs = [
        pl.BlockSpec((rows, D), lambda b: (b, 0)),
        _mod_spec(layer, lambda b: 0),
        _const_spec((1, D), layer, 1),
        _const_spec((1, D), layer, 1),
        _const_spec((D, 3 * D), j),
        _const_spec((D, D), j),
    ]
    aliases = {0: 0}
    if caches:
        in_specs += [pl.BlockSpec(memory_space=pl.ANY)] * 2
        aliases.update({6: 1, 7: 2})
    return pl.pallas_call(
        functools.partial(_ctx_attn_body, fresh_slot=None if caches else j),
        grid=(BATCH // CTX_NB,),
        in_specs=in_specs,
        out_specs=[pl.BlockSpec((rows, D), lambda b: (b, 0)), cache_spec, cache_spec],
        out_shape=[jax.ShapeDtypeStruct((T_ALL, D), F32), cache, cache],
        input_output_aliases=aliases,
        compiler_params=_cparams(1),
        name="context_attention",
    )(x, mods, pre, post, w_qkv, w_out, *caches)


def _qkv_lat_body(x_ref, mod_ref, gpre_ref, wq_ref, wkt_ref, wv_ref, q_ref, kt_ref, v_ref):
    h = _prenorm(x_ref[...], gpre_ref[...], mod_ref, 1)
    q_ref[...] = (_dot(h, wq_ref[...]) * Q_SCALE).astype(BF16)
    v_ref[...] = _dot(h, wv_ref[...]).astype(BF16)
    kt_ref[...] = _dot_nt(wkt_ref[...], h).astype(BF16)


def _qkv_latent(x, mods, pre, w_qkv, wkt, layer):
    j = layer // 2
    per_lat = LAT // QKV_TM
    first = GROUP_ROWS // QKV_TM
    tok = jax.ShapeDtypeStruct((N_LAT * LAT, D), BF16)

    def w_cols(col_block):
        return pl.BlockSpec((None, D, D), lambda i: (j, 0, col_block), pipeline_mode=pl.Buffered(1))

    return pl.pallas_call(
        _qkv_lat_body,
        grid=(N_LAT * per_lat,),
        in_specs=[
            pl.BlockSpec((QKV_TM, D), lambda i: (first + i, 0)),
            _mod_spec(layer, lambda i: 1 + i // per_lat),
            _const_spec((1, D), layer, 1),
            w_cols(0),
            _const_spec((D, D), j),
            w_cols(2),
        ],
        out_specs=[
            pl.BlockSpec((QKV_TM, D), lambda i: (i, 0)),
            pl.BlockSpec((None, D, QKV_TM), lambda i: (i // per_lat, 0, i % per_lat)),
            pl.BlockSpec((QKV_TM, D), lambda i: (i, 0)),
        ],
        out_shape=[tok, jax.ShapeDtypeStruct((N_LAT, D, LAT), BF16), tok],
        compiler_params=_cparams(1),
        name="qkv_latent",
    )(x, mods, pre, w_qkv, wkt, w_qkv)


def _na_block_geometry(blk):
    return min(max(NA_ROWS * blk - WIN_ROWS // 2, 0), GRID - NA_KROWS)


def _na_variant(blk):
    return 0 if blk == 0 else (2 if blk == NA_BLOCKS - 1 else 1)


def _na_tables():
    t = np.arange(NA_DT)[:, None]
    e = np.arange(2)[None, :]
    drow = t - 11 + e
    row_ok = (drow >= -(WIN_ROWS - 1)) & (drow <= WIN_ROWS - 1)
    c = np.arange(GRID)[:, None]
    cp = np.arange(GRID)[None, :]
    cs = np.clip(c - WIN_COLS // 2, 0, GRID - WIN_COLS)
    col_ok = (cp >= cs) & (cp < cs + WIN_COLS)
    ok = row_ok[:, None, :, None] & col_ok[None, :, None, :]
    rowmask = np.zeros((NA_MASK_ROWS, NA_K), np.float32)
    delta0 = np.zeros((NA_VARIANTS, NA_ROWS, NA_KROWS // 2), np.int64)
    seen = set()
    for blk in range(NA_BLOCKS):
        v = _na_variant(blk)
        u0 = _na_block_geometry(blk)
        for rl in range(NA_ROWS):
            r = NA_ROWS * blk + rl
            rs = min(max(r - WIN_ROWS // 2, 0), GRID - WIN_ROWS)
            kr = u0 + np.arange(NA_KROWS)
            valid = (kr >= rs) & (kr < rs + WIN_ROWS)
            mask = np.repeat(np.where(valid, 0.0, NEG), GRID).astype(np.float32)
            d0 = u0 - r + 2 * np.arange(NA_KROWS // 2) + 11
            if (v, rl) in seen:
                assert (rowmask[v * NA_ROWS + rl] == mask).all() and (delta0[v, rl] == d0).all()
            seen.add((v, rl))
            rowmask[v * NA_ROWS + rl] = mask
            delta0[v, rl] = d0
    assert delta0.min() >= 0 and delta0.max() < NA_DT
    pairmask = np.where(ok, 0.0, NEG).astype(np.float32).reshape(NA_DT, GRID, 2 * GRID)
    return pairmask, rowmask, delta0


def _na_bias_rows(rpb):
    nrow = 2 * WIN_ROWS - 1
    fill = jnp.full((N_HEADS, nrow, 2 * GRID - (2 * WIN_COLS - 1)), NEG, F32)
    p = jnp.concatenate([rpb[:, :, WIN_COLS - 1:] * LOG2E, fill, rpb[:, :, :WIN_COLS - 1] * LOG2E],
                        axis=-1)
    return jnp.concatenate([p, jnp.full((N_HEADS, 1, 2 * GRID), NEG, F32)], axis=1)


def _na_expand_bias(p_ref, pm_ref, rm_ref, pair_scr, bias_scr, delta0):
    nrow = 2 * WIN_ROWS - 1
    left = lax.broadcasted_iota(jnp.int32, (1, 2 * GRID), 1) < GRID
    for hh in range(2):
        toep = [pltpu.roll(jnp.broadcast_to(p_ref[hh, i:i + 1, :], (GRID, 2 * GRID)), 0, 1,
                           stride=1, stride_axis=0) for i in range(nrow + 1)]
        shifted = [pltpu.roll(t, GRID, 1) for t in toep]
        for t in range(NA_DT):
            i0, i1 = (d + WIN_ROWS - 1 if abs(d) < WIN_ROWS else nrow for d in (t - 11, t - 10))
            pair_scr[hh, t] = jnp.where(left, toep[i0], shifted[i1]) + pm_ref[t]
        for var in range(NA_VARIANTS):
            for rl in range(NA_ROWS):
                strip = jnp.concatenate(
                    [pair_scr[hh, int(delta0[var, rl, i])] for i in range(NA_KROWS // 2)], axis=1)
                row = var * NA_ROWS + rl
                bias_scr[hh, var, rl * GRID:(rl + 1) * GRID, :] = strip + rm_ref[row:row + 1, :]


def _natten_body(q_ref, kt_ref, v_ref, kct_ref, vc_ref, p_ref, pm_ref, rm_ref, o_ref,
                 pair_scr, bias_scr, s_scr, p_scr, *, delta0):
    @pl.when(pl.program_id(1) == 0)
    def _():
        _na_expand_bias(p_ref, pm_ref, rm_ref, pair_scr, bias_scr, delta0)

    first = lax.broadcasted_iota(jnp.int32, (1, 2 * HEAD_DIM), 1) < HEAD_DIM
    zero = jnp.zeros((), BF16)
    kct = kct_ref[...]
    vc = vc_ref[...]

    keeps = (first, jnp.logical_not(first))

    def offsets(blk):
        koff = pl.multiple_of(jnp.clip(blk - 1, 0, NA_BLOCKS - 3) * NA_Q, NA_Q)
        qoff = pl.multiple_of(blk * NA_Q, NA_Q)
        return koff, qoff

    def logits(blk, slot):
        koff, qoff = offsets(blk)
        var = jnp.where(blk == 0, 0, jnp.where(blk == NA_BLOCKS - 1, 2, 1))
        qb = q_ref[pl.ds(qoff, NA_Q), :]
        ktb = kt_ref[:, pl.ds(koff, NA_K)]
        for hh in range(2):
            qh = jnp.where(keeps[hh], qb, zero)
            s_scr[slot, hh, :, :NA_K] = _dot(qh, ktb) + bias_scr[hh, var]
            s_scr[slot, hh, :, NA_K:] = _dot(qh, kct)

    def softmax(slot):
        for hh in range(2):
            s = s_scr[slot, hh]
            p_scr[slot, hh] = jnp.exp2(s - jnp.max(s, axis=-1, keepdims=True)).astype(BF16)

    lane_id = lax.broadcasted_iota(jnp.int32, (1, 2 * HEAD_DIM), 1)

    def values(blk, slot):
        koff, qoff = offsets(blk)
        vb = v_ref[pl.ds(koff, NA_K), :]
        o = jnp.zeros((NA_Q, 2 * HEAD_DIM), F32)
        for hh in range(2):
            den_lane = HEAD_DIM if hh == 0 else 0
            ones_col = jnp.where(lane_id == den_lane, 1.0, 0.0).astype(BF16)
            oh = (_dot(p_scr[slot, hh, :, :NA_K], jnp.where(keeps[hh], vb, ones_col))
                  + _dot(p_scr[slot, hh, :, NA_K:], jnp.where(keeps[hh], vc, ones_col)))
            o = o + jnp.where(keeps[hh], oh / oh[:, den_lane:den_lane + 1], 0.0)
        o_ref[pl.ds(qoff, NA_Q), :] = o.astype(BF16)

    logits(0, 0)
    softmax(0)
    logits(1, 1)

    def steady(it, carry):
        t = 2 + 2 * it
        logits(t, 0)
        values(t - 2, 0)
        softmax(1)
        logits(t + 1, 1)
        values(t - 1, 1)
        softmax(0)
        return carry

    lax.fori_loop(0, (NA_BLOCKS - 2) // 2, steady, 0, unroll=True)
    values(NA_BLOCKS - 2, 0)
    softmax(1)
    values(NA_BLOCKS - 1, 1)


def _neighbourhood_attention(q, kt, v, kct, vc, bias_rows, pairmask, rowmask, delta0):
    lane = 2 * HEAD_DIM
    return pl.pallas_call(
        functools.partial(_natten_body, delta0=delta0),
        grid=(N_PAIRS, N_LAT),
        in_specs=[
            pl.BlockSpec((LAT, lane), lambda p, b: (b, p)),
            pl.BlockSpec((None, lane, LAT), lambda p, b: (b, p, 0)),
            pl.BlockSpec((LAT, lane), lambda p, b: (b, p)),
            pl.BlockSpec((None, lane, PAST), lambda p, b: (b, p, 0)),
            pl.BlockSpec((None, PAST, lane), lambda p, b: (b, 0, p)),
            pl.BlockSpec((2, 2 * WIN_ROWS, lane), lambda p, b: (p, 0, 0)),
            pl.BlockSpec((NA_DT, GRID, lane), lambda p, b: (0, 0, 0)),
            pl.BlockSpec((NA_MASK_ROWS, NA_K), lambda p, b: (0, 0)),
        ],
        out_specs=pl.BlockSpec((LAT, lane), lambda p, b: (b, p)),
        out_shape=jax.ShapeDtypeStruct((N_LAT * LAT, D), BF16),
        scratch_shapes=[
            pltpu.VMEM((2, NA_DT, GRID, lane), F32),
            pltpu.VMEM((2, NA_VARIANTS, NA_Q, NA_K), F32),
            pltpu.VMEM((2, 2, NA_Q, NA_K + PAST), F32),
            pltpu.VMEM((2, 2, NA_Q, NA_K + PAST), BF16),
        ],
        compiler_params=_cparams(2),
        name="neighbourhood_attention",
    )(q, kt, v, kct, vc, bias_rows, pairmask, rowmask)


def kernel(x_prompt, x_sample, cache_k, cache_v, c, c_ctx, w_mod, b_mod, norm_pre, norm_post,
           ffn_w1, ffn_w2, four_w_in, four_w_out, na_w_qkv, na_w_out, na_rpb):
    conds = jnp.concatenate([c_ctx[None, :], c, jnp.zeros((8 - 1 - N_LAT, D), F32)], axis=0)
    mods = _modulation(conds, w_mod, b_mod)[:, :N_GROUPS].reshape(DEPTH, N_GROUPS, N_MOD, D)

    chan, dft_prompt, m1, m2, tw = _fourier_constants()
    pairmask, rowmask, delta0 = _na_tables()
    pairmask = jnp.asarray(pairmask)
    rowmask = jnp.asarray(rowmask)

    w_in = four_w_in.astype(BF16)
    wf = _fold_channel_dft(chan, four_w_out)
    w_qkv = na_w_qkv.astype(BF16)
    w_out = na_w_out.astype(BF16)
    wkt = w_qkv[:, :, D:2 * D].transpose(0, 2, 1)
    pre = norm_pre[:, :, None, :]
    post = norm_post[:, :, None, :]

    n_ctx = BATCH * SEQ
    n_lat = N_LAT * LAT
    ffn = functools.partial(_half_ffn, mods=mods, pre=pre, post=post)
    w1b = ffn_w1[0, 0].astype(BF16)
    w2b = ffn_w2[0, 0].astype(BF16)
    caches = ()
    for i in range(DEPTH):
        j = i // 2
        nxt = (ffn_w1, ffn_w2, i, 1)
        if i == 0:
            x, w1b, w2b = ffn(x_prompt.reshape(n_ctx, D), w1b, w2b, layer=0, slot=0,
                              x_b=x_sample.reshape(n_lat, D), convert=nxt)
        else:
            x, w1b, w2b = ffn(x, w1b, w2b, layer=i, slot=0, convert=nxt)
        attn = None
        if i % 2 == 0:
            x = _fourier_prompt(x, mods, pre, post, w_in, dft_prompt, wf, i)
            x4 = x.reshape(N_GROUPS, GRID, GRID, D)
            y5 = _fourier_latent_stage1(x4, mods, pre, w_in, m1, tw, i)
            x4 = _fourier_latent_stage2(y5, x4, mods, post, m2, wf, i)
            x = x4.reshape(T_ALL, D)
        else:
            x, *caches = _context_attention(x, mods, pre, post, w_qkv, w_out, i, caches)
            q, kt, v = _qkv_latent(x, mods, pre, w_qkv, wkt, i)
            kct = cache_k[:, j].transpose(0, 1, 3, 2).reshape(N_LAT, D, PAST).astype(BF16)
            vct = cache_v[:, j].transpose(0, 2, 1, 3).reshape(N_LAT, PAST, D).astype(BF16)
            o = _neighbourhood_attention(q, kt, v, kct, vct, _na_bias_rows(na_rpb[j]), pairmask,
                                         rowmask, delta0)
            attn = (o, w_out)
        if i < DEPTH - 1:
            x, w1b, w2b = ffn(x, w1b, w2b, layer=i, slot=1, attn=attn,
                              convert=(ffn_w1, ffn_w2, i + 1, 0))

    last = DEPTH - 1
    y_prompt, _, _ = ffn(x, w1b, w2b, layer=last, slot=1, n_rows=n_ctx, dst_rows=n_ctx)
    y_sample, _, _ = ffn(x, w1b, w2b, layer=last, slot=1, first_row=n_ctx, n_rows=n_lat,
                         dst_rows=n_lat, dst_row=0, attn=attn)
    return (y_prompt.reshape(BATCH, SEQ, D), y_sample.reshape(N_LAT, LAT, D), caches[0], caches[1])
```

```python
import functools

import numpy as np
import jax
import jax.numpy as jnp
from jax import lax
from jax.experimental import pallas as pl
from jax.experimental.pallas import tpu as pltpu

F32 = jnp.float32
BF16 = jnp.bfloat16

D = 1024
D_FF = 2816
DEPTH = 4
N_MOD = 9
N_HEADS = 16
HEAD_DIM = 64
N_PAIRS = N_HEADS // 2
SEQ = 256
BATCH = 16
GRID = 64
LAT = GRID * GRID
N_LAT = 2
GROUP_ROWS = 4096
N_GROUPS = 3
T_ALL = N_GROUPS * GROUP_ROWS
PAST = 256
WIN_ROWS = 8
WIN_COLS = 16
N_FG = 8
FG_DIM = D // N_FG
EPS = 1e-6
NEG = -1e30
LOG2E = 1.4426950408889634
Q_SCALE = HEAD_DIM ** -0.5 * LOG2E

VMEM_LIMIT = 56 * 1024 * 1024

FFN_TILE_LARGE = (1024, 256)
FFN_TILE_SMALL = (512, 256)
FFN_NC = 1
FFN_CONV_SLABS = 11
MOD_TN = 2304
FOLD_ROWS = 512
QKV_TM = 1024
FP_NB = 4
CTX_NB = 1
FS_TN = 16
FS_SUB = 8
NA_ROWS = 4
NA_Q = NA_ROWS * GRID
NA_KROWS = 12
NA_K = NA_KROWS * GRID
NA_BLOCKS = GRID // NA_ROWS
NA_VARIANTS = 3
NA_MASK_ROWS = 16
NA_DT = 22


def _cparams(n_axes):
    return pltpu.CompilerParams(
        dimension_semantics=("arbitrary",) * n_axes, vmem_limit_bytes=VMEM_LIMIT)


def _const_spec(shape, *lead):
    nd = len(shape)
    return pl.BlockSpec((None,) * len(lead) + tuple(shape), lambda *_: tuple(lead) + (0,) * nd,
                        pipeline_mode=pl.Buffered(1))


def _mod_spec(layer, group_of):
    return pl.BlockSpec((None, None, N_MOD, D), lambda *ids: (layer, group_of(*ids), 0, 0))


def _silu(x):
    return x * jax.nn.sigmoid(x)


def _prenorm(x, g, mod_ref, k):
    shift = mod_ref[3 * k:3 * k + 1, :]
    scale = mod_ref[3 * k + 1:3 * k + 2, :]
    y = x * lax.rsqrt(jnp.mean(x * x, axis=-1, keepdims=True) + EPS)
    return ((y * g) * (1.0 + scale) + shift).astype(BF16)


def _postnorm(y, g):
    return (y * lax.rsqrt(jnp.mean(y * y, axis=-1, keepdims=True) + EPS)) * g


def _dot(a, b):
    return jnp.dot(a, b, preferred_element_type=F32)


def _dot_nt(a, b):
    return lax.dot_general(a, b, (((1,), (1,)), ((), ())), preferred_element_type=F32)


def _mod_body(cond_ref, w_ref, b_ref, o_ref):
    s = _silu(cond_ref[...]).astype(BF16)
    o_ref[...] = _dot(s, w_ref[...].astype(BF16)) + b_ref[...]


def _modulation(conds, w_mod, b_mod):
    n = N_MOD * D
    return pl.pallas_call(
        _mod_body,
        grid=(DEPTH, n // MOD_TN),
        in_specs=[
            pl.BlockSpec((8, D), lambda l, j: (0, 0)),
            pl.BlockSpec((None, D, MOD_TN), lambda l, j: (l, 0, j)),
            pl.BlockSpec((None, 1, MOD_TN), lambda l, j: (l, 0, j)),
        ],
        out_specs=pl.BlockSpec((None, 8, MOD_TN), lambda l, j: (l, 0, j)),
        out_shape=jax.ShapeDtypeStruct((DEPTH, 8, n), F32),
        compiler_params=_cparams(2),
        name="modulation",
    )(conds, w_mod, b_mod.reshape(DEPTH, 1, n))


def _ffn_tile(load_x, mod_ref, gpre_ref, gpost_ref, w1_ref, w2_ref, o_ref, k, tile):
    ck = D_FF // FFN_NC
    gate = mod_ref[3 * k + 2:3 * k + 3, :]
    tm, sub = tile
    for r in range(tm // sub):
        rows = slice(r * sub, (r + 1) * sub)
        x = load_x(rows)
        h = _prenorm(x, gpre_ref[...], mod_ref, k)
        y = None
        for c in range(FFN_NC):
            g = _dot(h, w1_ref[:, c * ck:(c + 1) * ck])
            u = _dot(h, w1_ref[:, D_FF + c * ck:D_FF + (c + 1) * ck])
            a = (_silu(g) * u).astype(BF16)
            yc = _dot(a, w2_ref[c * ck:(c + 1) * ck, :])
            y = yc if y is None else y + yc
        o_ref[rows, :] = x + (0.5 * gate) * _postnorm(y, gpost_ref[...])


def _ffn_body(*refs, k, tile, tile0, tiles_a, attn, convert):
    refs = list(refs)
    x_ref = refs.pop(0)
    xb_ref = refs.pop(0) if tiles_a is not None else None
    ao_ref = refs.pop(0) if attn else None
    mod_ref, gpre_ref, gpost_ref, w1_ref, w2_ref = refs[:5]
    refs = refs[5:]
    if attn:
        gmix_ref, wout_ref = refs[:2]
        refs = refs[2:]
    if convert:
        nw1_ref, nw2_ref, o_ref, nw1b_ref, nw2b_ref = refs
        nw1b_ref[...] = nw1_ref[...].astype(BF16)
        nw2b_ref[...] = nw2_ref[...].astype(BF16)
    else:
        (o_ref,) = refs
    tile_id = tile0 + pl.program_id(0)
    tail = (mod_ref, gpre_ref, gpost_ref, w1_ref, w2_ref, o_ref, k, tile)

    def plain(rows):
        if xb_ref is None:
            return x_ref[rows, :]
        return jnp.where(tile_id < tiles_a, x_ref[rows, :], xb_ref[rows, :])

    def mixed(rows):
        y = _dot(ao_ref[rows, :], wout_ref[...])
        return x_ref[rows, :] + mod_ref[5:6, :] * _postnorm(y, gmix_ref[...])

    ctx_tiles = GROUP_ROWS // tile[0]
    if not attn:
        _ffn_tile(plain, *tail)
    elif tile0 >= ctx_tiles:
        _ffn_tile(mixed, *tail)
    else:
        pl.when(tile_id < ctx_tiles)(lambda: _ffn_tile(plain, *tail))
        pl.when(tile_id >= ctx_tiles)(lambda: _ffn_tile(mixed, *tail))


def _half_ffn(x, w1b, w2b, mods, pre, post, layer, slot, *, x_b=None, first_row=0, n_rows=T_ALL,
              dst_rows=T_ALL, dst_row=None, attn=None, convert=None):
    k = 2 * slot
    tile = FFN_TILE_LARGE if (x_b is None and not attn) else FFN_TILE_SMALL
    tm = tile[0]
    tiles_per_group = GROUP_ROWS // tm
    n_tiles = n_rows // tm
    t0 = first_row // tm
    d0 = t0 if dst_row is None else dst_row // tm
    tiles_a = None
    in_specs = [pl.BlockSpec((tm, D), lambda i: (t0 + i, 0))]
    args = [x]
    if x_b is not None:
        tiles_a = x.shape[0] // tm
        in_specs = [pl.BlockSpec((tm, D), lambda i: (jnp.minimum(i, tiles_a - 1), 0)),
                    pl.BlockSpec((tm, D), lambda i: (jnp.maximum(i - tiles_a, 0), 0))]
        args.append(x_b)
    if attn:
        in_specs.append(pl.BlockSpec(
            (tm, D), lambda i: (jnp.maximum(t0 + i - tiles_per_group, 0), 0)))
        args.append(attn[0])
    in_specs += [
        _mod_spec(layer, lambda i: (t0 + i) // tiles_per_group),
        _const_spec((1, D), layer, k),
        _const_spec((1, D), layer, k),
        _const_spec((D, 2 * D_FF)),
        _const_spec((D_FF, D)),
    ]
    args += [mods, pre, post, w1b, w2b]
    if attn:
        in_specs += [_const_spec((1, D), layer, 1), _const_spec((D, D), layer // 2)]
        args += [post, attn[1]]
    out_specs = [pl.BlockSpec((tm, D), lambda i: (d0 + i, 0))]
    out_shape = [jax.ShapeDtypeStruct((dst_rows, D), F32)]
    if convert:
        nw1, nw2, nl, ns = convert
        slabs = FFN_CONV_SLABS * (2 if n_tiles >= 2 * FFN_CONV_SLABS else 1)
        assert n_tiles >= slabs
        c1, r2 = 2 * D_FF // slabs, D_FF // slabs
        last = slabs - 1
        in_specs += [
            pl.BlockSpec((None, None, D, c1), lambda i: (nl, ns, 0, jnp.minimum(i, last))),
            pl.BlockSpec((None, None, r2, D), lambda i: (nl, ns, jnp.minimum(i, last), 0)),
        ]
        args += [nw1, nw2]
        out_specs += [pl.BlockSpec((D, c1), lambda i: (0, jnp.minimum(i, last))),
                      pl.BlockSpec((r2, D), lambda i: (jnp.minimum(i, last), 0))]
        out_shape += [jax.ShapeDtypeStruct((D, 2 * D_FF), BF16),
                      jax.ShapeDtypeStruct((D_FF, D), BF16)]
    res = pl.pallas_call(
        functools.partial(_ffn_body, k=k, tile=tile, tile0=t0, tiles_a=tiles_a, attn=bool(attn),
                          convert=bool(convert)),
        grid=(n_tiles,),
        in_specs=in_specs,
        out_specs=out_specs,
        out_shape=out_shape,
        compiler_params=_cparams(1),
        name="half_ffn",
    )(*args)
    return (res[0], res[1], res[2]) if convert else (res[0], None, None)


def _dft_cos_sin(n):
    idx = np.arange(n)
    ang = 2.0 * np.pi * ((idx[:, None] * idx[None, :]) % n) / n
    return np.cos(ang), np.sin(ang)


def _fourier_constants():
    c128, s128 = _dft_cos_sin(FG_DIM)
    chan = np.stack([c128, s128]) / np.sqrt(FG_DIM)

    c256, s256 = _dft_cos_sin(SEQ)
    dft_prompt = np.concatenate([c256, -s256], axis=0) / np.sqrt(SEQ)

    c64, s64 = _dft_cos_sin(GRID)
    c64 = c64 / np.sqrt(GRID)
    s64 = s64 / np.sqrt(GRID)
    eye = np.eye(FS_SUB)
    rows = GRID * FS_SUB
    m1 = np.concatenate([np.kron(c64, eye), np.kron(-s64, eye)], axis=0)
    def spread(f):
        return np.einsum("kn,jl->kjln", f, eye).reshape(rows, rows)
    m2 = np.block([[spread(c64), spread(s64)], [spread(-s64), spread(c64)]])
    k1 = np.arange(GRID)[:, None]
    n2 = np.arange(GRID)[None, :]
    ang = 2.0 * np.pi * (k1 * n2) / LAT
    tw = np.stack([np.cos(ang), np.sin(ang)])[..., None] * np.ones((1, 1, 1, 128))

    def mxu_const(a):
        return jnp.asarray(a, F32).astype(BF16)

    return (jnp.asarray(chan, F32), mxu_const(dft_prompt), mxu_const(m1), mxu_const(m2),
            jnp.asarray(tw, F32))


def _fold_body(cs_ref, w_ref, o_ref):
    for g in range(FOLD_ROWS // FG_DIM):
        rows = slice(g * FG_DIM, (g + 1) * FG_DIM)
        w = w_ref[rows, :]
        for t in range(2):
            o_ref[t, rows, :] = jnp.dot(cs_ref[t], w, preferred_element_type=F32,
                                        precision=lax.Precision.HIGHEST).astype(BF16)


def _fold_channel_dft(chan, w_out):
    n_layers = w_out.shape[0]
    return pl.pallas_call(
        _fold_body,
        grid=(n_layers, D // FOLD_ROWS),
        in_specs=[
            pl.BlockSpec((2, FG_DIM, FG_DIM), lambda l, r: (0, 0, 0)),
            pl.BlockSpec((None, FOLD_ROWS, D), lambda l, r: (l, r, 0)),
        ],
        out_specs=pl.BlockSpec((None, 2, FOLD_ROWS, D), lambda l, r: (l, 0, r, 0)),
        out_shape=jax.ShapeDtypeStruct((n_layers, 2, D, D), BF16),
        compiler_params=_cparams(2),
        name="fold_channel_dft",
    )(chan, w_out)


def _four_prompt_body(x_ref, mod_ref, gpre_ref, gpost_ref, win_ref, dft_ref, w2_ref, o_ref):
    x = x_ref[...]
    h = _prenorm(x, gpre_ref[...], mod_ref, 1)
    u = _dot(h, win_ref[...]).astype(BF16)
    p = [_dot(dft_ref[...], u[nb * SEQ:(nb + 1) * SEQ]) for nb in range(FP_NB)]
    re = jnp.concatenate([t[:SEQ] for t in p], axis=0).astype(BF16)
    im = jnp.concatenate([t[SEQ:] for t in p], axis=0).astype(BF16)
    y = _dot(re, w2_ref[0]) + _dot(im, w2_ref[1])
    o_ref[...] = x + mod_ref[5:6, :] * _postnorm(y, gpost_ref[...])


def _fourier_prompt(x, mods, pre, post, w_in, dft_prompt, w2, layer):
    return pl.pallas_call(
        _four_prompt_body,
        grid=(BATCH // FP_NB,),
        in_specs=[
            pl.BlockSpec((FP_NB * SEQ, D), lambda b: (b, 0)),
            _mod_spec(layer, lambda b: 0),
            _const_spec((1, D), layer, 1),
            _const_spec((1, D), layer, 1),
            _const_spec((D, D), layer // 2),
            _const_spec((2 * SEQ, SEQ)),
            _const_spec((2, D, D), layer // 2),
        ],
        out_specs=pl.BlockSpec((FP_NB * SEQ, D), lambda b: (b, 0)),
        out_shape=jax.ShapeDtypeStruct((T_ALL, D), F32),
        input_output_aliases={0: 0},
        compiler_params=_cparams(1),
        name="fourier_prompt",
    )(x, mods, pre, post, w_in, dft_prompt, w2)


def _four_s1_body(x_ref, mod_ref, gpre_ref, win_ref, m1_ref, tw_ref, y_ref, y_scr):
    rows = GRID * FS_SUB
    for s in range(FS_TN // FS_SUB):
        cols = slice(s * FS_SUB, (s + 1) * FS_SUB)
        x = x_ref[:, cols, :].reshape(rows, D)
        h = _prenorm(x, gpre_ref[...], mod_ref, 1)
        u = _dot(h, win_ref[...]).astype(BF16)
        y = _dot(m1_ref[...], u)
        tc = tw_ref[0, :, cols, :].reshape(rows, 128)
        ts = tw_ref[1, :, cols, :].reshape(rows, 128)
        for l in range(D // 128):
            sl = slice(l * 128, (l + 1) * 128)
            yr = y[:rows, sl]
            yi = y[rows:, sl]
            y_scr[0, :, cols, sl] = (yr * tc + yi * ts).reshape(GRID, FS_SUB, 128)
            y_scr[1, :, cols, sl] = (yi * tc - yr * ts).reshape(GRID, FS_SUB, 128)
    y_ref[...] = y_scr[...].astype(BF16)


def _fourier_latent_stage1(x4, mods, pre, w_in, m1, tw, layer):
    rows = GRID * FS_SUB
    return pl.pallas_call(
        _four_s1_body,
        grid=(N_LAT, GRID // FS_TN),
        in_specs=[
            pl.BlockSpec((None, GRID, FS_TN, D), lambda b, j: (b + 1, 0, j, 0)),
            _mod_spec(layer, lambda b, j: b + 1),
            _const_spec((1, D), layer, 1),
            _const_spec((D, D), layer // 2),
            _const_spec((2 * rows, rows)),
            pl.BlockSpec((2, GRID, FS_TN, 128), lambda b, j: (0, 0, j, 0)),
        ],
        out_specs=pl.BlockSpec((None, 2, GRID, FS_TN, D), lambda b, j: (b, 0, 0, j, 0)),
        out_shape=jax.ShapeDtypeStruct((N_LAT, 2, GRID, GRID, D), BF16),
        scratch_shapes=[pltpu.VMEM((2, GRID, FS_TN, D), F32)],
        compiler_params=_cparams(2),
        name="fourier_latent_stage1",
    )(x4, mods, pre, w_in, m1, tw)


def _four_s2_body(y_ref, x_ref, mod_ref, gpost_ref, m2_ref, w2_ref, o_ref):
    rows = GRID * FS_SUB
    yin = y_ref[...].reshape(2 * rows, D)
    z = _dot(m2_ref[...], yin)
    y = _dot(z[:rows].astype(BF16), w2_ref[0]) + _dot(z[rows:].astype(BF16), w2_ref[1])
    x = x_ref[...].reshape(rows, D)
    out = x + mod_ref[5:6, :] * _postnorm(y, gpost_ref[...])
    o_ref[...] = out.reshape(GRID, FS_SUB, D)


def _fourier_latent_stage2(y5, x4, mods, post, m2, w2, layer):
    rows = GRID * FS_SUB
    return pl.pallas_call(
        _four_s2_body,
        grid=(N_LAT, GRID // FS_SUB),
        in_specs=[
            pl.BlockSpec((None, 2, FS_SUB, GRID, D), lambda b, j: (b, 0, j, 0, 0)),
            pl.BlockSpec((None, GRID, FS_SUB, D), lambda b, j: (b + 1, 0, j, 0)),
            _mod_spec(layer, lambda b, j: b + 1),
            _const_spec((1, D), layer, 1),
            _const_spec((2 * rows, 2 * rows)),
            _const_spec((2, D, D), layer // 2),
        ],
        out_specs=pl.BlockSpec((None, GRID, FS_SUB, D), lambda b, j: (b + 1, 0, j, 0)),
        out_shape=jax.ShapeDtypeStruct((N_GROUPS, GRID, GRID, D), F32),
        input_output_aliases={1: 0},
        compiler_params=_cparams(2),
        name="fourier_latent_stage2",
    )(y5, x4, mods, post, m2, w2)


def _ctx_attn_body(x_ref, mod_ref, gpre_ref, gpost_ref, wqkv_ref, wout_ref, *rest, fresh_slot):
    o_ref, kc_ref, vc_ref = rest[-3:]
    x = x_ref[...]
    h = _prenorm(x, gpre_ref[...], mod_ref, 1)
    qkv = _dot(h, wqkv_ref[...])
    k = qkv[:, D:2 * D]
    v = qkv[:, 2 * D:]
    for nb in range(CTX_NB):
        rows = slice(nb * SEQ, (nb + 1) * SEQ)
        for hd in range(N_HEADS):
            sl = slice(hd * HEAD_DIM, (hd + 1) * HEAD_DIM)
            if fresh_slot is None:
                kc_ref[nb, hd] = k[rows, sl]
                vc_ref[nb, hd] = v[rows, sl]
            else:
                kc_ref[nb, fresh_slot, hd] = k[rows, sl]
                vc_ref[nb, fresh_slot, hd] = v[rows, sl]
        if fresh_slot is not None:
            other = jnp.zeros((N_HEADS, SEQ, HEAD_DIM), F32)
            kc_ref[nb, 1 - fresh_slot] = other
            vc_ref[nb, 1 - fresh_slot] = other
    qb = (qkv[:, :D] * Q_SCALE).astype(BF16)
    kb = k.astype(BF16)
    vb = v.astype(BF16)
    first = lax.broadcasted_iota(jnp.int32, (1, 2 * HEAD_DIM), 1) < HEAD_DIM
    zero = jnp.zeros((), BF16)
    seqs = []
    for nb in range(CTX_NB):
        rows = slice(nb * SEQ, (nb + 1) * SEQ)
        outs = []
        for p in range(N_PAIRS):
            sl = slice(p * 2 * HEAD_DIM, (p + 1) * 2 * HEAD_DIM)
            kp = kb[rows, sl]
            vp = vb[rows, sl]
            k2 = jnp.concatenate([jnp.where(first, kp, zero), jnp.where(first, zero, kp)], axis=0)
            v2 = jnp.concatenate([jnp.where(first, vp, zero), jnp.where(first, zero, vp)], axis=0)
            s = _dot_nt(qb[rows, sl], k2)
            probs = []
            for hh in range(2):
                sh = s[:, hh * SEQ:(hh + 1) * SEQ]
                e = jnp.exp2(sh - jnp.max(sh, axis=-1, keepdims=True))
                probs.append(e / jnp.sum(e, axis=-1, keepdims=True))
            outs.append(_dot(jnp.concatenate(probs, axis=1).astype(BF16), v2))
        seqs.append(jnp.concatenate(outs, axis=1).astype(BF16))
    y = _dot(jnp.concatenate(seqs, axis=0), wout_ref[...])
    o_ref[...] = x + mod_ref[5:6, :] * _postnorm(y, gpost_ref[...])


def _context_attention(x, mods, pre, post, w_qkv, w_out, layer, caches):
    j = layer // 2
    rows = CTX_NB * SEQ
    cache = jax.ShapeDtypeStruct((BATCH, DEPTH // 2, N_HEADS, SEQ, HEAD_DIM), F32)
    if caches:
        cache_spec = pl.BlockSpec((CTX_NB, None, N_HEADS, SEQ, HEAD_DIM), lambda b: (b, j, 0, 0, 0))
    else:
        cache_spec = pl.BlockSpec((CTX_NB, DEPTH // 2, N_HEADS, SEQ, HEAD_DIM),
                                  lambda b: (b, 0, 0, 0, 0))
    in_specs = [
        pl.BlockSpec((rows, D), lambda b: (b, 0)),
        _mod_spec(layer, lambda b: 0),
        _const_spec((1, D), layer, 1),
        _const_spec((1, D), layer, 1),
        _const_spec((D, 3 * D), j),
        _const_spec((D, D), j),
    ]
    aliases = {0: 0}
    if caches:
        in_specs += [pl.BlockSpec(memory_space=pl.ANY)] * 2
        aliases.update({6: 1, 7: 2})
    return pl.pallas_call(
        functools.partial(_ctx_attn_body, fresh_slot=None if caches else j),
        grid=(BATCH // CTX_NB,),
        in_specs=in_specs,
        out_specs=[pl.BlockSpec((rows, D), lambda b: (b, 0)), cache_spec, cache_spec],
        out_shape=[jax.ShapeDtypeStruct((T_ALL, D), F32), cache, cache],
        input_output_aliases=aliases,
        compiler_params=_cparams(1),
        name="context_attention",
    )(x, mods, pre, post, w_qkv, w_out, *caches)


def _qkv_lat_body(x_ref, mod_ref, gpre_ref, wq_ref, wkt_ref, wv_ref, q_ref, kt_ref, v_ref):
    h = _prenorm(x_ref[...], gpre_ref[...], mod_ref, 1)
    q_ref[...] = (_dot(h, wq_ref[...]) * Q_SCALE).astype(BF16)
    v_ref[...] = _dot(h, wv_ref[...]).astype(BF16)
    kt_ref[...] = _dot_nt(wkt_ref[...], h).astype(BF16)


def _qkv_latent(x, mods, pre, w_qkv, wkt, layer):
    j = layer // 2
    per_lat = LAT // QKV_TM
    first = GROUP_ROWS // QKV_TM
    tok = jax.ShapeDtypeStruct((N_LAT * LAT, D), BF16)

    def w_cols(col_block):
        return pl.BlockSpec((None, D, D), lambda i: (j, 0, col_block), pipeline_mode=pl.Buffered(1))

    return pl.pallas_call(
        _qkv_lat_body,
        grid=(N_LAT * per_lat,),
        in_specs=[
            pl.BlockSpec((QKV_TM, D), lambda i: (first + i, 0)),
            _mod_spec(layer, lambda i: 1 + i // per_lat),
            _const_spec((1, D), layer, 1),
            w_cols(0),
            _const_spec((D, D), j),
            w_cols(2),
        ],
        out_specs=[
            pl.BlockSpec((QKV_TM, D), lambda i: (i, 0)),
            pl.BlockSpec((None, D, QKV_TM), lambda i: (i // per_lat, 0, i % per_lat)),
            pl.BlockSpec((QKV_TM, D), lambda i: (i, 0)),
        ],
        out_shape=[tok, jax.ShapeDtypeStruct((N_LAT, D, LAT), BF16), tok],
        compiler_params=_cparams(1),
        name="qkv_latent",
    )(x, mods, pre, w_qkv, wkt, w_qkv)


def _na_block_geometry(blk):
    return min(max(NA_ROWS * blk - WIN_ROWS // 2, 0), GRID - NA_KROWS)


def _na_variant(blk):
    return 0 if blk == 0 else (2 if blk == NA_BLOCKS - 1 else 1)


def _na_tables():
    t = np.arange(NA_DT)[:, None]
    e = np.arange(2)[None, :]
    drow = t - 11 + e
    row_ok = (drow >= -(WIN_ROWS - 1)) & (drow <= WIN_ROWS - 1)
    c = np.arange(GRID)[:, None]
    cp = np.arange(GRID)[None, :]
    cs = np.clip(c - WIN_COLS // 2, 0, GRID - WIN_COLS)
    col_ok = (cp >= cs) & (cp < cs + WIN_COLS)
    ok = row_ok[:, None, :, None] & col_ok[None, :, None, :]
    rowmask = np.zeros((NA_MASK_ROWS, NA_K), np.float32)
    delta0 = np.zeros((NA_VARIANTS, NA_ROWS, NA_KROWS // 2), np.int64)
    seen = set()
    for blk in range(NA_BLOCKS):
        v = _na_variant(blk)
        u0 = _na_block_geometry(blk)
        for rl in range(NA_ROWS):
            r = NA_ROWS * blk + rl
            rs = min(max(r - WIN_ROWS // 2, 0), GRID - WIN_ROWS)
            kr = u0 + np.arange(NA_KROWS)
            valid = (kr >= rs) & (kr < rs + WIN_ROWS)
            mask = np.repeat(np.where(valid, 0.0, NEG), GRID).astype(np.float32)
            d0 = u0 - r + 2 * np.arange(NA_KROWS // 2) + 11
            if (v, rl) in seen:
                assert (rowmask[v * NA_ROWS + rl] == mask).all() and (delta0[v, rl] == d0).all()
            seen.add((v, rl))
            rowmask[v * NA_ROWS + rl] = mask
            delta0[v, rl] = d0
    assert delta0.min() >= 0 and delta0.max() < NA_DT
    pairmask = np.where(ok, 0.0, NEG).astype(np.float32).reshape(NA_DT, GRID, 2 * GRID)
    return pairmask, rowmask, delta0


def _na_bias_rows(rpb):
    nrow = 2 * WIN_ROWS - 1
    fill = jnp.full((N_HEADS, nrow, 2 * GRID - (2 * WIN_COLS - 1)), NEG, F32)
    p = jnp.concatenate([rpb[:, :, WIN_COLS - 1:] * LOG2E, fill, rpb[:, :, :WIN_COLS - 1] * LOG2E],
                        axis=-1)
    return jnp.concatenate([p, jnp.full((N_HEADS, 1, 2 * GRID), NEG, F32)], axis=1)


def _na_expand_bias(p_ref, pm_ref, rm_ref, pair_scr, bias_scr, delta0):
    nrow = 2 * WIN_ROWS - 1
    left = lax.broadcasted_iota(jnp.int32, (1, 2 * GRID), 1) < GRID
    for hh in range(2):
        toep = [pltpu.roll(jnp.broadcast_to(p_ref[hh, i:i + 1, :], (GRID, 2 * GRID)), 0, 1,
                           stride=1, stride_axis=0) for i in range(nrow + 1)]
        shifted = [pltpu.roll(t, GRID, 1) for t in toep]
        for t in range(NA_DT):
            i0, i1 = (d + WIN_ROWS - 1 if abs(d) < WIN_ROWS else nrow for d in (t - 11, t - 10))
            pair_scr[hh, t] = jnp.where(left, toep[i0], shifted[i1]) + pm_ref[t]
        for var in range(NA_VARIANTS):
            for rl in range(NA_ROWS):
                strip = jnp.concatenate(
                    [pair_scr[hh, int(delta0[var, rl, i])] for i in range(NA_KROWS // 2)], axis=1)
                row = var * NA_ROWS + rl
                bias_scr[hh, var, rl * GRID:(rl + 1) * GRID, :] = strip + rm_ref[row:row + 1, :]


def _natten_body(q_ref, kt_ref, v_ref, kct_ref, vc_ref, p_ref, pm_ref, rm_ref, o_ref,
                 pair_scr, bias_scr, s_scr, p_scr, *, delta0):
    @pl.when(pl.program_id(1) == 0)
    def _():
        _na_expand_bias(p_ref, pm_ref, rm_ref, pair_scr, bias_scr, delta0)

    first = lax.broadcasted_iota(jnp.int32, (1, 2 * HEAD_DIM), 1) < HEAD_DIM
    zero = jnp.zeros((), BF16)
    kct = kct_ref[...]
    vc = vc_ref[...]

    keeps = (first, jnp.logical_not(first))

    def offsets(blk):
        koff = pl.multiple_of(jnp.clip(blk - 1, 0, NA_BLOCKS - 3) * NA_Q, NA_Q)
        qoff = pl.multiple_of(blk * NA_Q, NA_Q)
        return koff, qoff

    def logits(blk, slot):
        koff, qoff = offsets(blk)
        var = jnp.where(blk == 0, 0, jnp.where(blk == NA_BLOCKS - 1, 2, 1))
        qb = q_ref[pl.ds(qoff, NA_Q), :]
        ktb = kt_ref[:, pl.ds(koff, NA_K)]
        for hh in range(2):
            qh = jnp.where(keeps[hh], qb, zero)
            s_scr[slot, hh, :, :NA_K] = _dot(qh, ktb) + bias_scr[hh, var]
            s_scr[slot, hh, :, NA_K:] = _dot(qh, kct)

    def softmax(slot):
        for hh in range(2):
            s = s_scr[slot, hh]
            p_scr[slot, hh] = jnp.exp2(s - jnp.max(s, axis=-1, keepdims=True)).astype(BF16)

    lane_id = lax.broadcasted_iota(jnp.int32, (1, 2 * HEAD_DIM), 1)

    def values(blk, slot):
        koff, qoff = offsets(blk)
        vb = v_ref[pl.ds(koff, NA_K), :]
        o = jnp.zeros((NA_Q, 2 * HEAD_DIM), F32)
        for hh in range(2):
            den_lane = HEAD_DIM if hh == 0 else 0
            ones_col = jnp.where(lane_id == den_lane, 1.0, 0.0).astype(BF16)
            oh = (_dot(p_scr[slot, hh, :, :NA_K], jnp.where(keeps[hh], vb, ones_col))
                  + _dot(p_scr[slot, hh, :, NA_K:], jnp.where(keeps[hh], vc, ones_col)))
            o = o + jnp.where(keeps[hh], oh / oh[:, den_lane:den_lane + 1], 0.0)
        o_ref[pl.ds(qoff, NA_Q), :] = o.astype(BF16)

    logits(0, 0)
    softmax(0)
    logits(1, 1)

    def steady(it, carry):
        t = 2 + 2 * it
        logits(t, 0)
        values(t - 2, 0)
        softmax(1)
        logits(t + 1, 1)
        values(t - 1, 1)
        softmax(0)
        return carry

    lax.fori_loop(0, (NA_BLOCKS - 2) // 2, steady, 0, unroll=True)
    values(NA_BLOCKS - 2, 0)
    softmax(1)
    values(NA_BLOCKS - 1, 1)


def _neighbourhood_attention(q, kt, v, kct, vc, bias_rows, pairmask, rowmask, delta0):
    lane = 2 * HEAD_DIM
    return pl.pallas_call(
        functools.partial(_natten_body, delta0=delta0),
        grid=(N_PAIRS, N_LAT),
        in_specs=[
            pl.BlockSpec((LAT, lane), lambda p, b: (b, p)),
            pl.BlockSpec((None, lane, LAT), lambda p, b: (b, p, 0)),
            pl.BlockSpec((LAT, lane), lambda p, b: (b, p)),
            pl.BlockSpec((None, lane, PAST), lambda p, b: (b, p, 0)),
            pl.BlockSpec((None, PAST, lane), lambda p, b: (b, 0, p)),
            pl.BlockSpec((2, 2 * WIN_ROWS, lane), lambda p, b: (p, 0, 0)),
            pl.BlockSpec((NA_DT, GRID, lane), lambda p, b: (0, 0, 0)),
            pl.BlockSpec((NA_MASK_ROWS, NA_K), lambda p, b: (0, 0)),
        ],
        out_specs=pl.BlockSpec((LAT, lane), lambda p, b: (b, p)),
        out_shape=jax.ShapeDtypeStruct((N_LAT * LAT, D), BF16),
        scratch_shapes=[
            pltpu.VMEM((2, NA_DT, GRID, lane), F32),
            pltpu.VMEM((2, NA_VARIANTS, NA_Q, NA_K), F32),
            pltpu.VMEM((2, 2, NA_Q, NA_K + PAST), F32),
            pltpu.VMEM((2, 2, NA_Q, NA_K + PAST), BF16),
        ],
        compiler_params=_cparams(2),
        name="neighbourhood_attention",
    )(q, kt, v, kct, vc, bias_rows, pairmask, rowmask)


def kernel(x_prompt, x_sample, cache_k, cache_v, c, c_ctx, w_mod, b_mod, norm_pre, norm_post,
           ffn_w1, ffn_w2, four_w_in, four_w_out, na_w_qkv, na_w_out, na_rpb):
    conds = jnp.concatenate([c_ctx[None, :], c, jnp.zeros((8 - 1 - N_LAT, D), F32)], axis=0)
    mods = _modulation(conds, w_mod, b_mod)[:, :N_GROUPS].reshape(DEPTH, N_GROUPS, N_MOD, D)

    chan, dft_prompt, m1, m2, tw = _fourier_constants()
    pairmask, rowmask, delta0 = _na_tables()
    pairmask = jnp.asarray(pairmask)
    rowmask = jnp.asarray(rowmask)

    w_in = four_w_in.astype(BF16)
    wf = _fold_channel_dft(chan, four_w_out)
    w_qkv = na_w_qkv.astype(BF16)
    w_out = na_w_out.astype(BF16)
    wkt = w_qkv[:, :, D:2 * D].transpose(0, 2, 1)
    pre = norm_pre[:, :, None, :]
    post = norm_post[:, :, None, :]

    n_ctx = BATCH * SEQ
    n_lat = N_LAT * LAT
    ffn = functools.partial(_half_ffn, mods=mods, pre=pre, post=post)
    w1b = ffn_w1[0, 0].astype(BF16)
    w2b = ffn_w2[0, 0].astype(BF16)
    caches = ()
    for i in range(DEPTH):
        j = i // 2
        nxt = (ffn_w1, ffn_w2, i, 1)
        if i == 0:
            x, w1b, w2b = ffn(x_prompt.reshape(n_ctx, D), w1b, w2b, layer=0, slot=0,
                              x_b=x_sample.reshape(n_lat, D), convert=nxt)
        else:
            x, w1b, w2b = ffn(x, w1b, w2b, layer=i, slot=0, convert=nxt)
        attn = None
        if i % 2 == 0:
            x = _fourier_prompt(x, mods, pre, post, w_in, dft_prompt, wf, i)
            x4 = x.reshape(N_GROUPS, GRID, GRID, D)
            y5 = _fourier_latent_stage1(x4, mods, pre, w_in, m1, tw, i)
            x4 = _fourier_latent_stage2(y5, x4, mods, post, m2, wf, i)
            x = x4.reshape(T_ALL, D)
        else:
            x, *caches = _context_attention(x, mods, pre, post, w_qkv, w_out, i, caches)
            q, kt, v = _qkv_latent(x, mods, pre, w_qkv, wkt, i)
            kct = cache_k[:, j].transpose(0, 1, 3, 2).reshape(N_LAT, D, PAST).astype(BF16)
            vct = cache_v[:, j].transpose(0, 2, 1, 3).reshape(N_LAT, PAST, D).astype(BF16)
            o = _neighbourhood_attention(q, kt, v, kct, vct, _na_bias_rows(na_rpb[j]), pairmask,
                                         rowmask, delta0)
            attn = (o, w_out)
        if i < DEPTH - 1:
            x, w1b, w2b = ffn(x, w1b, w2b, layer=i, slot=1, attn=attn,
                              convert=(ffn_w1, ffn_w2, i + 1, 0))

    last = DEPTH - 1
    y_prompt, _, _ = ffn(x, w1b, w2b, layer=last, slot=1, n_rows=n_ctx, dst_rows=n_ctx)
    y_sample, _, _ = ffn(x, w1b, w2b, layer=last, slot=1, first_row=n_ctx, n_rows=n_lat,
                         dst_rows=n_lat, dst_row=0, attn=attn)
    return (y_prompt.reshape(BATCH, SEQ, D), y_sample.reshape(N_LAT, LAT, D), caches[0], caches[1])
```

```python
import functools

import numpy as np
import jax
import jax.numpy as jnp
from jax import lax
from jax.experimental import pallas as pl
from jax.experimental.pallas import tpu as pltpu

F32 = jnp.float32
BF16 = jnp.bfloat16

D = 1024
D_FF = 2816
DEPTH = 4
N_MOD = 9
N_HEADS = 16
HEAD_DIM = 64
N_PAIRS = N_HEADS // 2
SEQ = 256
BATCH = 16
GRID = 64
LAT = GRID * GRID
N_LAT = 2
GROUP_ROWS = 4096
N_GROUPS = 3
T_ALL = N_GROUPS * GROUP_ROWS
PAST = 256
WIN_ROWS = 8
WIN_COLS = 16
N_FG = 8
FG_DIM = D // N_FG
EPS = 1e-6
NEG = -1e30
LOG2E = 1.4426950408889634
Q_SCALE = HEAD_DIM ** -0.5 * LOG2E

VMEM_LIMIT = 56 * 1024 * 1024

FFN_TILE_LARGE = (1024, 256)
FFN_TILE_SMALL = (512, 256)
FFN_NC = 1
FFN_CONV_SLABS = 11
MOD_TN = 2304
FOLD_ROWS = 512
QKV_TM = 1024
FP_NB = 4
CTX_NB = 1
FS_TN = 16
FS_SUB = 8
NA_ROWS = 4
NA_Q = NA_ROWS * GRID
NA_KROWS = 12
NA_K = NA_KROWS * GRID
NA_BLOCKS = GRID // NA_ROWS
NA_VARIANTS = 3
NA_MASK_ROWS = 16
NA_DT = 22


def _cparams(n_axes):
    return pltpu.CompilerParams(
        dimension_semantics=("arbitrary",) * n_axes, vmem_limit_bytes=VMEM_LIMIT)


def _const_spec(shape, *lead):
    nd = len(shape)
    return pl.BlockSpec((None,) * len(lead) + tuple(shape), lambda *_: tuple(lead) + (0,) * nd,
                        pipeline_mode=pl.Buffered(1))


def _mod_spec(layer, group_of):
    return pl.BlockSpec((None, None, N_MOD, D), lambda *ids: (layer, group_of(*ids), 0, 0))


def _silu(x):
    return x * jax.nn.sigmoid(x)


def _prenorm(x, g, mod_ref, k):
    shift = mod_ref[3 * k:3 * k + 1, :]
    scale = mod_ref[3 * k + 1:3 * k + 2, :]
    y = x * lax.rsqrt(jnp.mean(x * x, axis=-1, keepdims=True) + EPS)
    return ((y * g) * (1.0 + scale) + shift).astype(BF16)


def _postnorm(y, g):
    return (y * lax.rsqrt(jnp.mean(y * y, axis=-1, keepdims=True) + EPS)) * g


def _dot(a, b):
    return jnp.dot(a, b, preferred_element_type=F32)


def _dot_nt(a, b):
    return lax.dot_general(a, b, (((1,), (1,)), ((), ())), preferred_element_type=F32)


def _mod_body(cond_ref, w_ref, b_ref, o_ref):
    s = _silu(cond_ref[...]).astype(BF16)
    o_ref[...] = _dot(s, w_ref[...].astype(BF16)) + b_ref[...]


def _modulation(conds, w_mod, b_mod):
    n = N_MOD * D
    return pl.pallas_call(
        _mod_body,
        grid=(DEPTH, n // MOD_TN),
        in_specs=[
            pl.BlockSpec((8, D), lambda l, j: (0, 0)),
            pl.BlockSpec((None, D, MOD_TN), lambda l, j: (l, 0, j)),
            pl.BlockSpec((None, 1, MOD_TN), lambda l, j: (l, 0, j)),
        ],
        out_specs=pl.BlockSpec((None, 8, MOD_TN), lambda l, j: (l, 0, j)),
        out_shape=jax.ShapeDtypeStruct((DEPTH, 8, n), F32),
        compiler_params=_cparams(2),
        name="modulation",
    )(conds, w_mod, b_mod.reshape(DEPTH, 1, n))


def _ffn_tile(load_x, mod_ref, gpre_ref, gpost_ref, w1_ref, w2_ref, o_ref, k, tile):
    ck = D_FF // FFN_NC
    gate = mod_ref[3 * k + 2:3 * k + 3, :]
    tm, sub = tile
    for r in range(tm // sub):
        rows = slice(r * sub, (r + 1) * sub)
        x = load_x(rows)
        h = _prenorm(x, gpre_ref[...], mod_ref, k)
        y = None
        for c in range(FFN_NC):
            g = _dot(h, w1_ref[:, c * ck:(c + 1) * ck])
            u = _dot(h, w1_ref[:, D_FF + c * ck:D_FF + (c + 1) * ck])
            a = (_silu(g) * u).astype(BF16)
            yc = _dot(a, w2_ref[c * ck:(c + 1) * ck, :])
            y = yc if y is None else y + yc
        o_ref[rows, :] = x + (0.5 * gate) * _postnorm(y, gpost_ref[...])


def _ffn_body(*refs, k, tile, tile0, tiles_a, attn, convert):
    refs = list(refs)
    x_ref = refs.pop(0)
    xb_ref = refs.pop(0) if tiles_a is not None else None
    ao_ref = refs.pop(0) if attn else None
    mod_ref, gpre_ref, gpost_ref, w1_ref, w2_ref = refs[:5]
    refs = refs[5:]
    if attn:
        gmix_ref, wout_ref = refs[:2]
        refs = refs[2:]
    if convert:
        nw1_ref, nw2_ref, o_ref, nw1b_ref, nw2b_ref = refs
        nw1b_ref[...] = nw1_ref[...].astype(BF16)
        nw2b_ref[...] = nw2_ref[...].astype(BF16)
    else:
        (o_ref,) = refs
    tile_id = tile0 + pl.program_id(0)
    tail = (mod_ref, gpre_ref, gpost_ref, w1_ref, w2_ref, o_ref, k, tile)

    def plain(rows):
        if xb_ref is None:
            return x_ref[rows, :]
        return jnp.where(tile_id < tiles_a, x_ref[rows, :], xb_ref[rows, :])

    def mixed(rows):
        y = _dot(ao_ref[rows, :], wout_ref[...])
        return x_ref[rows, :] + mod_ref[5:6, :] * _postnorm(y, gmix_ref[...])

    ctx_tiles = GROUP_ROWS // tile[0]
    if not attn:
        _ffn_tile(plain, *tail)
    elif tile0 >= ctx_tiles:
        _ffn_tile(mixed, *tail)
    else:
        pl.when(tile_id < ctx_tiles)(lambda: _ffn_tile(plain, *tail))
        pl.when(tile_id >= ctx_tiles)(lambda: _ffn_tile(mixed, *tail))


def _half_ffn(x, w1b, w2b, mods, pre, post, layer, slot, *, x_b=None, first_row=0, n_rows=T_ALL,
              dst_rows=T_ALL, dst_row=None, attn=None, convert=None):
    k = 2 * slot
    tile = FFN_TILE_LARGE if (x_b is None and not attn) else FFN_TILE_SMALL
    tm = tile[0]
    tiles_per_group = GROUP_ROWS // tm
    n_tiles = n_rows // tm
    t0 = first_row // tm
    d0 = t0 if dst_row is None else dst_row // tm
    tiles_a = None
    in_specs = [pl.BlockSpec((tm, D), lambda i: (t0 + i, 0))]
    args = [x]
    if x_b is not None:
        tiles_a = x.shape[0] // tm
        in_specs = [pl.BlockSpec((tm, D), lambda i: (jnp.minimum(i, tiles_a - 1), 0)),
                    pl.BlockSpec((tm, D), lambda i: (jnp.maximum(i - tiles_a, 0), 0))]
        args.append(x_b)
    if attn:
        in_specs.append(pl.BlockSpec(
            (tm, D), lambda i: (jnp.maximum(t0 + i - tiles_per_group, 0), 0)))
        args.append(attn[0])
    in_specs += [
        _mod_spec(layer, lambda i: (t0 + i) // tiles_per_group),
        _const_spec((1, D), layer, k),
        _const_spec((1, D), layer, k),
        _const_spec((D, 2 * D_FF)),
        _const_spec((D_FF, D)),
    ]
    args += [mods, pre, post, w1b, w2b]
    if attn:
        in_specs += [_const_spec((1, D), layer, 1), _const_spec((D, D), layer // 2)]
        args += [post, attn[1]]
    out_specs = [pl.BlockSpec((tm, D), lambda i: (d0 + i, 0))]
    out_shape = [jax.ShapeDtypeStruct((dst_rows, D), F32)]
    if convert:
        nw1, nw2, nl, ns = convert
        slabs = FFN_CONV_SLABS * (2 if n_tiles >= 2 * FFN_CONV_SLABS else 1)
        assert n_tiles >= slabs
        c1, r2 = 2 * D_FF // slabs, D_FF // slabs
        last = slabs - 1
        in_specs += [
            pl.BlockSpec((None, None, D, c1), lambda i: (nl, ns, 0, jnp.minimum(i, last))),
            pl.BlockSpec((None, None, r2, D), lambda i: (nl, ns, jnp.minimum(i, last), 0)),
        ]
        args += [nw1, nw2]
        out_specs += [pl.BlockSpec((D, c1), lambda i: (0, jnp.minimum(i, last))),
                      pl.BlockSpec((r2, D), lambda i: (jnp.minimum(i, last), 0))]
        out_shape += [jax.ShapeDtypeStruct((D, 2 * D_FF), BF16),
                      jax.ShapeDtypeStruct((D_FF, D), BF16)]
    res = pl.pallas_call(
        functools.partial(_ffn_body, k=k, tile=tile, tile0=t0, tiles_a=tiles_a, attn=bool(attn),
                          convert=bool(convert)),
        grid=(n_tiles,),
        in_specs=in_specs,
        out_specs=out_specs,
        out_shape=out_shape,
        compiler_params=_cparams(1),
        name="half_ffn",
    )(*args)
    return (res[0], res[1], res[2]) if convert else (res[0], None, None)


def _dft_cos_sin(n):
    idx = np.arange(n)
    ang = 2.0 * np.pi * ((idx[:, None] * idx[None, :]) % n) / n
    return np.cos(ang), np.sin(ang)


def _fourier_constants():
    c128, s128 = _dft_cos_sin(FG_DIM)
    chan = np.stack([c128, s128]) / np.sqrt(FG_DIM)

    c256, s256 = _dft_cos_sin(SEQ)
    dft_prompt = np.concatenate([c256, -s256], axis=0) / np.sqrt(SEQ)

    c64, s64 = _dft_cos_sin(GRID)
    c64 = c64 / np.sqrt(GRID)
    s64 = s64 / np.sqrt(GRID)
    eye = np.eye(FS_SUB)
    rows = GRID * FS_SUB
    m1 = np.concatenate([np.kron(c64, eye), np.kron(-s64, eye)], axis=0)
    def spread(f):
        return np.einsum("kn,jl->kjln", f, eye).reshape(rows, rows)
    m2 = np.block([[spread(c64), spread(s64)], [spread(-s64), spread(c64)]])
    k1 = np.arange(GRID)[:, None]
    n2 = np.arange(GRID)[None, :]
    ang = 2.0 * np.pi * (k1 * n2) / LAT
    tw = np.stack([np.cos(ang), np.sin(ang)])[..., None] * np.ones((1, 1, 1, 128))

    def mxu_const(a):
        return jnp.asarray(a, F32).astype(BF16)

    return (jnp.asarray(chan, F32), mxu_const(dft_prompt), mxu_const(m1), mxu_const(m2),
            jnp.asarray(tw, F32))


def _fold_body(cs_ref, w_ref, o_ref):
    for g in range(FOLD_ROWS // FG_DIM):
        rows = slice(g * FG_DIM, (g + 1) * FG_DIM)
        w = w_ref[rows, :]
        for t in range(2):
            o_ref[t, rows, :] = jnp.dot(cs_ref[t], w, preferred_element_type=F32,
                                        precision=lax.Precision.HIGHEST).astype(BF16)


def _fold_channel_dft(chan, w_out):
    n_layers = w_out.shape[0]
    return pl.pallas_call(
        _fold_body,
        grid=(n_layers, D // FOLD_ROWS),
        in_specs=[
            pl.BlockSpec((2, FG_DIM, FG_DIM), lambda l, r: (0, 0, 0)),
            pl.BlockSpec((None, FOLD_ROWS, D), lambda l, r: (l, r, 0)),
        ],
        out_specs=pl.BlockSpec((None, 2, FOLD_ROWS, D), lambda l, r: (l, 0, r, 0)),
        out_shape=jax.ShapeDtypeStruct((n_layers, 2, D, D), BF16),
        compiler_params=_cparams(2),
        name="fold_channel_dft",
    )(chan, w_out)


def _four_prompt_body(x_ref, mod_ref, gpre_ref, gpost_ref, win_ref, dft_ref, w2_ref, o_ref):
    x = x_ref[...]
    h = _prenorm(x, gpre_ref[...], mod_ref, 1)
    u = _dot(h, win_ref[...]).astype(BF16)
    p = [_dot(dft_ref[...], u[nb * SEQ:(nb + 1) * SEQ]) for nb in range(FP_NB)]
    re = jnp.concatenate([t[:SEQ] for t in p], axis=0).astype(BF16)
    im = jnp.concatenate([t[SEQ:] for t in p], axis=0).astype(BF16)
    y = _dot(re, w2_ref[0]) + _dot(im, w2_ref[1])
    o_ref[...] = x + mod_ref[5:6, :] * _postnorm(y, gpost_ref[...])


def _fourier_prompt(x, mods, pre, post, w_in, dft_prompt, w2, layer):
    return pl.pallas_call(
        _four_prompt_body,
        grid=(BATCH // FP_NB,),
        in_specs=[
            pl.BlockSpec((FP_NB * SEQ, D), lambda b: (b, 0)),
            _mod_spec(layer, lambda b: 0),
            _const_spec((1, D), layer, 1),
            _const_spec((1, D), layer, 1),
            _const_spec((D, D), layer // 2),
            _const_spec((2 * SEQ, SEQ)),
            _const_spec((2, D, D), layer // 2),
        ],
        out_specs=pl.BlockSpec((FP_NB * SEQ, D), lambda b: (b, 0)),
        out_shape=jax.ShapeDtypeStruct((T_ALL, D), F32),
        input_output_aliases={0: 0},
        compiler_params=_cparams(1),
        name="fourier_prompt",
    )(x, mods, pre, post, w_in, dft_prompt, w2)


def _four_s1_body(x_ref, mod_ref, gpre_ref, win_ref, m1_ref, tw_ref, y_ref, y_scr):
    rows = GRID * FS_SUB
    for s in range(FS_TN // FS_SUB):
        cols = slice(s * FS_SUB, (s + 1) * FS_SUB)
        x = x_ref[:, cols, :].reshape(rows, D)
        h = _prenorm(x, gpre_ref[...], mod_ref, 1)
        u = _dot(h, win_ref[...]).astype(BF16)
        y = _dot(m1_ref[...], u)
        tc = tw_ref[0, :, cols, :].reshape(rows, 128)
        ts = tw_ref[1, :, cols, :].reshape(rows, 128)
        for l in range(D // 128):
            sl = slice(l * 128, (l + 1) * 128)
            yr = y[:rows, sl]
            yi = y[rows:, sl]
            y_scr[0, :, cols, sl] = (yr * tc + yi * ts).reshape(GRID, FS_SUB, 128)
            y_scr[1, :, cols, sl] = (yi * tc - yr * ts).reshape(GRID, FS_SUB, 128)
    y_ref[...] = y_scr[...].astype(BF16)


def _fourier_latent_stage1(x4, mods, pre, w_in, m1, tw, layer):
    rows = GRID * FS_SUB
    return pl.pallas_call(
        _four_s1_body,
        grid=(N_LAT, GRID // FS_TN),
        in_specs=[
            pl.BlockSpec((None, GRID, FS_TN, D), lambda b, j: (b + 1, 0, j, 0)),
            _mod_spec(layer, lambda b, j: b + 1),
            _const_spec((1, D), layer, 1),
            _const_spec((D, D), layer // 2),
            _const_spec((2 * rows, rows)),
            pl.BlockSpec((2, GRID, FS_TN, 128), lambda b, j: (0, 0, j, 0)),
        ],
        out_specs=pl.BlockSpec((None, 2, GRID, FS_TN, D), lambda b, j: (b, 0, 0, j, 0)),
        out_shape=jax.ShapeDtypeStruct((N_LAT, 2, GRID, GRID, D), BF16),
        scratch_shapes=[pltpu.VMEM((2, GRID, FS_TN, D), F32)],
        compiler_params=_cparams(2),
        name="fourier_latent_stage1",
    )(x4, mods, pre, w_in, m1, tw)


def _four_s2_body(y_ref, x_ref, mod_ref, gpost_ref, m2_ref, w2_ref, o_ref):
    rows = GRID * FS_SUB
    yin = y_ref[...].reshape(2 * rows, D)
    z = _dot(m2_ref[...], yin)
    y = _dot(z[:rows].astype(BF16), w2_ref[0]) + _dot(z[rows:].astype(BF16), w2_ref[1])
    x = x_ref[...].reshape(rows, D)
    out = x + mod_ref[5:6, :] * _postnorm(y, gpost_ref[...])
    o_ref[...] = out.reshape(GRID, FS_SUB, D)


def _fourier_latent_stage2(y5, x4, mods, post, m2, w2, layer):
    rows = GRID * FS_SUB
    return pl.pallas_call(
        _four_s2_body,
        grid=(N_LAT, GRID // FS_SUB),
        in_specs=[
            pl.BlockSpec((None, 2, FS_SUB, GRID, D), lambda b, j: (b, 0, j, 0, 0)),
            pl.BlockSpec((None, GRID, FS_SUB, D), lambda b, j: (b + 1, 0, j, 0)),
            _mod_spec(layer, lambda b, j: b + 1),
            _const_spec((1, D), layer, 1),
            _const_spec((2 * rows, 2 * rows)),
            _const_spec((2, D, D), layer // 2),
        ],
        out_specs=pl.BlockSpec((None, GRID, FS_SUB, D), lambda b, j: (b + 1, 0, j, 0)),
        out_shape=jax.ShapeDtypeStruct((N_GROUPS, GRID, GRID, D), F32),
        input_output_aliases={1: 0},
        compiler_params=_cparams(2),
        name="fourier_latent_stage2",
    )(y5, x4, mods, post, m2, w2)


def _ctx_attn_body(x_ref, mod_ref, gpre_ref, gpost_ref, wqkv_ref, wout_ref, *rest, fresh_slot):
    o_ref, kc_ref, vc_ref = rest[-3:]
    x = x_ref[...]
    h = _prenorm(x, gpre_ref[...], mod_ref, 1)
    qkv = _dot(h, wqkv_ref[...])
    k = qkv[:, D:2 * D]
    v = qkv[:, 2 * D:]
    for nb in range(CTX_NB):
        rows = slice(nb * SEQ, (nb + 1) * SEQ)
        for hd in range(N_HEADS):
            sl = slice(hd * HEAD_DIM, (hd + 1) * HEAD_DIM)
            if fresh_slot is None:
                kc_ref[nb, hd] = k[rows, sl]
                vc_ref[nb, hd] = v[rows, sl]
            else:
                kc_ref[nb, fresh_slot, hd] = k[rows, sl]
                vc_ref[nb, fresh_slot, hd] = v[rows, sl]
        if fresh_slot is not None:
            other = jnp.zeros((N_HEADS, SEQ, HEAD_DIM), F32)
            kc_ref[nb, 1 - fresh_slot] = other
            vc_ref[nb, 1 - fresh_slot] = other
    qb = (qkv[:, :D] * Q_SCALE).astype(BF16)
    kb = k.astype(BF16)
    vb = v.astype(BF16)
    first = lax.broadcasted_iota(jnp.int32, (1, 2 * HEAD_DIM), 1) < HEAD_DIM
    zero = jnp.zeros((), BF16)
    seqs = []
    for nb in range(CTX_NB):
        rows = slice(nb * SEQ, (nb + 1) * SEQ)
        outs = []
        for p in range(N_PAIRS):
            sl = slice(p * 2 * HEAD_DIM, (p + 1) * 2 * HEAD_DIM)
            qp = qb[rows, sl]
            q2 = jnp.concatenate([jnp.where(first, qp, zero), jnp.where(first, zero, qp)], axis=0)
            s = _dot_nt(q2, kb[rows, sl])
            e = jnp.exp2(s - jnp.max(s, axis=-1, keepdims=True))
            p = (e / jnp.sum(e, axis=-1, keepdims=True)).astype(BF16)
            o2 = _dot(p, vb[rows, sl])
            outs.append(jnp.where(first, o2[:SEQ], o2[SEQ:]))
        seqs.append(jnp.concatenate(outs, axis=1).astype(BF16))
    y = _dot(jnp.concatenate(seqs, axis=0), wout_ref[...])
    o_ref[...] = x + mod_ref[5:6, :] * _postnorm(y, gpost_ref[...])


def _context_attention(x, mods, pre, post, w_qkv, w_out, layer, caches):
    j = layer // 2
    rows = CTX_NB * SEQ
    cache = jax.ShapeDtypeStruct((BATCH, DEPTH // 2, N_HEADS, SEQ, HEAD_DIM), F32)
    if caches:
        cache_spec = pl.BlockSpec((CTX_NB, None, N_HEADS, SEQ, HEAD_DIM), lambda b: (b, j, 0, 0, 0))
    else:
        cache_spec = pl.BlockSpec((CTX_NB, DEPTH // 2, N_HEADS, SEQ, HEAD_DIM),
                                  lambda b: (b, 0, 0, 0, 0))
    in_specs = [
        pl.BlockSpec((rows, D), lambda b: (b, 0)),
        _mod_spec(layer, lambda b: 0),
        _const_spec((1, D), layer, 1),
        _const_spec((1, D), layer, 1),
        _const_spec((D, 3 * D), j),
        _const_spec((D, D), j),
    ]
    aliases = {0: 0}
    if caches:
        in_specs += [pl.BlockSpec(memory_space=pl.ANY)] * 2
        aliases.update({6: 1, 7: 2})
    return pl.pallas_call(
        functools.partial(_ctx_attn_body, fresh_slot=None if caches else j),
        grid=(BATCH // CTX_NB,),
        in_specs=in_specs,
        out_specs=[pl.BlockSpec((rows, D), lambda b: (b, 0)), cache_spec, cache_spec],
        out_shape=[jax.ShapeDtypeStruct((T_ALL, D), F32), cache, cache],
        input_output_aliases=aliases,
        compiler_params=_cparams(1),
        name="context_attention",
    )(x, mods, pre, post, w_qkv, w_out, *caches)


def _qkv_lat_body(x_ref, mod_ref, gpre_ref, wq_ref, wkt_ref, wv_ref, q_ref, kt_ref, v_ref):
    h = _prenorm(x_ref[...], gpre_ref[...], mod_ref, 1)
    q_ref[...] = (_dot(h, wq_ref[...]) * Q_SCALE).astype(BF16)
    v_ref[...] = _dot(h, wv_ref[...]).astype(BF16)
    kt_ref[...] = _dot_nt(wkt_ref[...], h).astype(BF16)


def _qkv_latent(x, mods, pre, w_qkv, wkt, layer):
    j = layer // 2
    per_lat = LAT // QKV_TM
    first = GROUP_ROWS // QKV_TM
    tok = jax.ShapeDtypeStruct((N_LAT * LAT, D), BF16)

    def w_cols(col_block):
        return pl.BlockSpec((None, D, D), lambda i: (j, 0, col_block), pipeline_mode=pl.Buffered(1))

    return pl.pallas_call(
        _qkv_lat_body,
        grid=(N_LAT * per_lat,),
        in_specs=[
            pl.BlockSpec((QKV_TM, D), lambda i: (first + i, 0)),
            _mod_spec(layer, lambda i: 1 + i // per_lat),
            _const_spec((1, D), layer, 1),
            w_cols(0),
            _const_spec((D, D), j),
            w_cols(2),
        ],
        out_specs=[
            pl.BlockSpec((QKV_TM, D), lambda i: (i, 0)),
            pl.BlockSpec((None, D, QKV_TM), lambda i: (i // per_lat, 0, i % per_lat)),
            pl.BlockSpec((QKV_TM, D), lambda i: (i, 0)),
        ],
        out_shape=[tok, jax.ShapeDtypeStruct((N_LAT, D, LAT), BF16), tok],
        compiler_params=_cparams(1),
        name="qkv_latent",
    )(x, mods, pre, w_qkv, wkt, w_qkv)


def _na_block_geometry(blk):
    return min(max(NA_ROWS * blk - WIN_ROWS // 2, 0), GRID - NA_KROWS)


def _na_variant(blk):
    return 0 if blk == 0 else (2 if blk == NA_BLOCKS - 1 else 1)


def _na_tables():
    t = np.arange(NA_DT)[:, None]
    e = np.arange(2)[None, :]
    drow = t - 11 + e
    row_ok = (drow >= -(WIN_ROWS - 1)) & (drow <= WIN_ROWS - 1)
    c = np.arange(GRID)[:, None]
    cp = np.arange(GRID)[None, :]
    cs = np.clip(c - WIN_COLS // 2, 0, GRID - WIN_COLS)
    col_ok = (cp >= cs) & (cp < cs + WIN_COLS)
    ok = row_ok[:, None, :, None] & col_ok[None, :, None, :]
    rowmask = np.zeros((NA_MASK_ROWS, NA_K), np.float32)
    delta0 = np.zeros((NA_VARIANTS, NA_ROWS, NA_KROWS // 2), np.int64)
    seen = set()
    for blk in range(NA_BLOCKS):
        v = _na_variant(blk)
        u0 = _na_block_geometry(blk)
        for rl in range(NA_ROWS):
            r = NA_ROWS * blk + rl
            rs = min(max(r - WIN_ROWS // 2, 0), GRID - WIN_ROWS)
            kr = u0 + np.arange(NA_KROWS)
            valid = (kr >= rs) & (kr < rs + WIN_ROWS)
            mask = np.repeat(np.where(valid, 0.0, NEG), GRID).astype(np.float32)
            d0 = u0 - r + 2 * np.arange(NA_KROWS // 2) + 11
            if (v, rl) in seen:
                assert (rowmask[v * NA_ROWS + rl] == mask).all() and (delta0[v, rl] == d0).all()
            seen.add((v, rl))
            rowmask[v * NA_ROWS + rl] = mask
            delta0[v, rl] = d0
    assert delta0.min() >= 0 and delta0.max() < NA_DT
    pairmask = np.where(ok, 0.0, NEG).astype(np.float32).reshape(NA_DT, GRID, 2 * GRID)
    return pairmask, rowmask, delta0


def _na_bias_rows(rpb):
    nrow = 2 * WIN_ROWS - 1
    fill = jnp.full((N_HEADS, nrow, 2 * GRID - (2 * WIN_COLS - 1)), NEG, F32)
    p = jnp.concatenate([rpb[:, :, WIN_COLS - 1:] * LOG2E, fill, rpb[:, :, :WIN_COLS - 1] * LOG2E],
                        axis=-1)
    return jnp.concatenate([p, jnp.full((N_HEADS, 1, 2 * GRID), NEG, F32)], axis=1)


def _na_expand_bias(p_ref, pm_ref, rm_ref, pair_scr, bias_scr, delta0):
    nrow = 2 * WIN_ROWS - 1
    left = lax.broadcasted_iota(jnp.int32, (1, 2 * GRID), 1) < GRID
    for hh in range(2):
        toep = [pltpu.roll(jnp.broadcast_to(p_ref[hh, i:i + 1, :], (GRID, 2 * GRID)), 0, 1,
                           stride=1, stride_axis=0) for i in range(nrow + 1)]
        shifted = [pltpu.roll(t, GRID, 1) for t in toep]
        for t in range(NA_DT):
            i0, i1 = (d + WIN_ROWS - 1 if abs(d) < WIN_ROWS else nrow for d in (t - 11, t - 10))
            pair_scr[hh, t] = jnp.where(left, toep[i0], shifted[i1]) + pm_ref[t]
        for var in range(NA_VARIANTS):
            for rl in range(NA_ROWS):
                strip = jnp.concatenate(
                    [pair_scr[hh, int(delta0[var, rl, i])] for i in range(NA_KROWS // 2)], axis=1)
                row = var * NA_ROWS + rl
                bias_scr[var, hh, rl * GRID:(rl + 1) * GRID, :] = strip + rm_ref[row:row + 1, :]


def _natten_body(q_ref, kt_ref, v_ref, kct_ref, vc_ref, p_ref, pm_ref, rm_ref, o_ref,
                 pair_scr, bias_scr, s_scr, p_scr, *, delta0):
    @pl.when(pl.program_id(1) == 0)
    def _():
        _na_expand_bias(p_ref, pm_ref, rm_ref, pair_scr, bias_scr, delta0)

    first = lax.broadcasted_iota(jnp.int32, (1, 2 * HEAD_DIM), 1) < HEAD_DIM
    zero = jnp.zeros((), BF16)
    kct = kct_ref[...]
    vc = vc_ref[...]

    lane_id = lax.broadcasted_iota(jnp.int32, (1, 2 * HEAD_DIM), 1)
    ones_blk = jnp.broadcast_to(jnp.where(lane_id == 0, 1.0, 0.0).astype(BF16),
                                (NA_K + PAST, 2 * HEAD_DIM))

    def offsets(blk):
        koff = pl.multiple_of(jnp.clip(blk - 1, 0, NA_BLOCKS - 3) * NA_Q, NA_Q)
        qoff = pl.multiple_of(blk * NA_Q, NA_Q)
        return koff, qoff

    def logits(blk, slot):
        koff, qoff = offsets(blk)
        var = jnp.where(blk == 0, 0, jnp.where(blk == NA_BLOCKS - 1, 2, 1))
        qb = q_ref[pl.ds(qoff, NA_Q), :]
        ktb = kt_ref[:, pl.ds(koff, NA_K)]
        q2 = jnp.concatenate([jnp.where(first, qb, zero), jnp.where(first, zero, qb)], axis=0)
        s_scr[slot, :, :, :NA_K] = _dot(q2, ktb).reshape(2, NA_Q, NA_K) + bias_scr[var]
        s_scr[slot, :, :, NA_K:] = _dot(q2, kct).reshape(2, NA_Q, PAST)

    def softmax(slot):
        s = s_scr[slot]
        p_scr[slot] = jnp.exp2(s - jnp.max(s, axis=-1, keepdims=True)).astype(BF16)

    def values(blk, slot):
        koff, qoff = offsets(blk)
        v_all = jnp.concatenate([v_ref[pl.ds(koff, NA_K), :], vc], axis=0)
        p2 = p_scr[slot].reshape(2 * NA_Q, NA_K + PAST)
        o2 = _dot(p2, jnp.concatenate([v_all, ones_blk], axis=1))
        r = o2[:, :2 * HEAD_DIM] / o2[:, 2 * HEAD_DIM:2 * HEAD_DIM + 1]
        o_ref[pl.ds(qoff, NA_Q), :] = jnp.where(first, r[:NA_Q], r[NA_Q:]).astype(BF16)

    logits(0, 0)
    softmax(0)
    logits(1, 1)

    def steady(it, carry):
        t = 2 + 2 * it
        logits(t, 0)
        values(t - 2, 0)
        softmax(1)
        logits(t + 1, 1)
        values(t - 1, 1)
        softmax(0)
        return carry

    lax.fori_loop(0, (NA_BLOCKS - 2) // 2, steady, 0, unroll=True)
    values(NA_BLOCKS - 2, 0)
    softmax(1)
    values(NA_BLOCKS - 1, 1)


def _neighbourhood_attention(q, kt, v, kct, vc, bias_rows, pairmask, rowmask, delta0):
    lane = 2 * HEAD_DIM
    return pl.pallas_call(
        functools.partial(_natten_body, delta0=delta0),
        grid=(N_PAIRS, N_LAT),
        in_specs=[
            pl.BlockSpec((LAT, lane), lambda p, b: (b, p)),
            pl.BlockSpec((None, lane, LAT), lambda p, b: (b, p, 0)),
            pl.BlockSpec((LAT, lane), lambda p, b: (b, p)),
            pl.BlockSpec((None, lane, PAST), lambda p, b: (b, p, 0)),
            pl.BlockSpec((None, PAST, lane), lambda p, b: (b, 0, p)),
            pl.BlockSpec((2, 2 * WIN_ROWS, lane), lambda p, b: (p, 0, 0)),
            pl.BlockSpec((NA_DT, GRID, lane), lambda p, b: (0, 0, 0)),
            pl.BlockSpec((NA_MASK_ROWS, NA_K), lambda p, b: (0, 0)),
        ],
        out_specs=pl.BlockSpec((LAT, lane), lambda p, b: (b, p)),
        out_shape=jax.ShapeDtypeStruct((N_LAT * LAT, D), BF16),
        scratch_shapes=[
            pltpu.VMEM((2, NA_DT, GRID, lane), F32),
            pltpu.VMEM((NA_VARIANTS, 2, NA_Q, NA_K), F32),
            pltpu.VMEM((2, 2, NA_Q, NA_K + PAST), F32),
            pltpu.VMEM((2, 2, NA_Q, NA_K + PAST), BF16),
        ],
        compiler_params=_cparams(2),
        name="neighbourhood_attention",
    )(q, kt, v, kct, vc, bias_rows, pairmask, rowmask)


def kernel(x_prompt, x_sample, cache_k, cache_v, c, c_ctx, w_mod, b_mod, norm_pre, norm_post,
           ffn_w1, ffn_w2, four_w_in, four_w_out, na_w_qkv, na_w_out, na_rpb):
    conds = jnp.concatenate([c_ctx[None, :], c, jnp.zeros((8 - 1 - N_LAT, D), F32)], axis=0)
    mods = _modulation(conds, w_mod, b_mod)[:, :N_GROUPS].reshape(DEPTH, N_GROUPS, N_MOD, D)

    chan, dft_prompt, m1, m2, tw = _fourier_constants()
    pairmask, rowmask, delta0 = _na_tables()
    pairmask = jnp.asarray(pairmask)
    rowmask = jnp.asarray(rowmask)

    w_in = four_w_in.astype(BF16)
    wf = _fold_channel_dft(chan, four_w_out)
    w_qkv = na_w_qkv.astype(BF16)
    w_out = na_w_out.astype(BF16)
    wkt = w_qkv[:, :, D:2 * D].transpose(0, 2, 1)
    pre = norm_pre[:, :, None, :]
    post = norm_post[:, :, None, :]

    n_ctx = BATCH * SEQ
    n_lat = N_LAT * LAT
    ffn = functools.partial(_half_ffn, mods=mods, pre=pre, post=post)
    w1b = ffn_w1[0, 0].astype(BF16)
    w2b = ffn_w2[0, 0].astype(BF16)
    caches = ()
    for i in range(DEPTH):
        j = i // 2
        nxt = (ffn_w1, ffn_w2, i, 1)
        if i == 0:
            x, w1b, w2b = ffn(x_prompt.reshape(n_ctx, D), w1b, w2b, layer=0, slot=0,
                              x_b=x_sample.reshape(n_lat, D), convert=nxt)
        else:
            x, w1b, w2b = ffn(x, w1b, w2b, layer=i, slot=0, convert=nxt)
        attn = None
        if i % 2 == 0:
            x = _fourier_prompt(x, mods, pre, post, w_in, dft_prompt, wf, i)
            x4 = x.reshape(N_GROUPS, GRID, GRID, D)
            y5 = _fourier_latent_stage1(x4, mods, pre, w_in, m1, tw, i)
            x4 = _fourier_latent_stage2(y5, x4, mods, post, m2, wf, i)
            x = x4.reshape(T_ALL, D)
        else:
            x, *caches = _context_attention(x, mods, pre, post, w_qkv, w_out, i, caches)
            q, kt, v = _qkv_latent(x, mods, pre, w_qkv, wkt, i)
            kct = cache_k[:, j].transpose(0, 1, 3, 2).reshape(N_LAT, D, PAST).astype(BF16)
            vct = cache_v[:, j].transpose(0, 2, 1, 3).reshape(N_LAT, PAST, D).astype(BF16)
            o = _neighbourhood_attention(q, kt, v, kct, vct, _na_bias_rows(na_rpb[j]), pairmask,
                                         rowmask, delta0)
            attn = (o, w_out)
        if i < DEPTH - 1:
            x, w1b, w2b = ffn(x, w1b, w2b, layer=i, slot=1, attn=attn,
                              convert=(ffn_w1, ffn_w2, i + 1, 0))

    last = DEPTH - 1
    y_prompt, _, _ = ffn(x, w1b, w2b, layer=last, slot=1, n_rows=n_ctx, dst_rows=n_ctx)
    y_sample, _, _ = ffn(x, w1b, w2b, layer=last, slot=1, first_row=n_ctx, n_rows=n_lat,
                         dst_rows=n_lat, dst_row=0, attn=attn)
    return (y_prompt.reshape(BATCH, SEQ, D), y_sample.reshape(N_LAT, LAT, D), caches[0], caches[1])
```

```python
import functools

import numpy as np
import jax
import jax.numpy as jnp
from jax import lax
from jax.experimental import pallas as pl
from jax.experimental.pallas import tpu as pltpu

F32 = jnp.float32
BF16 = jnp.bfloat16

D = 1024
D_FF = 2816
DEPTH = 4
N_MOD = 9
N_HEADS = 16
HEAD_DIM = 64
N_PAIRS = N_HEADS // 2
SEQ = 256
BATCH = 16
GRID = 64
LAT = GRID * GRID
N_LAT = 2
GROUP_ROWS = 4096
N_GROUPS = 3
T_ALL = N_GROUPS * GROUP_ROWS
PAST = 256
WIN_ROWS = 8
WIN_COLS = 16
N_FG = 8
FG_DIM = D // N_FG
EPS = 1e-6
NEG = -1e30
LOG2E = 1.4426950408889634
Q_SCALE = HEAD_DIM ** -0.5 * LOG2E

VMEM_LIMIT = 56 * 1024 * 1024

FFN_TILE_LARGE = (1024, 256)
FFN_TILE_SMALL = (512, 256)
FFN_NC = 1
FFN_CONV_SLABS = 11
MOD_TN = 2304
FOLD_ROWS = 512
QKV_TM = 1024
FP_NB = 4
CTX_NB_CREATE = 1
CTX_NB_UPDATE = 2
FS_TN = 16
FS_SUB = 8
NA_ROWS = 4
NA_Q = NA_ROWS * GRID
NA_KROWS = 12
NA_K = NA_KROWS * GRID
NA_BLOCKS = GRID // NA_ROWS
NA_VARIANTS = 3
NA_MASK_ROWS = 16
NA_DT = 22


def _cparams(n_axes):
    return pltpu.CompilerParams(
        dimension_semantics=("arbitrary",) * n_axes, vmem_limit_bytes=VMEM_LIMIT)


def _const_spec(shape, *lead):
    nd = len(shape)
    return pl.BlockSpec((None,) * len(lead) + tuple(shape), lambda *_: tuple(lead) + (0,) * nd,
                        pipeline_mode=pl.Buffered(1))


def _mod_spec(layer, group_of):
    return pl.BlockSpec((None, None, N_MOD, D), lambda *ids: (layer, group_of(*ids), 0, 0))


def _silu(x):
    return x * jax.nn.sigmoid(x)


def _prenorm(x, g, mod_ref, k):
    shift = mod_ref[3 * k:3 * k + 1, :]
    scale = mod_ref[3 * k + 1:3 * k + 2, :]
    y = x * lax.rsqrt(jnp.mean(x * x, axis=-1, keepdims=True) + EPS)
    return ((y * g) * (1.0 + scale) + shift).astype(BF16)


def _postnorm(y, g):
    return (y * lax.rsqrt(jnp.mean(y * y, axis=-1, keepdims=True) + EPS)) * g


def _dot(a, b):
    return jnp.dot(a, b, preferred_element_type=F32)


def _dot_nt(a, b):
    return lax.dot_general(a, b, (((1,), (1,)), ((), ())), preferred_element_type=F32)


def _mod_body(cond_ref, w_ref, b_ref, o_ref):
    s = _silu(cond_ref[...]).astype(BF16)
    o_ref[...] = _dot(s, w_ref[...].astype(BF16)) + b_ref[...]


def _modulation(conds, w_mod, b_mod):
    n = N_MOD * D
    return pl.pallas_call(
        _mod_body,
        grid=(DEPTH, n // MOD_TN),
        in_specs=[
            pl.BlockSpec((8, D), lambda l, j: (0, 0)),
            pl.BlockSpec((None, D, MOD_TN), lambda l, j: (l, 0, j)),
            pl.BlockSpec((None, 1, MOD_TN), lambda l, j: (l, 0, j)),
        ],
        out_specs=pl.BlockSpec((None, 8, MOD_TN), lambda l, j: (l, 0, j)),
        out_shape=jax.ShapeDtypeStruct((DEPTH, 8, n), F32),
        compiler_params=_cparams(2),
        name="modulation",
    )(conds, w_mod, b_mod.reshape(DEPTH, 1, n))


def _ffn_tile(load_x, mod_ref, gpre_ref, gpost_ref, w1_ref, w2_ref, o_ref, k, tile):
    ck = D_FF // FFN_NC
    gate = mod_ref[3 * k + 2:3 * k + 3, :]
    tm, sub = tile
    for r in range(tm // sub):
        rows = slice(r * sub, (r + 1) * sub)
        x = load_x(rows)
        h = _prenorm(x, gpre_ref[...], mod_ref, k)
        y = None
        for c in range(FFN_NC):
            g = _dot(h, w1_ref[:, c * ck:(c + 1) * ck])
            u = _dot(h, w1_ref[:, D_FF + c * ck:D_FF + (c + 1) * ck])
            a = (_silu(g) * u).astype(BF16)
            yc = _dot(a, w2_ref[c * ck:(c + 1) * ck, :])
            y = yc if y is None else y + yc
        o_ref[rows, :] = x + (0.5 * gate) * _postnorm(y, gpost_ref[...])


def _ffn_body(*refs, k, tile, tile0, tiles_a, attn, convert):
    refs = list(refs)
    x_ref = refs.pop(0)
    xb_ref = refs.pop(0) if tiles_a is not None else None
    ao_ref = refs.pop(0) if attn else None
    mod_ref, gpre_ref, gpost_ref, w1_ref, w2_ref = refs[:5]
    refs = refs[5:]
    if attn:
        gmix_ref, wout_ref = refs[:2]
        refs = refs[2:]
    if convert:
        nw1_ref, nw2_ref, o_ref, nw1b_ref, nw2b_ref = refs
        nw1b_ref[...] = nw1_ref[...].astype(BF16)
        nw2b_ref[...] = nw2_ref[...].astype(BF16)
    else:
        (o_ref,) = refs
    tile_id = tile0 + pl.program_id(0)
    tail = (mod_ref, gpre_ref, gpost_ref, w1_ref, w2_ref, o_ref, k, tile)

    def plain(rows):
        if xb_ref is None:
            return x_ref[rows, :]
        return jnp.where(tile_id < tiles_a, x_ref[rows, :], xb_ref[rows, :])

    def mixed(rows):
        y = _dot(ao_ref[rows, :], wout_ref[...])
        return x_ref[rows, :] + mod_ref[5:6, :] * _postnorm(y, gmix_ref[...])

    ctx_tiles = GROUP_ROWS // tile[0]
    if not attn:
        _ffn_tile(plain, *tail)
    elif tile0 >= ctx_tiles:
        _ffn_tile(mixed, *tail)
    else:
        pl.when(tile_id < ctx_tiles)(lambda: _ffn_tile(plain, *tail))
        pl.when(tile_id >= ctx_tiles)(lambda: _ffn_tile(mixed, *tail))


def _half_ffn(x, w1b, w2b, mods, pre, post, layer, slot, *, x_b=None, first_row=0, n_rows=T_ALL,
              dst_rows=T_ALL, dst_row=None, attn=None, convert=None):
    k = 2 * slot
    tile = FFN_TILE_LARGE if (x_b is None and not (attn and convert)) else FFN_TILE_SMALL
    tm = tile[0]
    tiles_per_group = GROUP_ROWS // tm
    n_tiles = n_rows // tm
    t0 = first_row // tm
    d0 = t0 if dst_row is None else dst_row // tm
    tiles_a = None
    in_specs = [pl.BlockSpec((tm, D), lambda i: (t0 + i, 0))]
    args = [x]
    if x_b is not None:
        tiles_a = x.shape[0] // tm
        in_specs = [pl.BlockSpec((tm, D), lambda i: (jnp.minimum(i, tiles_a - 1), 0)),
                    pl.BlockSpec((tm, D), lambda i: (jnp.maximum(i - tiles_a, 0), 0))]
        args.append(x_b)
    if attn:
        in_specs.append(pl.BlockSpec(
            (tm, D), lambda i: (jnp.maximum(t0 + i - tiles_per_group, 0), 0)))
        args.append(attn[0])
    in_specs += [
        _mod_spec(layer, lambda i: (t0 + i) // tiles_per_group),
        _const_spec((1, D), layer, k),
        _const_spec((1, D), layer, k),
        _const_spec((D, 2 * D_FF)),
        _const_spec((D_FF, D)),
    ]
    args += [mods, pre, post, w1b, w2b]
    if attn:
        in_specs += [_const_spec((1, D), layer, 1), _const_spec((D, D), layer // 2)]
        args += [post, attn[1]]
    out_specs = [pl.BlockSpec((tm, D), lambda i: (d0 + i, 0))]
    out_shape = [jax.ShapeDtypeStruct((dst_rows, D), F32)]
    if convert:
        nw1, nw2, nl, ns = convert
        slabs = FFN_CONV_SLABS * (2 if n_tiles >= 2 * FFN_CONV_SLABS else 1)
        assert n_tiles >= slabs
        c1, r2 = 2 * D_FF // slabs, D_FF // slabs
        last = slabs - 1
        in_specs += [
            pl.BlockSpec((None, None, D, c1), lambda i: (nl, ns, 0, jnp.minimum(i, last))),
            pl.BlockSpec((None, None, r2, D), lambda i: (nl, ns, jnp.minimum(i, last), 0)),
        ]
        args += [nw1, nw2]
        out_specs += [pl.BlockSpec((D, c1), lambda i: (0, jnp.minimum(i, last))),
                      pl.BlockSpec((r2, D), lambda i: (jnp.minimum(i, last), 0))]
        out_shape += [jax.ShapeDtypeStruct((D, 2 * D_FF), BF16),
                      jax.ShapeDtypeStruct((D_FF, D), BF16)]
    res = pl.pallas_call(
        functools.partial(_ffn_body, k=k, tile=tile, tile0=t0, tiles_a=tiles_a, attn=bool(attn),
                          convert=bool(convert)),
        grid=(n_tiles,),
        in_specs=in_specs,
        out_specs=out_specs,
        out_shape=out_shape,
        compiler_params=_cparams(1),
        name="half_ffn",
    )(*args)
    return (res[0], res[1], res[2]) if convert else (res[0], None, None)


def _dft_cos_sin(n):
    idx = np.arange(n)
    ang = 2.0 * np.pi * ((idx[:, None] * idx[None, :]) % n) / n
    return np.cos(ang), np.sin(ang)


def _fourier_constants():
    c128, s128 = _dft_cos_sin(FG_DIM)
    chan = np.stack([c128, s128]) / np.sqrt(FG_DIM)

    c256, s256 = _dft_cos_sin(SEQ)
    dft_prompt = np.concatenate([c256, -s256], axis=0) / np.sqrt(SEQ)

    c64, s64 = _dft_cos_sin(GRID)
    c64 = c64 / np.sqrt(GRID)
    s64 = s64 / np.sqrt(GRID)
    eye = np.eye(FS_SUB)
    rows = GRID * FS_SUB
    m1 = np.concatenate([np.kron(c64, eye), np.kron(-s64, eye)], axis=0)
    def spread(f):
        return np.einsum("kn,jl->kjln", f, eye).reshape(rows, rows)
    m2 = np.block([[spread(c64), spread(s64)], [spread(-s64), spread(c64)]])
    k1 = np.arange(GRID)[:, None]
    n2 = np.arange(GRID)[None, :]
    ang = 2.0 * np.pi * (k1 * n2) / LAT
    tw = np.stack([np.cos(ang), np.sin(ang)])[..., None] * np.ones((1, 1, 1, 128))

    def mxu_const(a):
        return jnp.asarray(a, F32).astype(BF16)

    return (jnp.asarray(chan, F32), mxu_const(dft_prompt), mxu_const(m1), mxu_const(m2),
            jnp.asarray(tw, F32))


def _fold_body(cs_ref, w_ref, o_ref):
    for g in range(FOLD_ROWS // FG_DIM):
        rows = slice(g * FG_DIM, (g + 1) * FG_DIM)
        w = w_ref[rows, :]
        for t in range(2):
            o_ref[t, rows, :] = jnp.dot(cs_ref[t], w, preferred_element_type=F32,
                                        precision=lax.Precision.HIGHEST).astype(BF16)


def _fold_channel_dft(chan, w_out):
    n_layers = w_out.shape[0]
    return pl.pallas_call(
        _fold_body,
        grid=(n_layers, D // FOLD_ROWS),
        in_specs=[
            pl.BlockSpec((2, FG_DIM, FG_DIM), lambda l, r: (0, 0, 0)),
            pl.BlockSpec((None, FOLD_ROWS, D), lambda l, r: (l, r, 0)),
        ],
        out_specs=pl.BlockSpec((None, 2, FOLD_ROWS, D), lambda l, r: (l, 0, r, 0)),
        out_shape=jax.ShapeDtypeStruct((n_layers, 2, D, D), BF16),
        compiler_params=_cparams(2),
        name="fold_channel_dft",
    )(chan, w_out)


def _four_prompt_body(x_ref, mod_ref, gpre_ref, gpost_ref, win_ref, dft_ref, w2_ref, o_ref):
    x = x_ref[...]
    h = _prenorm(x, gpre_ref[...], mod_ref, 1)
    u = _dot(h, win_ref[...]).astype(BF16)
    p = [_dot(dft_ref[...], u[nb * SEQ:(nb + 1) * SEQ]) for nb in range(FP_NB)]
    re = jnp.concatenate([t[:SEQ] for t in p], axis=0).astype(BF16)
    im = jnp.concatenate([t[SEQ:] for t in p], axis=0).astype(BF16)
    y = _dot(re, w2_ref[0]) + _dot(im, w2_ref[1])
    o_ref[...] = x + mod_ref[5:6, :] * _postnorm(y, gpost_ref[...])


def _fourier_prompt(x, mods, pre, post, w_in, dft_prompt, w2, layer):
    return pl.pallas_call(
        _four_prompt_body,
        grid=(BATCH // FP_NB,),
        in_specs=[
            pl.BlockSpec((FP_NB * SEQ, D), lambda b: (b, 0)),
            _mod_spec(layer, lambda b: 0),
            _const_spec((1, D), layer, 1),
            _const_spec((1, D), layer, 1),
            _const_spec((D, D), layer // 2),
            _const_spec((2 * SEQ, SEQ)),
            _const_spec((2, D, D), layer // 2),
        ],
        out_specs=pl.BlockSpec((FP_NB * SEQ, D), lambda b: (b, 0)),
        out_shape=jax.ShapeDtypeStruct((T_ALL, D), F32),
        input_output_aliases={0: 0},
        compiler_params=_cparams(1),
        name="fourier_prompt",
    )(x, mods, pre, post, w_in, dft_prompt, w2)


def _four_s1_body(x_ref, mod_ref, gpre_ref, win_ref, m1_ref, tw_ref, y_ref, y_scr):
    rows = GRID * FS_SUB
    for s in range(FS_TN // FS_SUB):
        cols = slice(s * FS_SUB, (s + 1) * FS_SUB)
        x = x_ref[:, cols, :].reshape(rows, D)
        h = _prenorm(x, gpre_ref[...], mod_ref, 1)
        u = _dot(h, win_ref[...]).astype(BF16)
        y = _dot(m1_ref[...], u)
        tc = tw_ref[0, :, cols, :].reshape(rows, 128)
        ts = tw_ref[1, :, cols, :].reshape(rows, 128)
        for l in range(D // 128):
            sl = slice(l * 128, (l + 1) * 128)
            yr = y[:rows, sl]
            yi = y[rows:, sl]
            y_scr[0, :, cols, sl] = (yr * tc + yi * ts).reshape(GRID, FS_SUB, 128)
            y_scr[1, :, cols, sl] = (yi * tc - yr * ts).reshape(GRID, FS_SUB, 128)
    y_ref[...] = y_scr[...].astype(BF16)


def _fourier_latent_stage1(x4, mods, pre, w_in, m1, tw, layer):
    rows = GRID * FS_SUB
    return pl.pallas_call(
        _four_s1_body,
        grid=(N_LAT, GRID // FS_TN),
        in_specs=[
            pl.BlockSpec((None, GRID, FS_TN, D), lambda b, j: (b + 1, 0, j, 0)),
            _mod_spec(layer, lambda b, j: b + 1),
            _const_spec((1, D), layer, 1),
            _const_spec((D, D), layer // 2),
            _const_spec((2 * rows, rows)),
            pl.BlockSpec((2, GRID, FS_TN, 128), lambda b, j: (0, 0, j, 0)),
        ],
        out_specs=pl.BlockSpec((None, 2, GRID, FS_TN, D), lambda b, j: (b, 0, 0, j, 0)),
        out_shape=jax.ShapeDtypeStruct((N_LAT, 2, GRID, GRID, D), BF16),
        scratch_shapes=[pltpu.VMEM((2, GRID, FS_TN, D), F32)],
        compiler_params=_cparams(2),
        name="fourier_latent_stage1",
    )(x4, mods, pre, w_in, m1, tw)


def _four_s2_body(y_ref, x_ref, mod_ref, gpost_ref, m2_ref, w2_ref, o_ref):
    rows = GRID * FS_SUB
    yin = y_ref[...].reshape(2 * rows, D)
    z = _dot(m2_ref[...], yin)
    y = _dot(z[:rows].astype(BF16), w2_ref[0]) + _dot(z[rows:].astype(BF16), w2_ref[1])
    x = x_ref[...].reshape(rows, D)
    out = x + mod_ref[5:6, :] * _postnorm(y, gpost_ref[...])
    o_ref[...] = out.reshape(GRID, FS_SUB, D)


def _fourier_latent_stage2(y5, x4, mods, post, m2, w2, layer):
    rows = GRID * FS_SUB
    return pl.pallas_call(
        _four_s2_body,
        grid=(N_LAT, GRID // FS_SUB),
        in_specs=[
            pl.BlockSpec((None, 2, FS_SUB, GRID, D), lambda b, j: (b, 0, j, 0, 0)),
            pl.BlockSpec((None, GRID, FS_SUB, D), lambda b, j: (b + 1, 0, j, 0)),
            _mod_spec(layer, lambda b, j: b + 1),
            _const_spec((1, D), layer, 1),
            _const_spec((2 * rows, 2 * rows)),
            _const_spec((2, D, D), layer // 2),
        ],
        out_specs=pl.BlockSpec((None, GRID, FS_SUB, D), lambda b, j: (b + 1, 0, j, 0)),
        out_shape=jax.ShapeDtypeStruct((N_GROUPS, GRID, GRID, D), F32),
        input_output_aliases={1: 0},
        compiler_params=_cparams(2),
        name="fourier_latent_stage2",
    )(y5, x4, mods, post, m2, w2)


def _ctx_attn_body(x_ref, mod_ref, gpre_ref, gpost_ref, wqkv_ref, wout_ref, *rest, fresh_slot, n_seq):
    o_ref, kc_ref, vc_ref = rest[-3:]
    x = x_ref[...]
    h = _prenorm(x, gpre_ref[...], mod_ref, 1)
    qkv = _dot(h, wqkv_ref[...])
    k = qkv[:, D:2 * D]
    v = qkv[:, 2 * D:]
    for nb in range(n_seq):
        rows = slice(nb * SEQ, (nb + 1) * SEQ)
        for hd in range(N_HEADS):
            sl = slice(hd * HEAD_DIM, (hd + 1) * HEAD_DIM)
            if fresh_slot is None:
                kc_ref[nb, hd] = k[rows, sl]
                vc_ref[nb, hd] = v[rows, sl]
            else:
                kc_ref[nb, fresh_slot, hd] = k[rows, sl]
                vc_ref[nb, fresh_slot, hd] = v[rows, sl]
        if fresh_slot is not None:
            other = jnp.zeros((N_HEADS, SEQ, HEAD_DIM), F32)
            kc_ref[nb, 1 - fresh_slot] = other
            vc_ref[nb, 1 - fresh_slot] = other
    qb = (qkv[:, :D] * Q_SCALE).astype(BF16)
    kb = k.astype(BF16)
    vb = v.astype(BF16)
    first = lax.broadcasted_iota(jnp.int32, (1, 2 * HEAD_DIM), 1) < HEAD_DIM
    zero = jnp.zeros((), BF16)
    seqs = []
    for nb in range(n_seq):
        rows = slice(nb * SEQ, (nb + 1) * SEQ)
        outs = []
        for p in range(N_PAIRS):
            sl = slice(p * 2 * HEAD_DIM, (p + 1) * 2 * HEAD_DIM)
            qp = qb[rows, sl]
            q2 = jnp.concatenate([jnp.where(first, qp, zero), jnp.where(first, zero, qp)], axis=0)
            s = _dot_nt(q2, kb[rows, sl])
            e = jnp.exp2(s - jnp.max(s, axis=-1, keepdims=True))
            p = (e / jnp.sum(e, axis=-1, keepdims=True)).astype(BF16)
            o2 = _dot(p, vb[rows, sl])
            outs.append(jnp.where(first, o2[:SEQ], o2[SEQ:]))
        seqs.append(jnp.concatenate(outs, axis=1).astype(BF16))
    y = _dot(jnp.concatenate(seqs, axis=0), wout_ref[...])
    o_ref[...] = x + mod_ref[5:6, :] * _postnorm(y, gpost_ref[...])


def _context_attention(x, mods, pre, post, w_qkv, w_out, layer, caches):
    j = layer // 2
    n_seq = CTX_NB_UPDATE if caches else CTX_NB_CREATE
    rows = n_seq * SEQ
    cache = jax.ShapeDtypeStruct((BATCH, DEPTH // 2, N_HEADS, SEQ, HEAD_DIM), F32)
    if caches:
        cache_spec = pl.BlockSpec((n_seq, None, N_HEADS, SEQ, HEAD_DIM), lambda b: (b, j, 0, 0, 0))
    else:
        cache_spec = pl.BlockSpec((n_seq, DEPTH // 2, N_HEADS, SEQ, HEAD_DIM),
                                  lambda b: (b, 0, 0, 0, 0))
    in_specs = [
        pl.BlockSpec((rows, D), lambda b: (b, 0)),
        _mod_spec(layer, lambda b: 0),
        _const_spec((1, D), layer, 1),
        _const_spec((1, D), layer, 1),
        _const_spec((D, 3 * D), j),
        _const_spec((D, D), j),
    ]
    aliases = {0: 0}
    if caches:
        in_specs += [pl.BlockSpec(memory_space=pl.ANY)] * 2
        aliases.update({6: 1, 7: 2})
    return pl.pallas_call(
        functools.partial(_ctx_attn_body, fresh_slot=None if caches else j, n_seq=n_seq),
        grid=(BATCH // n_seq,),
        in_specs=in_specs,
        out_specs=[pl.BlockSpec((rows, D), lambda b: (b, 0)), cache_spec, cache_spec],
        out_shape=[jax.ShapeDtypeStruct((T_ALL, D), F32), cache, cache],
        input_output_aliases=aliases,
        compiler_params=_cparams(1),
        name="context_attention",
    )(x, mods, pre, post, w_qkv, w_out, *caches)


def _qkv_lat_body(x_ref, mod_ref, gpre_ref, wq_ref, wkt_ref, wv_ref, q_ref, kt_ref, v_ref):
    h = _prenorm(x_ref[...], gpre_ref[...], mod_ref, 1)
    q_ref[...] = (_dot(h, wq_ref[...]) * Q_SCALE).astype(BF16)
    v_ref[...] = _dot(h, wv_ref[...]).astype(BF16)
    kt_ref[...] = _dot_nt(wkt_ref[...], h).astype(BF16)


def _qkv_latent(x, mods, pre, w_qkv, wkt, layer):
    j = layer // 2
    per_lat = LAT // QKV_TM
    first = GROUP_ROWS // QKV_TM
    tok = jax.ShapeDtypeStruct((N_LAT * LAT, D), BF16)

    def w_cols(col_block):
        return pl.BlockSpec((None, D, D), lambda i: (j, 0, col_block), pipeline_mode=pl.Buffered(1))

    return pl.pallas_call(
        _qkv_lat_body,
        grid=(N_LAT * per_lat,),
        in_specs=[
            pl.BlockSpec((QKV_TM, D), lambda i: (first + i, 0)),
            _mod_spec(layer, lambda i: 1 + i // per_lat),
            _const_spec((1, D), layer, 1),
            w_cols(0),
            _const_spec((D, D), j),
            w_cols(2),
        ],
        out_specs=[
            pl.BlockSpec((QKV_TM, D), lambda i: (i, 0)),
            pl.BlockSpec((None, D, QKV_TM), lambda i: (i // per_lat, 0, i % per_lat)),
            pl.BlockSpec((QKV_TM, D), lambda i: (i, 0)),
        ],
        out_shape=[tok, jax.ShapeDtypeStruct((N_LAT, D, LAT), BF16), tok],
        compiler_params=_cparams(1),
        name="qkv_latent",
    )(x, mods, pre, w_qkv, wkt, w_qkv)


def _na_block_geometry(blk):
    return min(max(NA_ROWS * blk - WIN_ROWS // 2, 0), GRID - NA_KROWS)


def _na_variant(blk):
    return 0 if blk == 0 else (2 if blk == NA_BLOCKS - 1 else 1)


def _na_tables():
    t = np.arange(NA_DT)[:, None]
    e = np.arange(2)[None, :]
    drow = t - 11 + e
    row_ok = (drow >= -(WIN_ROWS - 1)) & (drow <= WIN_ROWS - 1)
    c = np.arange(GRID)[:, None]
    cp = np.arange(GRID)[None, :]
    cs = np.clip(c - WIN_COLS // 2, 0, GRID - WIN_COLS)
    col_ok = (cp >= cs) & (cp < cs + WIN_COLS)
    ok = row_ok[:, None, :, None] & col_ok[None, :, None, :]
    rowmask = np.zeros((NA_MASK_ROWS, NA_K), np.float32)
    delta0 = np.zeros((NA_VARIANTS, NA_ROWS, NA_KROWS // 2), np.int64)
    seen = set()
    for blk in range(NA_BLOCKS):
        v = _na_variant(blk)
        u0 = _na_block_geometry(blk)
        for rl in range(NA_ROWS):
            r = NA_ROWS * blk + rl
            rs = min(max(r - WIN_ROWS // 2, 0), GRID - WIN_ROWS)
            kr = u0 + np.arange(NA_KROWS)
            valid = (kr >= rs) & (kr < rs + WIN_ROWS)
            mask = np.repeat(np.where(valid, 0.0, NEG), GRID).astype(np.float32)
            d0 = u0 - r + 2 * np.arange(NA_KROWS // 2) + 11
            if (v, rl) in seen:
                assert (rowmask[v * NA_ROWS + rl] == mask).all() and (delta0[v, rl] == d0).all()
            seen.add((v, rl))
            rowmask[v * NA_ROWS + rl] = mask
            delta0[v, rl] = d0
    assert delta0.min() >= 0 and delta0.max() < NA_DT
    pairmask = np.where(ok, 0.0, NEG).astype(np.float32).reshape(NA_DT, GRID, 2 * GRID)
    return pairmask, rowmask, delta0


def _na_bias_rows(rpb):
    nrow = 2 * WIN_ROWS - 1
    fill = jnp.full((N_HEADS, nrow, 2 * GRID - (2 * WIN_COLS - 1)), NEG, F32)
    p = jnp.concatenate([rpb[:, :, WIN_COLS - 1:] * LOG2E, fill, rpb[:, :, :WIN_COLS - 1] * LOG2E],
                        axis=-1)
    return jnp.concatenate([p, jnp.full((N_HEADS, 1, 2 * GRID), NEG, F32)], axis=1)


def _na_expand_bias(p_ref, pm_ref, rm_ref, pair_scr, bias_scr, delta0):
    nrow = 2 * WIN_ROWS - 1
    left = lax.broadcasted_iota(jnp.int32, (1, 2 * GRID), 1) < GRID
    for hh in range(2):
        toep = [pltpu.roll(jnp.broadcast_to(p_ref[hh, i:i + 1, :], (GRID, 2 * GRID)), 0, 1,
                           stride=1, stride_axis=0) for i in range(nrow + 1)]
        shifted = [pltpu.roll(t, GRID, 1) for t in toep]
        for t in range(NA_DT):
            i0, i1 = (d + WIN_ROWS - 1 if abs(d) < WIN_ROWS else nrow for d in (t - 11, t - 10))
            pair_scr[hh, t] = jnp.where(left, toep[i0], shifted[i1]) + pm_ref[t]
        for var in range(NA_VARIANTS):
            for rl in range(NA_ROWS):
                strip = jnp.concatenate(
                    [pair_scr[hh, int(delta0[var, rl, i])] for i in range(NA_KROWS // 2)], axis=1)
                row = var * NA_ROWS + rl
                bias_scr[var, hh, rl * GRID:(rl + 1) * GRID, :] = strip + rm_ref[row:row + 1, :]


def _natten_body(q_ref, kt_ref, v_ref, kct_ref, vc_ref, p_ref, pm_ref, rm_ref, o_ref,
                 pair_scr, bias_scr, s_scr, p_scr, *, delta0):
    @pl.when(pl.program_id(1) == 0)
    def _():
        _na_expand_bias(p_ref, pm_ref, rm_ref, pair_scr, bias_scr, delta0)

    first = lax.broadcasted_iota(jnp.int32, (1, 2 * HEAD_DIM), 1) < HEAD_DIM
    zero = jnp.zeros((), BF16)
    kct = kct_ref[...]
    vc = vc_ref[...]

    lane_id = lax.broadcasted_iota(jnp.int32, (1, 2 * HEAD_DIM), 1)
    ones_blk = jnp.broadcast_to(jnp.where(lane_id == 0, 1.0, 0.0).astype(BF16),
                                (NA_K + PAST, 2 * HEAD_DIM))

    def offsets(blk):
        koff = pl.multiple_of(jnp.clip(blk - 1, 0, NA_BLOCKS - 3) * NA_Q, NA_Q)
        qoff = pl.multiple_of(blk * NA_Q, NA_Q)
        return koff, qoff

    def logits(blk, slot):
        koff, qoff = offsets(blk)
        var = jnp.where(blk == 0, 0, jnp.where(blk == NA_BLOCKS - 1, 2, 1))
        qb = q_ref[pl.ds(qoff, NA_Q), :]
        ktb = kt_ref[:, pl.ds(koff, NA_K)]
        q2 = jnp.concatenate([jnp.where(first, qb, zero), jnp.where(first, zero, qb)], axis=0)
        s_scr[slot, :, :, :NA_K] = _dot(q2, ktb).reshape(2, NA_Q, NA_K) + bias_scr[var]
        s_scr[slot, :, :, NA_K:] = _dot(q2, kct).reshape(2, NA_Q, PAST)

    def softmax(slot):
        s = s_scr[slot]
        p_scr[slot] = jnp.exp2(s - jnp.max(s, axis=-1, keepdims=True)).astype(BF16)

    def values(blk, slot):
        koff, qoff = offsets(blk)
        v_all = jnp.concatenate([v_ref[pl.ds(koff, NA_K), :], vc], axis=0)
        p2 = p_scr[slot].reshape(2 * NA_Q, NA_K + PAST)
        o2 = _dot(p2, jnp.concatenate([v_all, ones_blk], axis=1))
        r = o2[:, :2 * HEAD_DIM] / o2[:, 2 * HEAD_DIM:2 * HEAD_DIM + 1]
        o_ref[pl.ds(qoff, NA_Q), :] = jnp.where(first, r[:NA_Q], r[NA_Q:]).astype(BF16)

    logits(0, 0)
    softmax(0)
    logits(1, 1)

    def steady(it, carry):
        t = 2 + 2 * it
        logits(t, 0)
        values(t - 2, 0)
        softmax(1)
        logits(t + 1, 1)
        values(t - 1, 1)
        softmax(0)
        return carry

    lax.fori_loop(0, (NA_BLOCKS - 2) // 2, steady, 0, unroll=True)
    values(NA_BLOCKS - 2, 0)
    softmax(1)
    values(NA_BLOCKS - 1, 1)


def _neighbourhood_attention(q, kt, v, kct, vc, bias_rows, pairmask, rowmask, delta0):
    lane = 2 * HEAD_DIM
    return pl.pallas_call(
        functools.partial(_natten_body, delta0=delta0),
        grid=(N_PAIRS, N_LAT),
        in_specs=[
            pl.BlockSpec((LAT, lane), lambda p, b: (b, p)),
            pl.BlockSpec((None, lane, LAT), lambda p, b: (b, p, 0)),
            pl.BlockSpec((LAT, lane), lambda p, b: (b, p)),
            pl.BlockSpec((None, lane, PAST), lambda p, b: (b, p, 0)),
            pl.BlockSpec((None, PAST, lane), lambda p, b: (b, 0, p)),
            pl.BlockSpec((2, 2 * WIN_ROWS, lane), lambda p, b: (p, 0, 0)),
            pl.BlockSpec((NA_DT, GRID, lane), lambda p, b: (0, 0, 0)),
            pl.BlockSpec((NA_MASK_ROWS, NA_K), lambda p, b: (0, 0)),
        ],
        out_specs=pl.BlockSpec((LAT, lane), lambda p, b: (b, p)),
        out_shape=jax.ShapeDtypeStruct((N_LAT * LAT, D), BF16),
        scratch_shapes=[
            pltpu.VMEM((2, NA_DT, GRID, lane), F32),
            pltpu.VMEM((NA_VARIANTS, 2, NA_Q, NA_K), F32),
            pltpu.VMEM((2, 2, NA_Q, NA_K + PAST), F32),
            pltpu.VMEM((2, 2, NA_Q, NA_K + PAST), BF16),
        ],
        compiler_params=_cparams(2),
        name="neighbourhood_attention",
    )(q, kt, v, kct, vc, bias_rows, pairmask, rowmask)


def kernel(x_prompt, x_sample, cache_k, cache_v, c, c_ctx, w_mod, b_mod, norm_pre, norm_post,
           ffn_w1, ffn_w2, four_w_in, four_w_out, na_w_qkv, na_w_out, na_rpb):
    conds = jnp.concatenate([c_ctx[None, :], c, jnp.zeros((8 - 1 - N_LAT, D), F32)], axis=0)
    mods = _modulation(conds, w_mod, b_mod)[:, :N_GROUPS].reshape(DEPTH, N_GROUPS, N_MOD, D)

    chan, dft_prompt, m1, m2, tw = _fourier_constants()
    pairmask, rowmask, delta0 = _na_tables()
    pairmask = jnp.asarray(pairmask)
    rowmask = jnp.asarray(rowmask)

    w_in = four_w_in.astype(BF16)
    wf = _fold_channel_dft(chan, four_w_out)
    w_qkv = na_w_qkv.astype(BF16)
    w_out = na_w_out.astype(BF16)
    wkt = w_qkv[:, :, D:2 * D].transpose(0, 2, 1)
    pre = norm_pre[:, :, None, :]
    post = norm_post[:, :, None, :]

    n_ctx = BATCH * SEQ
    n_lat = N_LAT * LAT
    ffn = functools.partial(_half_ffn, mods=mods, pre=pre, post=post)
    w1b = ffn_w1[0, 0].astype(BF16)
    w2b = ffn_w2[0, 0].astype(BF16)
    caches = ()
    for i in range(DEPTH):
        j = i // 2
        nxt = (ffn_w1, ffn_w2, i, 1)
        if i == 0:
            x, w1b, w2b = ffn(x_prompt.reshape(n_ctx, D), w1b, w2b, layer=0, slot=0,
                              x_b=x_sample.reshape(n_lat, D), convert=nxt)
        else:
            x, w1b, w2b = ffn(x, w1b, w2b, layer=i, slot=0, convert=nxt)
        attn = None
        if i % 2 == 0:
            x = _fourier_prompt(x, mods, pre, post, w_in, dft_prompt, wf, i)
            x4 = x.reshape(N_GROUPS, GRID, GRID, D)
            y5 = _fourier_latent_stage1(x4, mods, pre, w_in, m1, tw, i)
            x4 = _fourier_latent_stage2(y5, x4, mods, post, m2, wf, i)
            x = x4.reshape(T_ALL, D)
        else:
            x, *caches = _context_attention(x, mods, pre, post, w_qkv, w_out, i, caches)
            q, kt, v = _qkv_latent(x, mods, pre, w_qkv, wkt, i)
            kct = cache_k[:, j].transpose(0, 1, 3, 2).reshape(N_LAT, D, PAST).astype(BF16)
            vct = cache_v[:, j].transpose(0, 2, 1, 3).reshape(N_LAT, PAST, D).astype(BF16)
            o = _neighbourhood_attention(q, kt, v, kct, vct, _na_bias_rows(na_rpb[j]), pairmask,
                                         rowmask, delta0)
            attn = (o, w_out)
        if i < DEPTH - 1:
            x, w1b, w2b = ffn(x, w1b, w2b, layer=i, slot=1, attn=attn,
                              convert=(ffn_w1, ffn_w2, i + 1, 0))

    last = DEPTH - 1
    y_prompt, _, _ = ffn(x, w1b, w2b, layer=last, slot=1, n_rows=n_ctx, dst_rows=n_ctx)
    y_sample, _, _ = ffn(x, w1b, w2b, layer=last, slot=1, first_row=n_ctx, n_rows=n_lat,
                         dst_rows=n_lat, dst_row=0, attn=attn)
    return (y_prompt.reshape(BATCH, SEQ, D), y_sample.reshape(N_LAT, LAT, D), caches[0], caches[1])
```
